```python
import math
import jax, jax.numpy as jnp
from jax import lax
import numpy as np

D_MODEL = 1024
BATCH = 2
SEQ = 16384
DEPTH = 4

GRID_W = 64
CTX_LEN = 256
N_MIXERS = 2
EPS = 1e-6
N_MOD = 6

POOL_WINDOWS = (2, 4, 8, 16)
N_POOL_GROUPS = len(POOL_WINDOWS)
POOL_GROUP = D_MODEL // N_POOL_GROUPS

SSD_EXPAND = 2
D_INNER = SSD_EXPAND * D_MODEL
HEAD_DIM = 64
N_SSD_HEADS = D_INNER // HEAD_DIM
N_SSD_GROUPS = 4
HEADS_PER_GROUP = N_SSD_HEADS // N_SSD_GROUPS
D_STATE = 128
D_CONV = 4
CONV_PAD_LEFT = 2
CONV_PAD_RIGHT = D_CONV - 1 - CONV_PAD_LEFT
CHUNK = 128
BC_DIM = N_SSD_GROUPS * D_STATE
CONV_DIM = D_INNER + 2 * BC_DIM
IN_PROJ_DIM = D_INNER + CONV_DIM + 2 * N_SSD_HEADS

D_FF = 2816
N_EXPERTS = 8
TOP_K = 2
D_FF_EXPERT = 3584

N_EVEN_LAYERS = (DEPTH + 1) // 2
N_ODD_LAYERS = DEPTH // 2

kernel_name = "hybrid_pool_ssd_moe_dit_prefix"


def rmsnorm(x, w):
    xf = x.astype(jnp.float32)
    y = xf * lax.rsqrt(jnp.mean(xf * xf, axis=-1, keepdims=True) + EPS)
    return (y * w.astype(jnp.float32)).astype(x.dtype)


def modulate(h, shift, scale):
    return h * (1 + scale) + shift


def window_mean_minus_self(x, w):
    L = x.shape[1]
    lo = w // 2
    hi = w - 1 - lo
    cs = jnp.pad(jnp.cumsum(x.astype(jnp.float32), axis=1), ((0, 0), (1, 0), (0, 0)))
    t = jnp.arange(L)
    end = jnp.minimum(t + hi + 1, L)
    start = jnp.maximum(t - lo, 0)
    s = jnp.take(cs, end, axis=1) - jnp.take(cs, start, axis=1)
    cnt = (end - start).astype(jnp.float32)
    return (s / cnt[None, :, None]).astype(x.dtype) - x


def pool_mix(h, w_groups, scale):
    parts = [window_mean_minus_self(h[..., g * POOL_GROUP:(g + 1) * POOL_GROUP], w)
             for g, w in enumerate(POOL_WINDOWS)]
    p = jnp.stack(parts, axis=-2)
    y = jnp.einsum('nlgi,gio->nlgo', p, w_groups).reshape(h.shape)
    return y * scale


def dwconv_centred(x, w, b):
    L = x.shape[1]
    xp = jnp.pad(x, ((0, 0), (CONV_PAD_LEFT, CONV_PAD_RIGHT), (0, 0)))
    y = b
    for k in range(D_CONV):
        y = y + xp[:, k:k + L] * w[k]
    return y


def ssd_chunked(x, dt, A, B, C, h0, need_y):
    b, L = x.shape[:2]
    nc = L // CHUNK
    xd = (x * dt[..., None]).reshape(b, nc, CHUNK, N_SSD_GROUPS, HEADS_PER_GROUP, HEAD_DIM)
    a = (dt * A).astype(jnp.float32).reshape(b, nc, CHUNK, N_SSD_GROUPS, HEADS_PER_GROUP)
    a_cum = jnp.cumsum(jnp.transpose(a, (0, 3, 4, 1, 2)), axis=-1)
    Bc = B.reshape(b, nc, CHUNK, N_SSD_GROUPS, D_STATE)
    Cc = C.reshape(b, nc, CHUNK, N_SSD_GROUPS, D_STATE)
    decay_to_end = jnp.exp(a_cum[..., -1:] - a_cum)
    states = jnp.einsum('bcsgn,bgecs,bcsgep->cbgepn', Bc, decay_to_end, xd).astype(jnp.float32)
    chunk_decay = jnp.moveaxis(jnp.exp(a_cum[..., -1]), -1, 0)

    def step(h, inp):
        s, d = inp
        return h * d[..., None, None] + s, h

    h_final, h_init = lax.scan(step, h0, (states, chunk_decay))
    if not need_y:
        return None, h_final
    seg = a_cum[..., :, None] - a_cum[..., None, :]
    lower = jnp.tril(jnp.ones((CHUNK, CHUNK), dtype=bool))
    Lmat = jnp.exp(jnp.where(lower, seg, -jnp.inf))
    CB = jnp.einsum('bclgn,bcsgn->bgcls', Cc, Bc)
    y_diag = jnp.einsum('bgcls,bgecls,bcsgep->bclgep', CB, Lmat, xd)
    y_off = jnp.einsum('bclgn,cbgepn,bgecl->bclgep', Cc, h_init, jnp.exp(a_cum))
    y = (y_diag + y_off).reshape(b, L, N_SSD_HEADS, HEAD_DIM)
    return y, h_final


def ssd_mixer(u_ctx, u_lat, in_w, conv_w, conv_b, A_log, dt_bias, D_skip, norm_w, out_w, ctx_out):
    A = -jnp.exp(A_log.astype(jnp.float32))

    def flip(t):
        return jnp.flip(t, axis=1)

    def bidir(u, h0_f, h0_b, need_y):
        bsz, L = u.shape[:2]
        proj = u @ in_w
        z = proj[..., :D_INNER]
        xbc = jax.nn.silu(dwconv_centred(proj[..., D_INNER:D_INNER + CONV_DIM], conv_w, conv_b))
        dt = jax.nn.softplus(
            proj[..., D_INNER + CONV_DIM:].astype(jnp.float32).reshape(bsz, L, 2, N_SSD_HEADS) + dt_bias)
        xs = xbc[..., :D_INNER].reshape(bsz, L, N_SSD_HEADS, HEAD_DIM)
        Bm = xbc[..., D_INNER:D_INNER + BC_DIM].reshape(bsz, L, N_SSD_GROUPS, D_STATE)
        Cm = xbc[..., D_INNER + BC_DIM:].reshape(bsz, L, N_SSD_GROUPS, D_STATE)
        y_f, hf = ssd_chunked(xs, dt[:, :, 0], A[0], Bm, Cm, h0_f, need_y)
        y_b, hb = ssd_chunked(flip(xs), flip(dt[:, :, 1]), A[1], flip(Bm), flip(Cm), h0_b, need_y)
        if not need_y:
            return None, hf, hb
        y = y_f + flip(y_b) + xs * D_skip[:, None]
        y = y.reshape(bsz, L, D_INNER).astype(u.dtype)
        y = rmsnorm(y * jax.nn.silu(z), norm_w) @ out_w
        return y, hf, hb

    zero = jnp.zeros((u_ctx.shape[0], N_SSD_GROUPS, HEADS_PER_GROUP, HEAD_DIM, D_STATE), jnp.float32)
    y_ctx, hf, hb = bidir(u_ctx, zero, zero, ctx_out)
    y_lat, _, _ = bidir(u_lat, hf, hb, True)
    return y_ctx, y_lat


def swiglu(h, w1, w3, w2):
    return (jax.nn.silu(h @ w1) * (h @ w3)) @ w2


def moe_swiglu(h, router_w, w1, w3, w2):
    logits = (h @ router_w).astype(jnp.float32)
    top_v, top_i = lax.top_k(logits, TOP_K)
    gates = jax.nn.softmax(top_v, axis=-1)
    combine = jnp.sum(jax.nn.one_hot(top_i, N_EXPERTS, dtype=jnp.float32) * gates[..., None], axis=-2)
    combine = combine.astype(h.dtype)
    out = jnp.zeros_like(h)
    for e in range(N_EXPERTS):
        out = out + combine[..., e:e + 1] * swiglu(h, w1[e], w3[e], w2[e])
    return out


def setup_inputs(seed: int = 0) -> dict:
    key = jax.random.key(seed)
    ks = jax.random.split(key, 28)
    f32 = jnp.float32

    def nrm(k, shape, s):
        return jax.random.normal(k, shape, f32) * s

    dt0 = jnp.exp(jax.random.uniform(ks[14], (N_ODD_LAYERS, 2, N_SSD_HEADS), f32,
                                     math.log(1e-3), math.log(1e-1)))
    return {
        "x": nrm(ks[0], (BATCH, SEQ, D_MODEL), 1.0),
        "c": nrm(ks[1], (BATCH, D_MODEL), 1.0),
        "ctx": nrm(ks[2], (BATCH, CTX_LEN, D_MODEL), 1.0),
        "c_ctx": nrm(ks[3], (D_MODEL,), 1.0),
        "ada_w": nrm(ks[4], (DEPTH, D_MODEL, N_MOD * D_MODEL), 0.5 * D_MODEL ** -0.5),
        "ada_b": nrm(ks[5], (DEPTH, N_MOD * D_MODEL), 0.02),
        "norm_mix_w": 1.0 + nrm(ks[6], (DEPTH, D_MODEL), 0.01),
        "norm_ffn_w": 1.0 + nrm(ks[7], (DEPTH, D_MODEL), 0.01),
        "pool_w": nrm(ks[8], (N_EVEN_LAYERS, N_POOL_GROUPS, POOL_GROUP, POOL_GROUP), POOL_GROUP ** -0.5),
        "pool_scale": 1.0 + nrm(ks[9], (N_EVEN_LAYERS, D_MODEL), 0.1),
        "ssd_in_w": nrm(ks[10], (N_ODD_LAYERS, D_MODEL, IN_PROJ_DIM), D_MODEL ** -0.5),
        "ssd_conv_w": nrm(ks[11], (N_ODD_LAYERS, D_CONV, CONV_DIM), D_CONV ** -0.5),
        "ssd_conv_b": nrm(ks[12], (N_ODD_LAYERS, CONV_DIM), 0.02),
        "ssd_A_log": jnp.log(jax.random.uniform(ks[13], (N_ODD_LAYERS, 2, N_SSD_HEADS), f32, 1.0, 16.0)),
        "ssd_dt_bias": dt0 + jnp.log(-jnp.expm1(-dt0)),
        "ssd_D": 1.0 + nrm(ks[15], (N_ODD_LAYERS, N_SSD_HEADS), 0.1),
        "ssd_norm_w": 1.0 + nrm(ks[16], (N_ODD_LAYERS, D_INNER), 0.01),
        "ssd_out_w": nrm(ks[17], (N_ODD_LAYERS, D_INNER, D_MODEL), D_INNER ** -0.5),
        "ffn_w1": nrm(ks[18], (N_EVEN_LAYERS, D_MODEL, D_FF), D_MODEL ** -0.5),
        "ffn_w3": nrm(ks[19], (N_EVEN_LAYERS, D_MODEL, D_FF), D_MODEL ** -0.5),
        "ffn_w2": nrm(ks[20], (N_EVEN_LAYERS, D_FF, D_MODEL), D_FF ** -0.5),
        "moe_router_w": nrm(ks[21], (N_ODD_LAYERS, D_MODEL, N_EXPERTS), D_MODEL ** -0.5),
        "moe_w1": nrm(ks[22], (N_ODD_LAYERS, N_EXPERTS, D_MODEL, D_FF_EXPERT), D_MODEL ** -0.5),
        "moe_w3": nrm(ks[23], (N_ODD_LAYERS, N_EXPERTS, D_MODEL, D_FF_EXPERT), D_MODEL ** -0.5),
        "moe_w2": nrm(ks[24], (N_ODD_LAYERS, N_EXPERTS, D_FF_EXPERT, D_MODEL), D_FF_EXPERT ** -0.5),
        "final_norm_w": 1.0 + nrm(ks[25], (D_MODEL,), 0.01),
    }


def reference(x, c, ctx, c_ctx, ada_w, ada_b, norm_mix_w, norm_ffn_w, pool_w, pool_scale,
              ssd_in_w, ssd_conv_w, ssd_conv_b, ssd_A_log, ssd_dt_bias, ssd_D, ssd_norm_w, ssd_out_w,
              ffn_w1, ffn_w3, ffn_w2, moe_router_w, moe_w1, moe_w3, moe_w2, final_norm_w):
    bsz, n_lat, d = x.shape
    rows = n_lat // GRID_W
    c_act = jax.nn.silu(c)
    cc_act = jax.nn.silu(c_ctx)[None]
    h_lat, h_ctx = x, ctx
    for i in range(DEPTH):
        last = i == DEPTH - 1
        j = i // 2
        mod_lat = jnp.split((c_act @ ada_w[i] + ada_b[i])[:, None, :], N_MOD, axis=-1)
        mod_ctx = jnp.split((cc_act @ ada_w[i] + ada_b[i])[:, None, :], N_MOD, axis=-1)
        sh1, sc1, g1, sh2, sc2, g2 = mod_lat
        sh1c, sc1c, g1c, sh2c, sc2c, g2c = mod_ctx

        u_lat = modulate(rmsnorm(h_lat, norm_mix_w[i]), sh1, sc1)
        if i % N_MIXERS == 0:
            y_lat = pool_mix(u_lat.reshape(bsz * rows, GRID_W, d), pool_w[j], pool_scale[j]).reshape(bsz, n_lat, d)
            if not last:
                u_ctx = modulate(rmsnorm(h_ctx, norm_mix_w[i]), sh1c, sc1c)
                y_ctx = pool_mix(u_ctx, pool_w[j], pool_scale[j])
        else:
            u_ctx = modulate(rmsnorm(h_ctx, norm_mix_w[i]), sh1c, sc1c)
            y_ctx, y_lat = ssd_mixer(u_ctx, u_lat, ssd_in_w[j], ssd_conv_w[j], ssd_conv_b[j], ssd_A_log[j],
                                     ssd_dt_bias[j], ssd_D[j], ssd_norm_w[j], ssd_out_w[j], not last)
        h_lat = h_lat + g1 * y_lat
        if not last:
            h_ctx = h_ctx + g1c * y_ctx

        v_lat = modulate(rmsnorm(h_lat, norm_ffn_w[i]), sh2, sc2)
        if i % 2 == 0:
            h_lat = h_lat + g2 * swiglu(v_lat, ffn_w1[j], ffn_w3[j], ffn_w2[j])
        else:
            h_lat = h_lat + g2 * moe_swiglu(v_lat, moe_router_w[j], moe_w1[j], moe_w3[j], moe_w2[j])
        if not last:
            v_ctx = modulate(rmsnorm(h_ctx, norm_ffn_w[i]), sh2c, sc2c)
            if i % 2 == 0:
                h_ctx = h_ctx + g2c * swiglu(v_ctx, ffn_w1[j], ffn_w3[j], ffn_w2[j])
            else:
                h_ctx = h_ctx + g2c * moe_swiglu(v_ctx, moe_router_w[j], moe_w1[j], moe_w3[j], moe_w2[j])
    return rmsnorm(h_lat, final_norm_w)
```

```python
import functools

import numpy as np
import jax
import jax.numpy as jnp
from jax import lax
from jax.experimental import pallas as pl
from jax.experimental.pallas import tpu as pltpu

F32 = jnp.float32
BF16 = jnp.bfloat16
EPS = 1e-6

BLK = 256
CHUNK = 128
GRID_W = 64
POOL_WINDOWS = (2, 4, 8, 16)
N_MOD = 6
HEAD_DIM = 64
N_HEADS = 32
N_GROUPS = 4
D_STATE = 128
D_CONV = 4
CONV_LEFT = 2
HALO = 8
N_EXPERTS = 8
MOE_TILE = 512
MOE_FCHUNK = 512
VMEM_LIMIT = 56 * 2**20


def _dot(a, b):
    return jnp.dot(a, b, preferred_element_type=F32)


def _split2(x):
    hi = x.astype(BF16)
    lo = (x - hi.astype(F32)).astype(BF16)
    return hi, lo


def _split3(x):
    p0 = x.astype(BF16)
    r = x - p0.astype(F32)
    p1 = r.astype(BF16)
    p2 = (r - p1.astype(F32)).astype(BF16)
    return p0, p1, p2


def _dot_hi(a, b):
    ah, al = _split2(a)
    bh, bl = _split2(b)
    return _dot(ah, bh) + _dot(al, bh) + _dot(ah, bl)


def _sigmoid(x):
    return 1.0 / (1.0 + jnp.exp(-x))


def _silu(x):
    return x * _sigmoid(x)


def _rms(x, w):
    ms = jnp.mean(x * x, axis=-1, keepdims=True)
    return x * lax.rsqrt(ms + EPS) * w


def _params(sem):
    return pltpu.CompilerParams(dimension_semantics=sem, vmem_limit_bytes=VMEM_LIMIT)


def _resident(shape):
    nd = len(shape)
    return pl.BlockSpec(shape, lambda *_: (0,) * nd, pipeline_mode=pl.Buffered(1))


def _block_row(nblk, nctx_blk, nbatch):
    blk = pl.program_id(0)
    b = blk // nblk
    j = blk - b * nblk
    is_ctx = j < nctx_blk
    return jnp.where(is_ctx, nbatch, b), is_ctx, j


def _mod_kernel(c_ref, w_ref, b_ref, o_ref):
    o_ref[0] = _dot_hi(_silu(c_ref[...]), w_ref[0]) + b_ref[0]


def _mod_table(cvec, ada_w, ada_b):
    depth, d, n = ada_w.shape
    tn = 512
    return pl.pallas_call(
        _mod_kernel,
        grid=(depth, n // tn),
        in_specs=[
            pl.BlockSpec((8, d), lambda l, j: (0, 0)),
            pl.BlockSpec((1, d, tn), lambda l, j: (l, 0, j)),
            pl.BlockSpec((1, 1, tn), lambda l, j: (l, 0, j)),
        ],
        out_specs=pl.BlockSpec((1, 8, tn), lambda l, j: (l, 0, j)),
        out_shape=jax.ShapeDtypeStruct((depth, 8, n), F32),
        compiler_params=_params(("arbitrary", "arbitrary")),
        name="mod_table",
    )(cvec, ada_w, ada_b.reshape(depth, 1, n))


def _even_kernel(nblk, nctx_blk, nbatch, d,
                 h_ref, tbl_ref, nw1_ref, nw2_ref, ahi_ref, alo_ref, pw_ref, ps_ref,
                 w1_ref, w3_ref, w2_ref, o_ref):
    row, is_ctx, _ = _block_row(nblk, nctx_blk, nbatch)
    kind = is_ctx.astype(jnp.int32)

    def mod(k):
        return tbl_ref[0, pl.ds(row, 1), k * d:(k + 1) * d]

    h = h_ref[...]
    u = _rms(h, nw1_ref[0]) * (1.0 + mod(1)) + mod(0)
    gw = d // len(POOL_WINDOWS)
    ys = []
    for g in range(len(POOL_WINDOWS)):
        ug = u[:, g * gw:(g + 1) * gw]
        uh, ul = _split2(ug)
        ah = ahi_ref[kind, g]
        al = alo_ref[kind, g]
        p = _dot(ah, uh) + _dot(al, uh) + _dot(ah, ul) - ug
        ys.append(_dot(p.astype(BF16), pw_ref[0, g]))
    y = jnp.concatenate(ys, axis=1) * ps_ref[0]
    h1 = h + mod(2) * y
    v = (_rms(h1, nw2_ref[0]) * (1.0 + mod(4)) + mod(3)).astype(BF16)
    a = _dot(v, w1_ref[0])
    b = _dot(v, w3_ref[0])
    act = (_silu(a) * b).astype(BF16)
    o_ref[...] = h1 + mod(5) * _dot(act, w2_ref[0])


def _pool_matrices():
    mats = np.zeros((2, len(POOL_WINDOWS), BLK, BLK), np.float64)
    for kind, seg in enumerate((GRID_W, BLK)):
        for g, w in enumerate(POOL_WINDOWS):
            lo = w // 2
            hi = w - 1 - lo
            for t in range(BLK):
                base = (t // seg) * seg
                tt = t - base
                start = max(tt - lo, 0)
                end = min(tt + hi + 1, seg)
                mats[kind, g, t, base + start:base + end] = 1.0 / (end - start)
    m32 = jnp.asarray(mats, F32)
    hi = m32.astype(BF16)
    lo = (m32 - hi.astype(F32)).astype(BF16)
    return hi, lo


def _even_layer(h, tbl, layer, nw1, nw2, ahi, alo, pw, ps, w1, w3, w2, geom):
    nt, d = h.shape
    nblk, nctx_blk, nbatch = geom
    dff = w1.shape[-1]
    j = layer // 2
    gw = d // len(POOL_WINDOWS)
    kern = functools.partial(_even_kernel, nblk, nctx_blk, nbatch, d)
    vec = lambda idx: pl.BlockSpec((1, 1, d), lambda i: (idx, 0, 0))
    return pl.pallas_call(
        kern,
        grid=(nt // BLK,),
        in_specs=[
            pl.BlockSpec((BLK, d), lambda i: (i, 0)),
            pl.BlockSpec((1, 8, N_MOD * d), lambda i: (layer, 0, 0)),
            vec(layer), vec(layer),
            _resident(ahi.shape), _resident(alo.shape),
            pl.BlockSpec((1, len(POOL_WINDOWS), gw, gw), lambda i: (j, 0, 0, 0)),
            vec(j),
            pl.BlockSpec((1, d, dff), lambda i: (j, 0, 0), pipeline_mode=pl.Buffered(1)),
            pl.BlockSpec((1, d, dff), lambda i: (j, 0, 0), pipeline_mode=pl.Buffered(1)),
            pl.BlockSpec((1, dff, d), lambda i: (j, 0, 0), pipeline_mode=pl.Buffered(1)),
        ],
        out_specs=pl.BlockSpec((BLK, d), lambda i: (i, 0)),
        out_shape=jax.ShapeDtypeStruct((nt, d), F32),
        compiler_params=_params(("arbitrary",)),
        name=f"pool_ffn_{layer}",
    )(h, tbl, nw1, nw2, ahi, alo, pw, ps, w1, w3, w2)


def _ssd_in_kernel(nblk, nctx_blk, nbatch, d, d_inner,
                   hp_ref, h_ref, hn_ref, tbl_ref, nw_ref, wz_ref, wx_ref, wdf_ref, wdb_ref,
                   cw_ref, cb_ref, dtb_ref,
                   z_ref, xs_ref, bc_ref, dtf_ref, dtb_out_ref, xbc_scr):
    row, _, j = _block_row(nblk, nctx_blk, nbatch)
    first = jnp.logical_or(j == 0, j == nctx_blk)
    last = jnp.logical_or(j == nctx_blk - 1, j == nblk - 1)
    shift = tbl_ref[0, pl.ds(row, 1), 0:d]
    scale = tbl_ref[0, pl.ds(row, 1), d:2 * d]
    nw = nw_ref[0]

    def modn(x):
        return _rms(x, nw) * (1.0 + scale) + shift

    u = modn(h_ref[...]).astype(BF16)
    up = (modn(hp_ref[...]) * jnp.where(first, 0.0, 1.0)).astype(BF16)
    un = (modn(hn_ref[...]) * jnp.where(last, 0.0, 1.0)).astype(BF16)
    wx = wx_ref[0]
    xbc_scr[0:HALO, :] = _dot(up, wx)
    xbc_scr[HALO:HALO + BLK, :] = _dot(u, wx)
    xbc_scr[HALO + BLK:HALO + BLK + HALO, :] = _dot(un, wx)
    acc = cb_ref[0]
    for k in range(D_CONV):
        off = HALO - CONV_LEFT + k
        acc = acc + xbc_scr[off:off + BLK, :] * cw_ref[0, k:k + 1, :]
    xbc = _silu(acc)
    xs_ref[...] = xbc[:, :d_inner]
    bc_ref[...] = xbc[:, d_inner:]
    z_ref[...] = _dot(u, wz_ref[0])

    def softplus(x):
        return jnp.maximum(x, 0.0) + jnp.log1p(jnp.exp(-jnp.abs(x)))

    dtf_ref[...] = softplus(_dot(u, wdf_ref[0]) + dtb_ref[0, 0:1, :])
    dtb_out_ref[...] = softplus(_dot(u, wdb_ref[0]) + dtb_ref[0, 1:2, :])


def _ssd_in(h, tbl, layer, nw, wz, wx, wdf, wdb, cw, cb, dtb, geom):
    nt, d = h.shape
    nblk, nctx_blk, nbatch = geom
    j = layer // 2
    d_inner = wz.shape[-1]
    conv_dim = wx.shape[-1]
    hb = BLK // HALO
    nh = nt // HALO
    kern = functools.partial(_ssd_in_kernel, nblk, nctx_blk, nbatch, d, d_inner)
    res3 = lambda a: pl.BlockSpec((1,) + a.shape[1:], lambda i: (j, 0, 0), pipeline_mode=pl.Buffered(1))
    return pl.pallas_call(
        kern,
        grid=(nt // BLK,),
        in_specs=[
            pl.BlockSpec((HALO, d), lambda i: (jnp.maximum(i * hb - 1, 0), 0)),
            pl.BlockSpec((BLK, d), lambda i: (i, 0)),
            pl.BlockSpec((HALO, d), lambda i: (jnp.minimum((i + 1) * hb, nh - 1), 0)),
            pl.BlockSpec((1, 8, N_MOD * d), lambda i: (layer, 0, 0)),
            pl.BlockSpec((1, 1, d), lambda i: (layer, 0, 0)),
            res3(wz), res3(wx), res3(wdf), res3(wdb),
            pl.BlockSpec((1, D_CONV, conv_dim), lambda i: (j, 0, 0)),
            pl.BlockSpec((1, 1, conv_dim), lambda i: (j, 0, 0)),
            pl.BlockSpec((1, 2, CHUNK), lambda i: (j, 0, 0)),
        ],
        out_specs=[
            pl.BlockSpec((BLK, d_inner), lambda i: (i, 0)),
            pl.BlockSpec((BLK, d_inner), lambda i: (i, 0)),
            pl.BlockSpec((BLK, conv_dim - d_inner), lambda i: (i, 0)),
            pl.BlockSpec((BLK, CHUNK), lambda i: (i, 0)),
            pl.BlockSpec((BLK, CHUNK), lambda i: (i, 0)),
        ],
        out_shape=[
            jax.ShapeDtypeStruct((nt, d_inner), F32),
            jax.ShapeDtypeStruct((nt, d_inner), F32),
            jax.ShapeDtypeStruct((nt, conv_dim - d_inner), F32),
            jax.ShapeDtypeStruct((nt, CHUNK), F32),
            jax.ShapeDtypeStruct((nt, CHUNK), F32),
        ],
        scratch_shapes=[pltpu.VMEM((BLK + 2 * HALO, conv_dim), F32)],
        compiler_params=_params(("arbitrary",)),
        name=f"ssd_in_{layer}",
    )(h, h, h, tbl, nw, wz, wx, wdf, wdb, cw, cb, dtb)


def _ssd_direction(xs, bc, dt, alog, s_ref, tri, e_ref, e3_ref, reverse, dskip):
    t = xs.shape[0]
    lane = lax.broadcasted_iota(jnp.int32, (1, CHUNK), 1)
    a_row = jnp.where(lane < N_HEADS, -jnp.exp(alog), 0.0)
    a = dt * a_row
    p0, p1, p2 = _split3(a)
    cum = _dot(tri, p0) + _dot(tri, p1) + _dot(tri, p2)
    tot = cum[0:1, :] if reverse else cum[t - 1:t, :]
    dte = jnp.exp(tot - cum)
    ecum = jnp.exp(cum)
    cdec = jnp.exp(tot)
    cum_t = cum.T
    dt_t = dt.T

    e = e_ref[...]
    w_x = _dot((dt * dte).astype(BF16), e)
    ec_x = _dot(ecum.astype(BF16), e)
    c0, c1, c2 = _split3(jnp.broadcast_to(cdec, (8, CHUNK)))
    cd_x = (_dot(c0, e) + _dot(c1, e) + _dot(c2, e))[0:1, :]

    q0, q1, q2 = _split3(cum)
    stacked = (q0.astype(F32) + pltpu.roll(q1.astype(F32), N_HEADS, 1)
               + pltpu.roll(q2.astype(F32), 2 * N_HEADS, 1)).astype(BF16)
    colb = _dot(stacked, e3_ref[...])

    li = lax.broadcasted_iota(jnp.int32, (t, t), 0)
    si = lax.broadcasted_iota(jnp.int32, (t, t), 1)
    keep = (si >= li) if reverse else (si <= li)
    lane_p = lax.broadcasted_iota(jnp.int32, (t, 2 * HEAD_DIM), 1)
    first_head = lane_p < HEAD_DIM

    gn = N_GROUPS * D_STATE
    gp = (N_HEADS // N_GROUPS) * HEAD_DIM
    outs = []
    for g in range(N_GROUPS):
        b_g = bc[:, g * D_STATE:(g + 1) * D_STATE].astype(BF16)
        c_g = bc[:, gn + g * D_STATE:gn + (g + 1) * D_STATE].astype(BF16)
        cb = lax.dot_general(c_g, b_g, (((1,), (1,)), ((), ())), preferred_element_type=F32)
        ydiag = []
        for q in range(gp // (2 * HEAD_DIM)):
            h1 = g * (N_HEADS // N_GROUPS) + 2 * q
            ms = []
            for hh in (h1, h1 + 1):
                seg = colb[:, hh * CHUNK:(hh + 1) * CHUNK] - cum_t[hh:hh + 1, :]
                lmat = jnp.exp(jnp.where(keep, seg, -1e30))
                ms.append((cb * lmat * dt_t[hh:hh + 1, :]).astype(BF16))
            m_pair = jnp.concatenate(ms, axis=1)
            x_pair = xs[:, h1 * HEAD_DIM:(h1 + 2) * HEAD_DIM]
            rhs = jnp.concatenate([jnp.where(first_head, x_pair, 0.0),
                                   jnp.where(first_head, 0.0, x_pair)], axis=0).astype(BF16)
            ydiag.append(_dot(m_pair, rhs))
        sl = slice(g * gp, (g + 1) * gp)
        s_old = s_ref[:, sl]
        y_off = _dot(c_g, s_old.astype(BF16)) * ec_x[:, sl]
        y_g = jnp.concatenate(ydiag, axis=1) + y_off
        if dskip is not None:
            y_g = y_g + xs[:, sl] * dskip[:, sl]
        outs.append(y_g)
        xw = (xs[:, sl] * w_x[:, sl]).astype(BF16)
        s_new = lax.dot_general(b_g, xw, (((0,), (0,)), ((), ())), preferred_element_type=F32)
        s_ref[:, sl] = s_old * cd_x[:, sl] + s_new
    return jnp.concatenate(outs, axis=1)


def _ssd_scan_kernel(xf_ref, bcf_ref, dtf_ref, xb_ref, bcb_ref, dtb_ref, alog_ref, dsk_ref,
                     tril_ref, triu_ref, e_ref, e3_ref, yf_ref, yb_ref, sf_ref, sb_ref):
    @pl.when(pl.program_id(1) == 0)
    def _():
        sf_ref[...] = jnp.zeros_like(sf_ref)
        sb_ref[...] = jnp.zeros_like(sb_ref)

    yf_ref[...] = _ssd_direction(xf_ref[...], bcf_ref[...], dtf_ref[...], alog_ref[0, 0:1, :], sf_ref,
                                 tril_ref[...], e_ref, e3_ref, False, dsk_ref[0])
    yb_ref[...] = _ssd_direction(xb_ref[...], bcb_ref[...], dtb_ref[...], alog_ref[0, 1:2, :], sb_ref,
                                 triu_ref[...], e_ref, e3_ref, True, None)


def _scan_constants():
    li = np.arange(CHUNK)[:, None]
    ti = np.arange(CHUNK)[None, :]
    tril = (ti <= li).astype(np.float32)
    triu = (ti >= li).astype(np.float32)
    e = np.zeros((CHUNK, N_HEADS * HEAD_DIM), np.float32)
    e3 = np.zeros((CHUNK, N_HEADS * CHUNK), np.float32)
    for h in range(N_HEADS):
        e[h, h * HEAD_DIM:(h + 1) * HEAD_DIM] = 1.0
        for piece in range(3):
            e3[piece * N_HEADS + h, h * CHUNK:(h + 1) * CHUNK] = 1.0
    return tuple(jnp.asarray(m, BF16) for m in (tril, triu, e, e3))


def _ssd_scan(xs, bc, dtf, dtb, alog, dskip, layer, consts, nbatch, nchunk, ncc):
    nt, d_inner = xs.shape
    bcw = bc.shape[1]
    j = layer // 2
    tril, triu, e, e3 = consts

    def fwd(b, c):
        return (b * nchunk + c, 0)

    def bwd(b, c):
        return (b * nchunk + jnp.where(c < ncc, ncc - 1 - c, nchunk - 1 - (c - ncc)), 0)

    return pl.pallas_call(
        _ssd_scan_kernel,
        grid=(nbatch, nchunk),
        in_specs=[
            pl.BlockSpec((CHUNK, d_inner), fwd), pl.BlockSpec((CHUNK, bcw), fwd), pl.BlockSpec((CHUNK, CHUNK), fwd),
            pl.BlockSpec((CHUNK, d_inner), bwd), pl.BlockSpec((CHUNK, bcw), bwd), pl.BlockSpec((CHUNK, CHUNK), bwd),
            pl.BlockSpec((1, 2, CHUNK), lambda b, c: (j, 0, 0)),
            pl.BlockSpec((1, 1, d_inner), lambda b, c: (j, 0, 0)),
            _resident(tril.shape), _resident(triu.shape), _resident(e.shape), _resident(e3.shape),
        ],
        out_specs=[pl.BlockSpec((CHUNK, d_inner), fwd), pl.BlockSpec((CHUNK, d_inner), bwd)],
        out_shape=[jax.ShapeDtypeStruct((nt, d_inner), F32)] * 2,
        scratch_shapes=[pltpu.VMEM((D_STATE, d_inner), F32)] * 2,
        compiler_params=_params(("arbitrary", "arbitrary")),
        name=f"ssd_scan_{layer}",
    )(xs, bc, dtf, xs, bc, dtb, alog, dskip, tril, triu, e, e3)


def _ssd_out_kernel(nblk, nctx_blk, nbatch, d,
                    yf_ref, yb_ref, z_ref, h_ref, tbl_ref, nw_ref, wo_ref, o_ref):
    row, _, _ = _block_row(nblk, nctx_blk, nbatch)
    y = (yf_ref[...] + yb_ref[...]) * _silu(z_ref[...])
    yn = _rms(y, nw_ref[0]).astype(BF16)
    gate = tbl_ref[0, pl.ds(row, 1), 2 * d:3 * d]
    o_ref[...] = h_ref[...] + gate * _dot(yn, wo_ref[0])


def _ssd_out(yf, yb, z, h, tbl, layer, nw, wo, geom):
    nt, d = h.shape
    d_inner = z.shape[1]
    nblk, nctx_blk, nbatch = geom
    j = layer // 2
    kern = functools.partial(_ssd_out_kernel, nblk, nctx_blk, nbatch, d)
    big = pl.BlockSpec((BLK, d_inner), lambda i: (i, 0))
    return pl.pallas_call(
        kern,
        grid=(nt // BLK,),
        in_specs=[
            big, big, big,
            pl.BlockSpec((BLK, d), lambda i: (i, 0)),
            pl.BlockSpec((1, 8, N_MOD * d), lambda i: (layer, 0, 0)),
            pl.BlockSpec((1, 1, d_inner), lambda i: (j, 0, 0)),
            pl.BlockSpec((1, d_inner, d), lambda i: (j, 0, 0), pipeline_mode=pl.Buffered(1)),
        ],
        out_specs=pl.BlockSpec((BLK, d), lambda i: (i, 0)),
        out_shape=jax.ShapeDtypeStruct((nt, d), F32),
        compiler_params=_params(("arbitrary",)),
        name=f"ssd_out_{layer}",
    )(yf, yb, z, h, tbl, nw, wo)


def _route_kernel(nblk, nctx_blk, nbatch, d,
                  h_ref, tbl_ref, nw_ref, rw_ref, sl_ref, info_ref, cnt_ref, carry_ref):
    @pl.when(pl.program_id(0) == 0)
    def _():
        carry_ref[...] = jnp.zeros_like(carry_ref)

    row, _, _ = _block_row(nblk, nctx_blk, nbatch)
    shift = tbl_ref[0, pl.ds(row, 1), 3 * d:4 * d]
    scale = tbl_ref[0, pl.ds(row, 1), 4 * d:5 * d]
    v = _rms(h_ref[...], nw_ref[0]) * (1.0 + scale) + shift
    lane = lax.broadcasted_iota(jnp.int32, (BLK, CHUNK), 1).astype(F32)
    logits = jnp.where(lane < N_EXPERTS, _dot_hi(v, rw_ref[0]), -jnp.inf)
    m1 = jnp.max(logits, axis=1, keepdims=True)
    i1 = jnp.min(jnp.where(logits == m1, lane, float(CHUNK)), axis=1, keepdims=True)
    rest = jnp.where(lane == i1, -jnp.inf, logits)
    m2 = jnp.max(rest, axis=1, keepdims=True)
    i2 = jnp.min(jnp.where(rest == m2, lane, float(CHUNK)), axis=1, keepdims=True)
    e2 = jnp.exp(m2 - m1)
    g1 = 1.0 / (1.0 + e2)
    g2 = e2 / (1.0 + e2)
    oh1 = (lane == i1)
    oh2 = (lane == i2)
    member = jnp.where(jnp.logical_or(oh1, oh2), 1.0, 0.0)
    before = carry_ref[...] + _dot(sl_ref[...], member.astype(BF16))
    r1 = jnp.sum(jnp.where(oh1, before, 0.0), axis=1, keepdims=True)
    r2 = jnp.sum(jnp.where(oh2, before, 0.0), axis=1, keepdims=True)
    total = carry_ref[...] + jnp.sum(member, axis=0, keepdims=True)
    carry_ref[...] = total
    cnt_ref[...] = jnp.broadcast_to(total, cnt_ref.shape)
    lane8 = lax.broadcasted_iota(jnp.int32, (BLK, 8), 1)
    info = jnp.where(lane8 == 0, i1,
           jnp.where(lane8 == 1, i2,
           jnp.where(lane8 == 2, r1,
           jnp.where(lane8 == 3, r2,
           jnp.where(lane8 == 4, g1,
           jnp.where(lane8 == 5, g2, 0.0))))))
    info_ref[...] = info


def _route(h, tbl, layer, nw, rw, strict_lower, geom):
    nt, d = h.shape
    nblk, nctx_blk, nbatch = geom
    j = layer // 2
    kern = functools.partial(_route_kernel, nblk, nctx_blk, nbatch, d)
    return pl.pallas_call(
        kern,
        grid=(nt // BLK,),
        in_specs=[
            pl.BlockSpec((BLK, d), lambda i: (i, 0)),
            pl.BlockSpec((1, 8, N_MOD * d), lambda i: (layer, 0, 0)),
            pl.BlockSpec((1, 1, d), lambda i: (layer, 0, 0)),
            pl.BlockSpec((1, d, CHUNK), lambda i: (j, 0, 0)),
            _resident(strict_lower.shape),
        ],
        out_specs=[
            pl.BlockSpec((BLK, 8), lambda i: (i, 0)),
            pl.BlockSpec((8, CHUNK), lambda i: (0, 0)),
        ],
        out_shape=[jax.ShapeDtypeStruct((nt, 8), F32), jax.ShapeDtypeStruct((8, CHUNK), F32)],
        scratch_shapes=[pltpu.VMEM((1, CHUNK), F32)],
        compiler_params=_params(("arbitrary",)),
        name=f"route_{layer}",
    )(h, tbl, nw, rw, strict_lower)


def _dispatch_kernel(nblk, nctx_blk, nbatch, d,
                     pos_ref, h_ref, tbl_ref, nw_ref, xs_in_ref, xs_out_ref, v_scr, sem):
    del xs_in_ref
    row, _, _ = _block_row(nblk, nctx_blk, nbatch)
    shift = tbl_ref[0, pl.ds(row, 1), 3 * d:4 * d]
    scale = tbl_ref[0, pl.ds(row, 1), 4 * d:5 * d]
    v_scr[...] = _rms(h_ref[...], nw_ref[0]) * (1.0 + scale) + shift

    def copy(r, k):
        return pltpu.make_async_copy(v_scr.at[pl.ds(r, 1), :],
                                     xs_out_ref.at[pl.ds(pos_ref[0, 0, 2 * r + k], 1), :], sem)

    def start(r, c):
        copy(r, 0).start()
        copy(r, 1).start()
        return c

    def wait(r, c):
        copy(r, 0).wait()
        copy(r, 1).wait()
        return c

    lax.fori_loop(0, BLK, start, 0)
    lax.fori_loop(0, BLK, wait, 0)


def _dispatch(h, tbl, layer, nw, pos, nrows, geom):
    nt, d = h.shape
    nblk, nctx_blk, nbatch = geom
    kern = functools.partial(_dispatch_kernel, nblk, nctx_blk, nbatch, d)
    zeros = jnp.zeros((nrows, d), F32)
    return pl.pallas_call(
        kern,
        grid=(nt // BLK,),
        in_specs=[
            pl.BlockSpec((1, 1, 2 * BLK), lambda i: (i, 0, 0), memory_space=pltpu.SMEM),
            pl.BlockSpec((BLK, d), lambda i: (i, 0)),
            pl.BlockSpec((1, 8, N_MOD * d), lambda i: (layer, 0, 0)),
            pl.BlockSpec((1, 1, d), lambda i: (layer, 0, 0)),
            pl.BlockSpec(memory_space=pl.ANY),
        ],
        out_specs=pl.BlockSpec(memory_space=pl.ANY),
        out_shape=jax.ShapeDtypeStruct((nrows, d), F32),
        scratch_shapes=[pltpu.VMEM((BLK, d), F32), pltpu.SemaphoreType.DMA(())],
        input_output_aliases={4: 0},
        compiler_params=_params(("arbitrary",)),
        name=f"dispatch_{layer}",
    )(pos.reshape(nt // BLK, 1, 2 * BLK), h, tbl, nw, zeros)


def _expert_kernel(n_fchunk, te_ref, nu_ref, x_ref, w1_ref, w3_ref, w2_ref, o_ref):
    i = pl.program_id(0)

    @pl.when(i >= nu_ref[0])
    def _():
        o_ref[...] = jnp.zeros_like(o_ref)

    @pl.when(i < nu_ref[0])
    def _():
        x = x_ref[...].astype(BF16)
        acc = None
        for k in range(n_fchunk):
            sl = slice(k * MOE_FCHUNK, (k + 1) * MOE_FCHUNK)
            a = _dot(x, w1_ref[0, :, sl])
            b = _dot(x, w3_ref[0, :, sl])
            act = (_silu(a) * b).astype(BF16)
            part = _dot(act, w2_ref[0, sl, :])
            acc = part if acc is None else acc + part
        o_ref[...] = acc


def _experts(x_sorted, tile_expert, n_used, w1, w3, w2):
    nrows, d = x_sorted.shape
    dffe = w1.shape[-1]
    n_tiles = nrows // MOE_TILE
    kern = functools.partial(_expert_kernel, dffe // MOE_FCHUNK)
    gs = pltpu.PrefetchScalarGridSpec(
        num_scalar_prefetch=2,
        grid=(n_tiles,),
        in_specs=[
            pl.BlockSpec((MOE_TILE, d), lambda i, te, nu: (i, 0)),
            pl.BlockSpec((1, d, dffe), lambda i, te, nu: (te[i], 0, 0), pipeline_mode=pl.Buffered(1)),
            pl.BlockSpec((1, d, dffe), lambda i, te, nu: (te[i], 0, 0), pipeline_mode=pl.Buffered(1)),
            pl.BlockSpec((1, dffe, d), lambda i, te, nu: (te[i], 0, 0), pipeline_mode=pl.Buffered(1)),
        ],
        out_specs=pl.BlockSpec((MOE_TILE, d), lambda i, te, nu: (i, 0)),
    )
    return pl.pallas_call(
        kern,
        grid_spec=gs,
        out_shape=jax.ShapeDtypeStruct((nrows, d), F32),
        compiler_params=_params(("arbitrary",)),
        name="experts",
    )(tile_expert, n_used, x_sorted, w1, w3, w2)


def _combine_kernel(nblk, nctx_blk, nbatch, d,
                    pos_ref, h_ref, info_ref, tbl_ref, y_ref, o_ref, buf0, buf1, sem):
    row, _, _ = _block_row(nblk, nctx_blk, nbatch)

    def copy(r, k):
        dst = buf0 if k == 0 else buf1
        return pltpu.make_async_copy(y_ref.at[pl.ds(pos_ref[0, 0, 2 * r + k], 1), :],
                                     dst.at[pl.ds(r, 1), :], sem)

    def start(r, c):
        copy(r, 0).start()
        copy(r, 1).start()
        return c

    def wait(r, c):
        copy(r, 0).wait()
        copy(r, 1).wait()
        return c

    lax.fori_loop(0, BLK, start, 0)
    lax.fori_loop(0, BLK, wait, 0)
    info = info_ref[...]
    g1 = info[:, 4:5]
    g2 = info[:, 5:6]
    gate = tbl_ref[0, pl.ds(row, 1), 5 * d:6 * d]
    o_ref[...] = h_ref[...] + gate * (g1 * buf0[...] + g2 * buf1[...])


def _combine(h, info, tbl, layer, y_sorted, pos, geom):
    nt, d = h.shape
    nblk, nctx_blk, nbatch = geom
    kern = functools.partial(_combine_kernel, nblk, nctx_blk, nbatch, d)
    return pl.pallas_call(
        kern,
        grid=(nt // BLK,),
        in_specs=[
            pl.BlockSpec((1, 1, 2 * BLK), lambda i: (i, 0, 0), memory_space=pltpu.SMEM),
            pl.BlockSpec((BLK, d), lambda i: (i, 0)),
            pl.BlockSpec((BLK, 8), lambda i: (i, 0)),
            pl.BlockSpec((1, 8, N_MOD * d), lambda i: (layer, 0, 0)),
            pl.BlockSpec(memory_space=pl.ANY),
        ],
        out_specs=pl.BlockSpec((BLK, d), lambda i: (i, 0)),
        out_shape=jax.ShapeDtypeStruct((nt, d), F32),
        scratch_shapes=[pltpu.VMEM((BLK, d), F32), pltpu.VMEM((BLK, d), F32), pltpu.SemaphoreType.DMA(())],
        compiler_params=_params(("arbitrary",)),
        name=f"combine_{layer}",
    )(pos.reshape(nt // BLK, 1, 2 * BLK), h, info, tbl, y_sorted)


def _moe_layer(h, tbl, layer, nw, rw, strict_lower, w1, w3, w2, geom):
    nt, d = h.shape
    info, counts = _route(h, tbl, layer, nw, rw, strict_lower, geom)
    cnt = counts[0, :N_EXPERTS].astype(jnp.int32)
    tiles = (cnt + MOE_TILE - 1) // MOE_TILE
    tile_end = jnp.cumsum(tiles)
    offs = (tile_end - tiles) * MOE_TILE
    idx = info[:, 0:2].astype(jnp.int32)
    pos = (offs[idx] + info[:, 2:4].astype(jnp.int32)).reshape(-1)
    n_tiles = (2 * nt) // MOE_TILE + N_EXPERTS
    tile_expert = jnp.minimum(jnp.searchsorted(tile_end, jnp.arange(n_tiles), side="right"),
                              N_EXPERTS - 1).astype(jnp.int32)
    n_used = tile_end[-1:].astype(jnp.int32)
    x_sorted = _dispatch(h, tbl, layer, nw, pos, n_tiles * MOE_TILE, geom)
    y_sorted = _experts(x_sorted, tile_expert, n_used, w1, w3, w2)
    return _combine(h, info, tbl, layer, y_sorted, pos, geom)


def _final_kernel(h_ref, w_ref, o_ref):
    o_ref[0] = _rms(h_ref[...], w_ref[...])


def _final_norm(h, w, nbatch, nblk, nctx_blk):
    nt, d = h.shape
    nlat = nblk - nctx_blk
    return pl.pallas_call(
        _final_kernel,
        grid=(nbatch, nlat),
        in_specs=[
            pl.BlockSpec((BLK, d), lambda b, j: (b * nblk + nctx_blk + j, 0)),
            pl.BlockSpec((1, d), lambda b, j: (0, 0)),
        ],
        out_specs=pl.BlockSpec((1, BLK, d), lambda b, j: (b, j, 0)),
        out_shape=jax.ShapeDtypeStruct((nbatch, nlat * BLK, d), F32),
        compiler_params=_params(("arbitrary", "arbitrary")),
        name="final_norm",
    )(h, w.reshape(1, d))


def kernel(x, c, ctx, c_ctx, ada_w, ada_b, norm_mix_w, norm_ffn_w, pool_w, pool_scale, ssd_in_w, ssd_conv_w, ssd_conv_b, ssd_A_log, ssd_dt_bias, ssd_D, ssd_norm_w, ssd_out_w, ffn_w1, ffn_w3, ffn_w2, moe_router_w, moe_w1, moe_w3, moe_w2, final_norm_w):
    nbatch, seq, d = x.shape
    ctx_len = ctx.shape[1]
    depth = ada_w.shape[0]
    d_inner = ssd_norm_w.shape[-1]
    assert ctx_len % BLK == 0 and seq % BLK == 0 and nbatch < 8
    assert d_inner == N_HEADS * HEAD_DIM and d % len(POOL_WINDOWS) == 0
    nblk = (ctx_len + seq) // BLK
    nctx_blk = ctx_len // BLK
    geom = (nblk, nctx_blk, nbatch)
    nchunk = (ctx_len + seq) // CHUNK
    ncc = ctx_len // CHUNK

    h = jnp.concatenate([ctx, x], axis=1).reshape(nbatch * (ctx_len + seq), d)
    cvec = jnp.zeros((8, d), F32).at[:nbatch].set(c).at[nbatch].set(c_ctx)
    tbl = _mod_table(cvec, ada_w, ada_b)

    vec3 = lambda a: a.reshape(a.shape[0], 1, a.shape[-1])
    nmix = vec3(norm_mix_w)
    nffn = vec3(norm_ffn_w)
    ahi, alo = _pool_matrices()
    scan_consts = _scan_constants()
    strict_lower = jnp.asarray(np.tril(np.ones((BLK, BLK), np.float32), -1), BF16)

    conv_dim = ssd_conv_w.shape[-1]
    wz = ssd_in_w[:, :, :d_inner].astype(BF16)
    wx = ssd_in_w[:, :, d_inner:d_inner + conv_dim].astype(BF16)
    wdt = ssd_in_w[:, :, d_inner + conv_dim:]
    pad_dt = lambda w: jnp.pad(w, ((0, 0), (0, 0), (0, CHUNK - N_HEADS))).astype(BF16)
    wdf = pad_dt(wdt[:, :, :N_HEADS])
    wdb = pad_dt(wdt[:, :, N_HEADS:])
    pad_h = lambda a: jnp.pad(a, ((0, 0), (0, 0), (0, CHUNK - N_HEADS)))
    dtb = pad_h(ssd_dt_bias)
    alog = pad_h(ssd_A_log)
    dskip = vec3(jnp.repeat(ssd_D, HEAD_DIM, axis=-1))
    rw = jnp.pad(moe_router_w, ((0, 0), (0, 0), (0, CHUNK - N_EXPERTS)))

    for i in range(depth):
        j = i // 2
        if i % 2 == 0:
            h = _even_layer(h, tbl, i, nmix, nffn, ahi, alo, pool_w.astype(BF16), vec3(pool_scale),
                            ffn_w1.astype(BF16), ffn_w3.astype(BF16), ffn_w2.astype(BF16), geom)
        else:
            z, xs, bc, dtf, dtbw = _ssd_in(h, tbl, i, nmix, wz, wx, wdf, wdb, ssd_conv_w,
                                           vec3(ssd_conv_b), dtb, geom)
            yf, yb = _ssd_scan(xs, bc, dtf, dtbw, alog, dskip, i, scan_consts, nbatch, nchunk, ncc)
            h = _ssd_out(yf, yb, z, h, tbl, i, vec3(ssd_norm_w), ssd_out_w.astype(BF16), geom)
            h = _moe_layer(h, tbl, i, nffn, rw, strict_lower, moe_w1[j].astype(BF16),
                           moe_w3[j].astype(BF16), moe_w2[j].astype(BF16), geom)
    return _final_norm(h, final_norm_w, nbatch, nblk, nctx_blk)
```

```python
import functools

import numpy as np
import jax
import jax.numpy as jnp
from jax import lax
from jax.experimental import pallas as pl
from jax.experimental.pallas import tpu as pltpu

F32 = jnp.float32
BF16 = jnp.bfloat16
EPS = 1e-6

BLK = 256
CHUNK = 128
GRID_W = 64
POOL_WINDOWS = (2, 4, 8, 16)
N_MOD = 6
HEAD_DIM = 64
N_HEADS = 32
N_GROUPS = 4
D_STATE = 128
D_CONV = 4
CONV_LEFT = 2
HALO = 8
N_EXPERTS = 8
MOE_TILE = 512
MOE_FCHUNK = 512
VMEM_LIMIT = 56 * 2**20
LOG2E = 1.4426950408889634
SUB = 8
DMA_UNROLL = 8


def _dot(a, b):
    return jnp.dot(a, b, preferred_element_type=F32)


def _split2(x):
    hi = x.astype(BF16)
    lo = (x - hi.astype(F32)).astype(BF16)
    return hi, lo


def _split3(x):
    p0 = x.astype(BF16)
    r = x - p0.astype(F32)
    p1 = r.astype(BF16)
    p2 = (r - p1.astype(F32)).astype(BF16)
    return p0, p1, p2


def _dot_hi(a, b):
    ah, al = _split2(a)
    bh, bl = _split2(b)
    return _dot(ah, bh) + _dot(al, bh) + _dot(ah, bl)


def _sigmoid(x):
    return 1.0 / (1.0 + jnp.exp(-x))


def _silu(x):
    return x * _sigmoid(x)


def _rms(x, w):
    ms = jnp.mean(x * x, axis=-1, keepdims=True)
    return x * lax.rsqrt(ms + EPS) * w


def _params(sem):
    return pltpu.CompilerParams(dimension_semantics=sem, vmem_limit_bytes=VMEM_LIMIT)


def _resident(shape):
    nd = len(shape)
    return pl.BlockSpec(shape, lambda *_: (0,) * nd, pipeline_mode=pl.Buffered(1))


def _block_row(nblk, nctx_blk, nbatch):
    blk = pl.program_id(0)
    b = blk // nblk
    j = blk - b * nblk
    is_ctx = j < nctx_blk
    return jnp.where(is_ctx, nbatch, b), is_ctx, j


def _mod_kernel(c_ref, w_ref, b_ref, o_ref):
    o_ref[0] = _dot_hi(_silu(c_ref[...]), w_ref[0]) + b_ref[0]


def _mod_table(cvec, ada_w, ada_b):
    depth, d, n = ada_w.shape
    tn = 512
    return pl.pallas_call(
        _mod_kernel,
        grid=(depth, n // tn),
        in_specs=[
            pl.BlockSpec((8, d), lambda l, j: (0, 0)),
            pl.BlockSpec((1, d, tn), lambda l, j: (l, 0, j)),
            pl.BlockSpec((1, 1, tn), lambda l, j: (l, 0, j)),
        ],
        out_specs=pl.BlockSpec((1, 8, tn), lambda l, j: (l, 0, j)),
        out_shape=jax.ShapeDtypeStruct((depth, 8, n), F32),
        compiler_params=_params(("arbitrary", "arbitrary")),
        name="mod_table",
    )(cvec, ada_w, ada_b.reshape(depth, 1, n))


def _even_kernel(nblk, nctx_blk, nbatch, d,
                 h_ref, tbl_ref, nw1_ref, nw2_ref, ahi_ref, alo_ref, pw_ref, ps_ref,
                 w1_ref, w3_ref, w2_ref, o_ref):
    row, is_ctx, _ = _block_row(nblk, nctx_blk, nbatch)
    kind = is_ctx.astype(jnp.int32)

    def mod(k):
        return tbl_ref[0, pl.ds(row, 1), k * d:(k + 1) * d]

    h = h_ref[...]
    u = _rms(h, nw1_ref[0]) * (1.0 + mod(1)) + mod(0)
    gw = d // len(POOL_WINDOWS)
    ys = []
    for g in range(len(POOL_WINDOWS)):
        ug = u[:, g * gw:(g + 1) * gw]
        uh, ul = _split2(ug)
        ah = ahi_ref[kind, g]
        al = alo_ref[kind, g]
        p = _dot(ah, uh) + _dot(al, uh) + _dot(ah, ul) - ug
        ys.append(_dot(p.astype(BF16), pw_ref[0, g]))
    y = jnp.concatenate(ys, axis=1) * ps_ref[0]
    h1 = h + mod(2) * y
    v = (_rms(h1, nw2_ref[0]) * (1.0 + mod(4)) + mod(3)).astype(BF16)
    a = _dot(v, w1_ref[0])
    b = _dot(v, w3_ref[0])
    act = (_silu(a) * b).astype(BF16)
    o_ref[...] = h1 + mod(5) * _dot(act, w2_ref[0])


def _pool_matrices():
    mats = np.zeros((2, len(POOL_WINDOWS), BLK, BLK), np.float64)
    for kind, seg in enumerate((GRID_W, BLK)):
        for g, w in enumerate(POOL_WINDOWS):
            lo = w // 2
            hi = w - 1 - lo
            for t in range(BLK):
                base = (t // seg) * seg
                tt = t - base
                start = max(tt - lo, 0)
                end = min(tt + hi + 1, seg)
                mats[kind, g, t, base + start:base + end] = 1.0 / (end - start)
    m32 = jnp.asarray(mats, F32)
    hi = m32.astype(BF16)
    lo = (m32 - hi.astype(F32)).astype(BF16)
    return hi, lo


def _even_layer(h, tbl, layer, nw1, nw2, ahi, alo, pw, ps, w1, w3, w2, geom):
    nt, d = h.shape
    nblk, nctx_blk, nbatch = geom
    dff = w1.shape[-1]
    j = layer // 2
    gw = d // len(POOL_WINDOWS)
    kern = functools.partial(_even_kernel, nblk, nctx_blk, nbatch, d)
    vec = lambda idx: pl.BlockSpec((1, 1, d), lambda i: (idx, 0, 0))
    return pl.pallas_call(
        kern,
        grid=(nt // BLK,),
        in_specs=[
            pl.BlockSpec((BLK, d), lambda i: (i, 0)),
            pl.BlockSpec((1, 8, N_MOD * d), lambda i: (layer, 0, 0)),
            vec(layer), vec(layer),
            _resident(ahi.shape), _resident(alo.shape),
            pl.BlockSpec((1, len(POOL_WINDOWS), gw, gw), lambda i: (j, 0, 0, 0)),
            vec(j),
            pl.BlockSpec((1, d, dff), lambda i: (j, 0, 0), pipeline_mode=pl.Buffered(1)),
            pl.BlockSpec((1, d, dff), lambda i: (j, 0, 0), pipeline_mode=pl.Buffered(1)),
            pl.BlockSpec((1, dff, d), lambda i: (j, 0, 0), pipeline_mode=pl.Buffered(1)),
        ],
        out_specs=pl.BlockSpec((BLK, d), lambda i: (i, 0)),
        out_shape=jax.ShapeDtypeStruct((nt, d), F32),
        compiler_params=_params(("arbitrary",)),
        name=f"pool_ffn_{layer}",
    )(h, tbl, nw1, nw2, ahi, alo, pw, ps, w1, w3, w2)


def _ssd_in_kernel(nblk, nctx_blk, nbatch, d, d_inner,
                   hp_ref, h_ref, hn_ref, tbl_ref, nw_ref, wz_ref, wx_ref, wdf_ref, wdb_ref,
                   cw_ref, cb_ref, dtb_ref,
                   z_ref, xs_ref, bc_ref, dtf_ref, dtb_out_ref, xbc_scr):
    row, _, j = _block_row(nblk, nctx_blk, nbatch)
    first = jnp.logical_or(j == 0, j == nctx_blk)
    last = jnp.logical_or(j == nctx_blk - 1, j == nblk - 1)
    shift = tbl_ref[0, pl.ds(row, 1), 0:d]
    scale = tbl_ref[0, pl.ds(row, 1), d:2 * d]
    nw = nw_ref[0]

    def modn(x):
        return _rms(x, nw) * (1.0 + scale) + shift

    u = modn(h_ref[...]).astype(BF16)
    up = (modn(hp_ref[...]) * jnp.where(first, 0.0, 1.0)).astype(BF16)
    un = (modn(hn_ref[...]) * jnp.where(last, 0.0, 1.0)).astype(BF16)
    wx = wx_ref[0]
    xbc_scr[0:HALO, :] = _dot(up, wx)
    xbc_scr[HALO:HALO + BLK, :] = _dot(u, wx)
    xbc_scr[HALO + BLK:HALO + BLK + HALO, :] = _dot(un, wx)
    acc = cb_ref[0]
    for k in range(D_CONV):
        off = HALO - CONV_LEFT + k
        acc = acc + xbc_scr[off:off + BLK, :] * cw_ref[0, k:k + 1, :]
    xbc = _silu(acc)
    xs_ref[...] = xbc[:, :d_inner]
    bc_ref[...] = xbc[:, d_inner:]
    z_ref[...] = _dot(u, wz_ref[0])

    def softplus(x):
        return jnp.maximum(x, 0.0) + jnp.log1p(jnp.exp(-jnp.abs(x)))

    dtf_ref[...] = softplus(_dot(u, wdf_ref[0]) + dtb_ref[0, 0:1, :])
    dtb_out_ref[...] = softplus(_dot(u, wdb_ref[0]) + dtb_ref[0, 1:2, :])


def _ssd_in(h, tbl, layer, nw, wz, wx, wdf, wdb, cw, cb, dtb, geom):
    nt, d = h.shape
    nblk, nctx_blk, nbatch = geom
    j = layer // 2
    d_inner = wz.shape[-1]
    conv_dim = wx.shape[-1]
    hb = BLK // HALO
    nh = nt // HALO
    kern = functools.partial(_ssd_in_kernel, nblk, nctx_blk, nbatch, d, d_inner)
    res3 = lambda a: pl.BlockSpec((1,) + a.shape[1:], lambda i: (j, 0, 0), pipeline_mode=pl.Buffered(1))
    return pl.pallas_call(
        kern,
        grid=(nt // BLK,),
        in_specs=[
            pl.BlockSpec((HALO, d), lambda i: (jnp.maximum(i * hb - 1, 0), 0)),
            pl.BlockSpec((BLK, d), lambda i: (i, 0)),
            pl.BlockSpec((HALO, d), lambda i: (jnp.minimum((i + 1) * hb, nh - 1), 0)),
            pl.BlockSpec((1, 8, N_MOD * d), lambda i: (layer, 0, 0)),
            pl.BlockSpec((1, 1, d), lambda i: (layer, 0, 0)),
            res3(wz), res3(wx), res3(wdf), res3(wdb),
            pl.BlockSpec((1, D_CONV, conv_dim), lambda i: (j, 0, 0)),
            pl.BlockSpec((1, 1, conv_dim), lambda i: (j, 0, 0)),
            pl.BlockSpec((1, 2, CHUNK), lambda i: (j, 0, 0)),
        ],
        out_specs=[
            pl.BlockSpec((BLK, d_inner), lambda i: (i, 0)),
            pl.BlockSpec((BLK, d_inner), lambda i: (i, 0)),
            pl.BlockSpec((BLK, conv_dim - d_inner), lambda i: (i, 0)),
            pl.BlockSpec((BLK, CHUNK), lambda i: (i, 0)),
            pl.BlockSpec((BLK, CHUNK), lambda i: (i, 0)),
        ],
        out_shape=[
            jax.ShapeDtypeStruct((nt, d_inner), F32),
            jax.ShapeDtypeStruct((nt, d_inner), F32),
            jax.ShapeDtypeStruct((nt, conv_dim - d_inner), F32),
            jax.ShapeDtypeStruct((nt, CHUNK), F32),
            jax.ShapeDtypeStruct((nt, CHUNK), F32),
        ],
        scratch_shapes=[pltpu.VMEM((BLK + 2 * HALO, conv_dim), F32)],
        compiler_params=_params(("arbitrary",)),
        name=f"ssd_in_{layer}",
    )(h, h, h, tbl, nw, wz, wx, wdf, wdb, cw, cb, dtb)


def _ssd_direction(xs, bc, dt, alog, s_ref, tri, e_ref, e3_ref, reverse, dskip):
    t = xs.shape[0]
    lane = lax.broadcasted_iota(jnp.int32, (1, CHUNK), 1)
    a_row = jnp.where(lane < N_HEADS, -jnp.exp(alog), 0.0)
    a = dt * (a_row * LOG2E)
    p0, p1, p2 = _split3(a)
    cum = _dot(tri, p0) + _dot(tri, p1) + _dot(tri, p2)
    tot = cum[0:1, :] if reverse else cum[t - 1:t, :]
    dte = jnp.exp2(tot - cum)
    ecum = jnp.exp2(cum)
    cdec = jnp.exp2(tot)
    row_t = (cum - jnp.log2(dt)).T

    e = e_ref[...]
    w_x = _dot((dt * dte).astype(BF16), e)
    ec_x = _dot(ecum.astype(BF16), e)
    c0, c1, c2 = _split3(jnp.broadcast_to(cdec, (8, CHUNK)))
    cd_x = (_dot(c0, e) + _dot(c1, e) + _dot(c2, e))[0:1, :]

    q0, q1, q2 = _split3(cum)
    stacked = (q0.astype(F32) + pltpu.roll(q1.astype(F32), N_HEADS, 1)
               + pltpu.roll(q2.astype(F32), 2 * N_HEADS, 1)).astype(BF16)
    colb = _dot(stacked, e3_ref[...])

    li = lax.broadcasted_iota(jnp.int32, (t, t), 0)
    si = lax.broadcasted_iota(jnp.int32, (t, t), 1)
    keep = (si >= li) if reverse else (si <= li)
    lane_p = lax.broadcasted_iota(jnp.int32, (t, 2 * HEAD_DIM), 1)
    first_head = lane_p < HEAD_DIM

    gn = N_GROUPS * D_STATE
    gp = (N_HEADS // N_GROUPS) * HEAD_DIM
    outs = []
    for g in range(N_GROUPS):
        b_g = bc[:, g * D_STATE:(g + 1) * D_STATE].astype(BF16)
        c_g = bc[:, gn + g * D_STATE:gn + (g + 1) * D_STATE].astype(BF16)
        cb = lax.dot_general(c_g, b_g, (((1,), (1,)), ((), ())), preferred_element_type=F32)
        ydiag = []
        for q in range(gp // (2 * HEAD_DIM)):
            h1 = g * (N_HEADS // N_GROUPS) + 2 * q
            ms = []
            for hh in (h1, h1 + 1):
                seg = colb[:, hh * CHUNK:(hh + 1) * CHUNK] - row_t[hh:hh + 1, :]
                lmat = jnp.exp2(jnp.where(keep, seg, -1e30))
                ms.append((cb * lmat).astype(BF16))
            m_pair = jnp.concatenate(ms, axis=1)
            x_pair = xs[:, h1 * HEAD_DIM:(h1 + 2) * HEAD_DIM]
            rhs = jnp.concatenate([jnp.where(first_head, x_pair, 0.0),
                                   jnp.where(first_head, 0.0, x_pair)], axis=0).astype(BF16)
            ydiag.append(_dot(m_pair, rhs))
        sl = slice(g * gp, (g + 1) * gp)
        s_old = s_ref[:, sl]
        y_off = _dot(c_g, s_old.astype(BF16)) * ec_x[:, sl]
        y_g = jnp.concatenate(ydiag, axis=1) + y_off
        if dskip is not None:
            y_g = y_g + xs[:, sl] * dskip[:, sl]
        outs.append(y_g)
        xw = (xs[:, sl] * w_x[:, sl]).astype(BF16)
        s_new = lax.dot_general(b_g, xw, (((0,), (0,)), ((), ())), preferred_element_type=F32)
        s_ref[:, sl] = s_old * cd_x[:, sl] + s_new
    return jnp.concatenate(outs, axis=1)


def _ssd_scan_kernel(xf_ref, bcf_ref, dtf_ref, xb_ref, bcb_ref, dtb_ref, alog_ref, dsk_ref,
                     tril_ref, triu_ref, e_ref, e3_ref, yf_ref, yb_ref, sf_ref, sb_ref):
    @pl.when(pl.program_id(1) == 0)
    def _():
        sf_ref[...] = jnp.zeros_like(sf_ref)
        sb_ref[...] = jnp.zeros_like(sb_ref)

    yf_ref[...] = _ssd_direction(xf_ref[...], bcf_ref[...], dtf_ref[...], alog_ref[0, 0:1, :], sf_ref,
                                 tril_ref[...], e_ref, e3_ref, False, dsk_ref[0])
    yb_ref[...] = _ssd_direction(xb_ref[...], bcb_ref[...], dtb_ref[...], alog_ref[0, 1:2, :], sb_ref,
                                 triu_ref[...], e_ref, e3_ref, True, None)


def _scan_constants():
    li = np.arange(CHUNK)[:, None]
    ti = np.arange(CHUNK)[None, :]
    tril = (ti <= li).astype(np.float32)
    triu = (ti >= li).astype(np.float32)
    e = np.zeros((CHUNK, N_HEADS * HEAD_DIM), np.float32)
    e3 = np.zeros((CHUNK, N_HEADS * CHUNK), np.float32)
    for h in range(N_HEADS):
        e[h, h * HEAD_DIM:(h + 1) * HEAD_DIM] = 1.0
        for piece in range(3):
            e3[piece * N_HEADS + h, h * CHUNK:(h + 1) * CHUNK] = 1.0
    return tuple(jnp.asarray(m, BF16) for m in (tril, triu, e, e3))


def _ssd_scan(xs, bc, dtf, dtb, alog, dskip, layer, consts, nbatch, nchunk, ncc):
    nt, d_inner = xs.shape
    bcw = bc.shape[1]
    j = layer // 2
    tril, triu, e, e3 = consts

    def fwd(b, c):
        return (b * nchunk + c, 0)

    def bwd(b, c):
        return (b * nchunk + jnp.where(c < ncc, ncc - 1 - c, nchunk - 1 - (c - ncc)), 0)

    return pl.pallas_call(
        _ssd_scan_kernel,
        grid=(nbatch, nchunk),
        in_specs=[
            pl.BlockSpec((CHUNK, d_inner), fwd), pl.BlockSpec((CHUNK, bcw), fwd), pl.BlockSpec((CHUNK, CHUNK), fwd),
            pl.BlockSpec((CHUNK, d_inner), bwd), pl.BlockSpec((CHUNK, bcw), bwd), pl.BlockSpec((CHUNK, CHUNK), bwd),
            pl.BlockSpec((1, 2, CHUNK), lambda b, c: (j, 0, 0)),
            pl.BlockSpec((1, 1, d_inner), lambda b, c: (j, 0, 0)),
            _resident(tril.shape), _resident(triu.shape), _resident(e.shape), _resident(e3.shape),
        ],
        out_specs=[pl.BlockSpec((CHUNK, d_inner), fwd), pl.BlockSpec((CHUNK, d_inner), bwd)],
        out_shape=[jax.ShapeDtypeStruct((nt, d_inner), F32)] * 2,
        scratch_shapes=[pltpu.VMEM((D_STATE, d_inner), F32)] * 2,
        compiler_params=_params(("arbitrary", "arbitrary")),
        name=f"ssd_scan_{layer}",
    )(xs, bc, dtf, xs, bc, dtb, alog, dskip, tril, triu, e, e3)


def _ssd_out_kernel(nblk, nctx_blk, nbatch, d,
                    yf_ref, yb_ref, z_ref, h_ref, tbl_ref, nw_ref, wo_ref, o_ref):
    row, _, _ = _block_row(nblk, nctx_blk, nbatch)
    y = (yf_ref[...] + yb_ref[...]) * _silu(z_ref[...])
    yn = _rms(y, nw_ref[0]).astype(BF16)
    gate = tbl_ref[0, pl.ds(row, 1), 2 * d:3 * d]
    o_ref[...] = h_ref[...] + gate * _dot(yn, wo_ref[0])


def _ssd_out(yf, yb, z, h, tbl, layer, nw, wo, geom):
    nt, d = h.shape
    d_inner = z.shape[1]
    nblk, nctx_blk, nbatch = geom
    j = layer // 2
    kern = functools.partial(_ssd_out_kernel, nblk, nctx_blk, nbatch, d)
    big = pl.BlockSpec((BLK, d_inner), lambda i: (i, 0))
    return pl.pallas_call(
        kern,
        grid=(nt // BLK,),
        in_specs=[
            big, big, big,
            pl.BlockSpec((BLK, d), lambda i: (i, 0)),
            pl.BlockSpec((1, 8, N_MOD * d), lambda i: (layer, 0, 0)),
            pl.BlockSpec((1, 1, d_inner), lambda i: (j, 0, 0)),
            pl.BlockSpec((1, d_inner, d), lambda i: (j, 0, 0), pipeline_mode=pl.Buffered(1)),
        ],
        out_specs=pl.BlockSpec((BLK, d), lambda i: (i, 0)),
        out_shape=jax.ShapeDtypeStruct((nt, d), F32),
        compiler_params=_params(("arbitrary",)),
        name=f"ssd_out_{layer}",
    )(yf, yb, z, h, tbl, nw, wo)


def _route_kernel(nblk, nctx_blk, nbatch, d,
                  h_ref, tbl_ref, nw_ref, rw_ref, sl_ref, info_ref, cnt_ref, carry_ref):
    @pl.when(pl.program_id(0) == 0)
    def _():
        carry_ref[...] = jnp.zeros_like(carry_ref)

    row, _, _ = _block_row(nblk, nctx_blk, nbatch)
    shift = tbl_ref[0, pl.ds(row, 1), 3 * d:4 * d]
    scale = tbl_ref[0, pl.ds(row, 1), 4 * d:5 * d]
    v = _rms(h_ref[...], nw_ref[0]) * (1.0 + scale) + shift
    lane = lax.broadcasted_iota(jnp.int32, (BLK, CHUNK), 1).astype(F32)
    logits = jnp.where(lane < N_EXPERTS, _dot_hi(v, rw_ref[0]), -jnp.inf)
    m1 = jnp.max(logits, axis=1, keepdims=True)
    i1 = jnp.min(jnp.where(logits == m1, lane, float(CHUNK)), axis=1, keepdims=True)
    rest = jnp.where(lane == i1, -jnp.inf, logits)
    m2 = jnp.max(rest, axis=1, keepdims=True)
    i2 = jnp.min(jnp.where(rest == m2, lane, float(CHUNK)), axis=1, keepdims=True)
    e2 = jnp.exp(m2 - m1)
    g1 = 1.0 / (1.0 + e2)
    g2 = e2 / (1.0 + e2)
    oh1 = (lane == i1)
    oh2 = (lane == i2)
    member = jnp.where(jnp.logical_or(oh1, oh2), 1.0, 0.0)
    before = carry_ref[...] + _dot(sl_ref[...], member.astype(BF16))
    r1 = jnp.sum(jnp.where(oh1, before, 0.0), axis=1, keepdims=True)
    r2 = jnp.sum(jnp.where(oh2, before, 0.0), axis=1, keepdims=True)
    total = carry_ref[...] + jnp.sum(member, axis=0, keepdims=True)
    carry_ref[...] = total
    cnt_ref[...] = jnp.broadcast_to(total, cnt_ref.shape)
    lane8 = lax.broadcasted_iota(jnp.int32, (BLK, 8), 1)
    info = jnp.where(lane8 == 0, i1,
           jnp.where(lane8 == 1, i2,
           jnp.where(lane8 == 2, r1,
           jnp.where(lane8 == 3, r2,
           jnp.where(lane8 == 4, g1,
           jnp.where(lane8 == 5, g2, 0.0))))))
    info_ref[...] = info


def _route(h, tbl, layer, nw, rw, strict_lower, geom):
    nt, d = h.shape
    nblk, nctx_blk, nbatch = geom
    j = layer // 2
    kern = functools.partial(_route_kernel, nblk, nctx_blk, nbatch, d)
    return pl.pallas_call(
        kern,
        grid=(nt // BLK,),
        in_specs=[
            pl.BlockSpec((BLK, d), lambda i: (i, 0)),
            pl.BlockSpec((1, 8, N_MOD * d), lambda i: (layer, 0, 0)),
            pl.BlockSpec((1, 1, d), lambda i: (layer, 0, 0)),
            pl.BlockSpec((1, d, CHUNK), lambda i: (j, 0, 0)),
            _resident(strict_lower.shape),
        ],
        out_specs=[
            pl.BlockSpec((BLK, 8), lambda i: (i, 0)),
            pl.BlockSpec((8, CHUNK), lambda i: (0, 0)),
        ],
        out_shape=[jax.ShapeDtypeStruct((nt, 8), F32), jax.ShapeDtypeStruct((8, CHUNK), F32)],
        scratch_shapes=[pltpu.VMEM((1, CHUNK), F32)],
        compiler_params=_params(("arbitrary",)),
        name=f"route_{layer}",
    )(h, tbl, nw, rw, strict_lower)


def _to_tiles(ref, x, rows):
    for k in range(SUB):
        ref[pl.ds(k, rows, stride=SUB), :] = x[:, k * 128:(k + 1) * 128]


def _from_tiles(ref, rows):
    return jnp.concatenate([ref[pl.ds(k, rows, stride=SUB), :] for k in range(SUB)], axis=1)


def _tile_rows(ref, p):
    return ref.at[pl.ds(pl.multiple_of(p * SUB, SUB), SUB), :]


def _dispatch_kernel(nblk, nctx_blk, nbatch, d,
                     ztile_ref, zvalid_ref, pos_ref, h_ref, tbl_ref, nw_ref, xs_ref, v_scr, z_scr, sem, zsem):
    @pl.when(pl.program_id(0) == 0)
    def _():
        z_scr[...] = jnp.zeros_like(z_scr)
        tile_rows = MOE_TILE * SUB
        for e in range(2 * N_EXPERTS):
            @pl.when(zvalid_ref[e] == 1)
            def _():
                first = pl.multiple_of(ztile_ref[e] * tile_rows, SUB)
                pltpu.make_async_copy(z_scr, xs_ref.at[pl.ds(first, tile_rows), :], zsem).start()
        for e in range(2 * N_EXPERTS):
            @pl.when(zvalid_ref[e] == 1)
            def _():
                pltpu.make_async_copy(z_scr, xs_ref.at[pl.ds(0, tile_rows), :], zsem).wait()

    row, _, _ = _block_row(nblk, nctx_blk, nbatch)
    shift = tbl_ref[0, pl.ds(row, 1), 3 * d:4 * d]
    scale = tbl_ref[0, pl.ds(row, 1), 4 * d:5 * d]
    _to_tiles(v_scr, _rms(h_ref[...], nw_ref[0]) * (1.0 + scale) + shift, BLK)

    def start(i, c):
        for u in range(DMA_UNROLL):
            r = i * DMA_UNROLL + u
            for k in range(2):
                pltpu.make_async_copy(_tile_rows(v_scr, r), _tile_rows(xs_ref, pos_ref[0, 0, 2 * r + k]),
                                      sem).start(priority=k)
        return c

    def wait(i, c):
        for _ in range(2 * DMA_UNROLL):
            pltpu.make_async_copy(v_scr.at[pl.ds(0, SUB), :], xs_ref.at[pl.ds(0, SUB), :], sem).wait()
        return c

    lax.fori_loop(0, BLK // DMA_UNROLL, start, 0)
    lax.fori_loop(0, BLK // DMA_UNROLL, wait, 0)


def _dispatch(h, tbl, layer, nw, pos, ztile, zvalid, nslots, geom):
    nt, d = h.shape
    nblk, nctx_blk, nbatch = geom
    kern = functools.partial(_dispatch_kernel, nblk, nctx_blk, nbatch, d)
    gs = pltpu.PrefetchScalarGridSpec(
        num_scalar_prefetch=2,
        grid=(nt // BLK,),
        in_specs=[
            pl.BlockSpec((1, 1, 2 * BLK), lambda i, zt, zv: (i, 0, 0), memory_space=pltpu.SMEM),
            pl.BlockSpec((BLK, d), lambda i, zt, zv: (i, 0)),
            pl.BlockSpec((1, 8, N_MOD * d), lambda i, zt, zv: (layer, 0, 0)),
            pl.BlockSpec((1, 1, d), lambda i, zt, zv: (layer, 0, 0)),
        ],
        out_specs=pl.BlockSpec(memory_space=pl.ANY),
        scratch_shapes=[pltpu.VMEM((BLK * SUB, 128), F32), pltpu.VMEM((MOE_TILE * SUB, 128), F32),
                        pltpu.SemaphoreType.DMA(()), pltpu.SemaphoreType.DMA(())],
    )
    return pl.pallas_call(
        kern,
        grid_spec=gs,
        out_shape=jax.ShapeDtypeStruct((nslots * SUB, 128), F32),
        compiler_params=_params(("arbitrary",)),
        name=f"dispatch_{layer}",
    )(ztile, zvalid, pos.reshape(nt // BLK, 1, 2 * BLK), h, tbl, nw)


def _expert_kernel(n_fchunk, te_ref, nu_ref, x_ref, w1_ref, w3_ref, w2_ref, o_ref):
    @pl.when(pl.program_id(0) >= nu_ref[0])
    def _():
        o_ref[...] = jnp.zeros_like(o_ref)

    @pl.when(pl.program_id(0) < nu_ref[0])
    def _():
        x = _from_tiles(x_ref, MOE_TILE).astype(BF16)
        acc = None
        for k in range(n_fchunk):
            sl = slice(k * MOE_FCHUNK, (k + 1) * MOE_FCHUNK)
            a = _dot(x, w1_ref[0, 0, :, sl])
            b = _dot(x, w3_ref[0, 0, :, sl])
            act = (_silu(a) * b).astype(BF16)
            part = _dot(act, w2_ref[0, 0, sl, :])
            acc = part if acc is None else acc + part
        _to_tiles(o_ref, acc, MOE_TILE)


def _experts(x_sorted, tile_expert, n_used, j, w1, w3, w2):
    d, dffe = w1.shape[-2:]
    n_tiles = x_sorted.shape[0] // (MOE_TILE * SUB)
    kern = functools.partial(_expert_kernel, dffe // MOE_FCHUNK)
    tile = lambda i, te, nu: (jnp.minimum(i, nu[0] - 1), 0)
    wspec = lambda shape: pl.BlockSpec((1, 1) + shape, lambda i, te, nu: (j, te[i], 0, 0),
                                       pipeline_mode=pl.Buffered(1))
    gs = pltpu.PrefetchScalarGridSpec(
        num_scalar_prefetch=2,
        grid=(n_tiles,),
        in_specs=[pl.BlockSpec((MOE_TILE * SUB, 128), tile), wspec((d, dffe)), wspec((d, dffe)), wspec((dffe, d))],
        out_specs=pl.BlockSpec((MOE_TILE * SUB, 128), lambda i, te, nu: (i, 0)),
    )
    return pl.pallas_call(
        kern,
        grid_spec=gs,
        out_shape=jax.ShapeDtypeStruct(x_sorted.shape, F32),
        compiler_params=_params(("arbitrary",)),
        name=f"experts_{j}",
    )(tile_expert, n_used, x_sorted, w1, w3, w2)


def _combine_kernel(row_of_step, d, final,
                    pos_ref, h_ref, info_ref, tbl_ref, y_ref, *rest):
    if final:
        fw_ref, o_ref, buf0, buf1, sem = rest
    else:
        o_ref, buf0, buf1, sem = rest
    row = row_of_step()

    def start(i, c):
        for u in range(DMA_UNROLL):
            r = i * DMA_UNROLL + u
            for k, buf in enumerate((buf0, buf1)):
                pltpu.make_async_copy(_tile_rows(y_ref, pos_ref[0, 0, 2 * r + k]), _tile_rows(buf, r),
                                      sem).start(priority=k)
        return c

    def wait(i, c):
        for _ in range(2 * DMA_UNROLL):
            pltpu.make_async_copy(y_ref.at[pl.ds(0, SUB), :], buf0.at[pl.ds(0, SUB), :], sem).wait()
        return c

    lax.fori_loop(0, BLK // DMA_UNROLL, start, 0)
    lax.fori_loop(0, BLK // DMA_UNROLL, wait, 0)
    info = info_ref[...]
    g1 = info[:, 4:5]
    g2 = info[:, 5:6]
    gate = tbl_ref[0, pl.ds(row, 1), 5 * d:6 * d]
    out = h_ref[...] + gate * (g1 * _from_tiles(buf0, BLK) + g2 * _from_tiles(buf1, BLK))
    if final:
        o_ref[0] = _rms(out, fw_ref[...])
    else:
        o_ref[...] = out


def _combine(h, info, tbl, layer, y_sorted, pos, geom, final_w=None):
    nt, d = h.shape
    nblk, nctx_blk, nbatch = geom
    final = final_w is not None
    scratch = [pltpu.VMEM((BLK * SUB, 128), F32), pltpu.VMEM((BLK * SUB, 128), F32), pltpu.SemaphoreType.DMA(())]
    pos3 = pos.reshape(nt // BLK, 1, 2 * BLK)
    if final:
        nlat = nblk - nctx_blk
        blk = lambda b, j: b * nblk + nctx_blk + j
        kern = functools.partial(_combine_kernel, lambda: pl.program_id(0), d, True)
        return pl.pallas_call(
            kern,
            grid=(nbatch, nlat),
            in_specs=[
                pl.BlockSpec((1, 1, 2 * BLK), lambda b, j: (blk(b, j), 0, 0), memory_space=pltpu.SMEM),
                pl.BlockSpec((BLK, d), lambda b, j: (blk(b, j), 0)),
                pl.BlockSpec((BLK, 8), lambda b, j: (blk(b, j), 0)),
                pl.BlockSpec((1, 8, N_MOD * d), lambda b, j: (layer, 0, 0)),
                pl.BlockSpec(memory_space=pl.ANY),
                pl.BlockSpec((1, d), lambda b, j: (0, 0)),
            ],
            out_specs=pl.BlockSpec((1, BLK, d), lambda b, j: (b, j, 0)),
            out_shape=jax.ShapeDtypeStruct((nbatch, nlat * BLK, d), F32),
            scratch_shapes=scratch,
            compiler_params=_params(("arbitrary", "arbitrary")),
            name=f"combine_{layer}",
        )(pos3, h, info, tbl, y_sorted, final_w.reshape(1, d))
    kern = functools.partial(_combine_kernel, lambda: _block_row(nblk, nctx_blk, nbatch)[0], d, False)
    return pl.pallas_call(
        kern,
        grid=(nt // BLK,),
        in_specs=[
            pl.BlockSpec((1, 1, 2 * BLK), lambda i: (i, 0, 0), memory_space=pltpu.SMEM),
            pl.BlockSpec((BLK, d), lambda i: (i, 0)),
            pl.BlockSpec((BLK, 8), lambda i: (i, 0)),
            pl.BlockSpec((1, 8, N_MOD * d), lambda i: (layer, 0, 0)),
            pl.BlockSpec(memory_space=pl.ANY),
        ],
        out_specs=pl.BlockSpec((BLK, d), lambda i: (i, 0)),
        out_shape=jax.ShapeDtypeStruct((nt, d), F32),
        scratch_shapes=scratch,
        compiler_params=_params(("arbitrary",)),
        name=f"combine_{layer}",
    )(pos3, h, info, tbl, y_sorted)


def _moe_layer(h, tbl, layer, nw, rw, strict_lower, w1, w3, w2, geom, final_w=None):
    nt, d = h.shape
    j = layer // 2
    info, counts = _route(h, tbl, layer, nw, rw, strict_lower, geom)
    cnt = counts[0, :N_EXPERTS].astype(jnp.int32)
    tiles = (cnt + MOE_TILE - 1) // MOE_TILE
    tile_end = jnp.cumsum(tiles)
    offs = (tile_end - tiles) * MOE_TILE
    n_used = tile_end[-1]
    idx = info[:, 0:2].astype(jnp.int32)
    pos = (offs[idx] + info[:, 2:4].astype(jnp.int32)).reshape(-1)
    n_tiles = (2 * nt) // MOE_TILE + N_EXPERTS
    t = jnp.minimum(jnp.arange(n_tiles, dtype=jnp.int32), n_used - 1)
    tile_expert = jnp.sum(t[:, None] >= tile_end[None, :], axis=1).astype(jnp.int32)
    spare = n_used + jnp.arange(N_EXPERTS, dtype=jnp.int32)
    ztile = jnp.concatenate([tile_end - 1, spare]).astype(jnp.int32)
    zvalid = jnp.concatenate([tiles > 0, spare < n_tiles]).astype(jnp.int32)
    x_sorted = _dispatch(h, tbl, layer, nw, pos, ztile, zvalid, n_tiles * MOE_TILE, geom)
    y_sorted = _experts(x_sorted, tile_expert, n_used.reshape(1).astype(jnp.int32), j, w1, w3, w2)
    return _combine(h, info, tbl, layer, y_sorted, pos, geom, final_w)


def kernel(x, c, ctx, c_ctx, ada_w, ada_b, norm_mix_w, norm_ffn_w, pool_w, pool_scale, ssd_in_w, ssd_conv_w, ssd_conv_b, ssd_A_log, ssd_dt_bias, ssd_D, ssd_norm_w, ssd_out_w, ffn_w1, ffn_w3, ffn_w2, moe_router_w, moe_w1, moe_w3, moe_w2, final_norm_w):
    nbatch, seq, d = x.shape
    ctx_len = ctx.shape[1]
    depth = ada_w.shape[0]
    d_inner = ssd_norm_w.shape[-1]
    assert ctx_len % BLK == 0 and seq % BLK == 0 and nbatch < 8
    assert d_inner == N_HEADS * HEAD_DIM and d % len(POOL_WINDOWS) == 0
    nblk = (ctx_len + seq) // BLK
    nctx_blk = ctx_len // BLK
    geom = (nblk, nctx_blk, nbatch)
    nchunk = (ctx_len + seq) // CHUNK
    ncc = ctx_len // CHUNK

    h = jnp.concatenate([ctx, x], axis=1).reshape(nbatch * (ctx_len + seq), d)
    cvec = jnp.zeros((8, d), F32).at[:nbatch].set(c).at[nbatch].set(c_ctx)
    tbl = _mod_table(cvec, ada_w, ada_b)

    vec3 = lambda a: a.reshape(a.shape[0], 1, a.shape[-1])
    nmix = vec3(norm_mix_w)
    nffn = vec3(norm_ffn_w)
    ahi, alo = _pool_matrices()
    scan_consts = _scan_constants()
    strict_lower = jnp.asarray(np.tril(np.ones((BLK, BLK), np.float32), -1), BF16)

    conv_dim = ssd_conv_w.shape[-1]
    wz = ssd_in_w[:, :, :d_inner].astype(BF16)
    wx = ssd_in_w[:, :, d_inner:d_inner + conv_dim].astype(BF16)
    wdt = ssd_in_w[:, :, d_inner + conv_dim:]
    pad_dt = lambda w: jnp.pad(w, ((0, 0), (0, 0), (0, CHUNK - N_HEADS))).astype(BF16)
    wdf = pad_dt(wdt[:, :, :N_HEADS])
    wdb = pad_dt(wdt[:, :, N_HEADS:])
    pad_h = lambda a: jnp.pad(a, ((0, 0), (0, 0), (0, CHUNK - N_HEADS)))
    dtb = pad_h(ssd_dt_bias)
    alog = pad_h(ssd_A_log)
    dskip = vec3(jnp.repeat(ssd_D, HEAD_DIM, axis=-1))
    rw = jnp.pad(moe_router_w, ((0, 0), (0, 0), (0, CHUNK - N_EXPERTS)))
    pool_wb = pool_w.astype(BF16)
    ffn = [w.astype(BF16) for w in (ffn_w1, ffn_w3, ffn_w2)]
    moe = [w.astype(BF16) for w in (moe_w1, moe_w3, moe_w2)]
    wout = ssd_out_w.astype(BF16)

    for i in range(depth):
        if i % 2 == 0:
            h = _even_layer(h, tbl, i, nmix, nffn, ahi, alo, pool_wb, vec3(pool_scale), *ffn, geom)
        else:
            z, xs, bc, dtf, dtbw = _ssd_in(h, tbl, i, nmix, wz, wx, wdf, wdb, ssd_conv_w,
                                           vec3(ssd_conv_b), dtb, geom)
            yf, yb = _ssd_scan(xs, bc, dtf, dtbw, alog, dskip, i, scan_consts, nbatch, nchunk, ncc)
            h = _ssd_out(yf, yb, z, h, tbl, i, vec3(ssd_norm_w), wout, geom)
            fin = final_norm_w if (i == depth - 1) else None
            h = _moe_layer(h, tbl, i, nffn, rw, strict_lower, *moe, geom, fin)
    if depth % 2 == 1:
        h = _final_norm(h, final_norm_w, nbatch, nblk, nctx_blk)
    return h
```

```python
import functools

import numpy as np
import jax
import jax.numpy as jnp
from jax import lax
from jax.experimental import pallas as pl
from jax.experimental.pallas import tpu as pltpu

F32 = jnp.float32
BF16 = jnp.bfloat16
EPS = 1e-6

BLK = 256
TILE_BLKS = 2
CHUNK = 128
GRID_W = 64
POOL_WINDOWS = (2, 4, 8, 16)
N_MOD = 6
HEAD_DIM = 64
N_HEADS = 32
N_GROUPS = 4
D_STATE = 128
D_CONV = 4
CONV_LEFT = 2
HALO = 8
N_EXPERTS = 8
MOE_TILE = 512
MOE_FCHUNK = 512
VMEM_LIMIT = 56 * 2**20
LOG2E = 1.4426950408889634
SUB = 8
DMA_UNROLL = 8


def _dot(a, b):
    return jnp.dot(a, b, preferred_element_type=F32)


def _split2(x):
    hi = x.astype(BF16)
    lo = (x - hi.astype(F32)).astype(BF16)
    return hi, lo


def _split3(x):
    p0 = x.astype(BF16)
    r = x - p0.astype(F32)
    p1 = r.astype(BF16)
    p2 = (r - p1.astype(F32)).astype(BF16)
    return p0, p1, p2


def _dot_hi(a, b):
    ah, al = _split2(a)
    bh, bl = _split2(b)
    return _dot(ah, bh) + _dot(al, bh) + _dot(ah, bl)


def _sigmoid(x):
    return 1.0 / (1.0 + jnp.exp(-x))


def _silu(x):
    return x * _sigmoid(x)


def _rms(x, w):
    ms = jnp.mean(x * x, axis=-1, keepdims=True)
    return x * lax.rsqrt(ms + EPS) * w


def _params(sem):
    return pltpu.CompilerParams(dimension_semantics=sem, vmem_limit_bytes=VMEM_LIMIT)


def _resident(shape):
    nd = len(shape)
    return pl.BlockSpec(shape, lambda *_: (0,) * nd, pipeline_mode=pl.Buffered(1))


def _block_row(nblk, nctx_blk, nbatch, blk=None):
    if blk is None:
        blk = pl.program_id(0)
    b = blk // nblk
    j = blk - b * nblk
    is_ctx = j < nctx_blk
    return jnp.where(is_ctx, nbatch, b), is_ctx, j


def _mod_kernel(c_ref, w_ref, b_ref, o_ref):
    o_ref[0] = _dot_hi(_silu(c_ref[...]), w_ref[0]) + b_ref[0]


def _mod_table(cvec, ada_w, ada_b):
    depth, d, n = ada_w.shape
    tn = 512
    return pl.pallas_call(
        _mod_kernel,
        grid=(depth, n // tn),
        in_specs=[
            pl.BlockSpec((8, d), lambda l, j: (0, 0)),
            pl.BlockSpec((1, d, tn), lambda l, j: (l, 0, j)),
            pl.BlockSpec((1, 1, tn), lambda l, j: (l, 0, j)),
        ],
        out_specs=pl.BlockSpec((1, 8, tn), lambda l, j: (l, 0, j)),
        out_shape=jax.ShapeDtypeStruct((depth, 8, n), F32),
        compiler_params=_params(("arbitrary", "arbitrary")),
        name="mod_table",
    )(cvec, ada_w, ada_b.reshape(depth, 1, n))


def _even_kernel(nblk, nctx_blk, nbatch, d,
                 h_ref, tbl_ref, nw1_ref, nw2_ref, ahi_ref, alo_ref, pw_ref, ps_ref,
                 w1_ref, w3_ref, w2_ref, o_ref):
    def mods(sb):
        row, is_ctx, _ = _block_row(nblk, nctx_blk, nbatch, pl.program_id(0) * TILE_BLKS + sb)
        return (lambda k: tbl_ref[0, pl.ds(row, 1), k * d:(k + 1) * d]), is_ctx.astype(jnp.int32)

    def mixer(sb):
        mod, kind = mods(sb)
        h = h_ref[sb * BLK:(sb + 1) * BLK, :]
        u = _rms(h, nw1_ref[0]) * (1.0 + mod(1)) + mod(0)
        gw = d // len(POOL_WINDOWS)
        ys = []
        for g in range(len(POOL_WINDOWS)):
            ug = u[:, g * gw:(g + 1) * gw]
            uh, ul = _split2(ug)
            ah = ahi_ref[kind, g]
            al = alo_ref[kind, g]
            p = _dot(ah, uh) + _dot(al, uh) + _dot(ah, ul) - ug
            ys.append(_dot(p.astype(BF16), pw_ref[0, g]))
        y = jnp.concatenate(ys, axis=1) * ps_ref[0]
        h1 = h + mod(2) * y
        v = (_rms(h1, nw2_ref[0]) * (1.0 + mod(4)) + mod(3)).astype(BF16)
        return h1, v

    def up(v):
        return _dot(v, w1_ref[0]), _dot(v, w3_ref[0])

    def down(sb, h1, ab):
        mod, _ = mods(sb)
        act = (_silu(ab[0]) * ab[1]).astype(BF16)
        o_ref[sb * BLK:(sb + 1) * BLK, :] = h1 + mod(5) * _dot(act, w2_ref[0])

    state = [mixer(0)]
    ab = [up(state[0][1])]
    for sb in range(TILE_BLKS):
        if sb + 1 < TILE_BLKS:
            state.append(mixer(sb + 1))
        down(sb, state[sb][0], ab[sb])
        if sb + 1 < TILE_BLKS:
            ab.append(up(state[sb + 1][1]))


def _pool_matrices():
    mats = np.zeros((2, len(POOL_WINDOWS), BLK, BLK), np.float64)
    for kind, seg in enumerate((GRID_W, BLK)):
        for g, w in enumerate(POOL_WINDOWS):
            lo = w // 2
            hi = w - 1 - lo
            for t in range(BLK):
                base = (t // seg) * seg
                tt = t - base
                start = max(tt - lo, 0)
                end = min(tt + hi + 1, seg)
                mats[kind, g, t, base + start:base + end] = 1.0 / (end - start)
    m32 = jnp.asarray(mats, F32)
    hi = m32.astype(BF16)
    lo = (m32 - hi.astype(F32)).astype(BF16)
    return hi, lo


def _even_layer(h, tbl, layer, nw1, nw2, ahi, alo, pw, ps, w1, w3, w2, geom):
    nt, d = h.shape
    nblk, nctx_blk, nbatch = geom
    dff = w1.shape[-1]
    j = layer // 2
    gw = d // len(POOL_WINDOWS)
    kern = functools.partial(_even_kernel, nblk, nctx_blk, nbatch, d)
    vec = lambda idx: pl.BlockSpec((1, 1, d), lambda i: (idx, 0, 0))
    tm = TILE_BLKS * BLK
    return pl.pallas_call(
        kern,
        grid=(nt // tm,),
        in_specs=[
            pl.BlockSpec((tm, d), lambda i: (i, 0)),
            pl.BlockSpec((1, 8, N_MOD * d), lambda i: (layer, 0, 0)),
            vec(layer), vec(layer),
            _resident(ahi.shape), _resident(alo.shape),
            pl.BlockSpec((1, len(POOL_WINDOWS), gw, gw), lambda i: (j, 0, 0, 0)),
            vec(j),
            pl.BlockSpec((1, d, dff), lambda i: (j, 0, 0), pipeline_mode=pl.Buffered(1)),
            pl.BlockSpec((1, d, dff), lambda i: (j, 0, 0), pipeline_mode=pl.Buffered(1)),
            pl.BlockSpec((1, dff, d), lambda i: (j, 0, 0), pipeline_mode=pl.Buffered(1)),
        ],
        out_specs=pl.BlockSpec((tm, d), lambda i: (i, 0)),
        out_shape=jax.ShapeDtypeStruct((nt, d), F32),
        compiler_params=_params(("arbitrary",)),
        name=f"pool_ffn_{layer}",
    )(h, tbl, nw1, nw2, ahi, alo, pw, ps, w1, w3, w2)


def _ssd_in_kernel(nblk, nctx_blk, nbatch, d, d_inner,
                   hp_ref, h_ref, hn_ref, tbl_ref, nw_ref, wz_ref, wx_ref, wdf_ref, wdb_ref,
                   cw_ref, cb_ref, dtb_ref,
                   z_ref, xs_ref, bc_ref, dtf_ref, dtb_out_ref, xbc_scr):
    row, _, j = _block_row(nblk, nctx_blk, nbatch)
    first = jnp.logical_or(j == 0, j == nctx_blk)
    last = jnp.logical_or(j == nctx_blk - 1, j == nblk - 1)
    shift = tbl_ref[0, pl.ds(row, 1), 0:d]
    scale = tbl_ref[0, pl.ds(row, 1), d:2 * d]
    nw = nw_ref[0]

    def modn(x):
        return _rms(x, nw) * (1.0 + scale) + shift

    uf = modn(h_ref[...])
    u = uf.astype(BF16)
    up = modn(hp_ref[...]) * jnp.where(first, 0.0, 1.0)
    un = modn(hn_ref[...]) * jnp.where(last, 0.0, 1.0)
    u_ext = jnp.concatenate([up, uf, un], axis=0).astype(BF16)
    nslab = xbc_scr.shape[0]
    xs_slabs = d_inner // 128
    per = 4

    def project(c0):
        val = _dot(u_ext, wx_ref[0, :, c0 * 128:(c0 + per) * 128])
        for c in range(c0, c0 + per):
            xbc_scr[c, :, :] = val[:, (c - c0) * 128:(c - c0 + 1) * 128]

    def conv(c0):
        for c in range(c0, c0 + per):
            lanes = slice(c * 128, (c + 1) * 128)
            acc = cb_ref[0, :, lanes]
            for k in range(D_CONV):
                off = HALO - CONV_LEFT + k
                acc = acc + xbc_scr[c, off:off + BLK, :] * cw_ref[0, k:k + 1, lanes]
            y = _silu(acc).astype(BF16)
            if c < xs_slabs:
                xs_ref[:, lanes] = y
            else:
                bc_ref[:, (c - xs_slabs) * 128:(c - xs_slabs + 1) * 128] = y

    def softplus(x):
        return jnp.maximum(x, 0.0) + jnp.log1p(jnp.exp(-jnp.abs(x)))

    zw = per * 128
    z_pieces = d_inner // zw
    project(0)
    for i, c0 in enumerate(range(0, nslab, per)):
        if c0 + per < nslab:
            project(c0 + per)
        if i < z_pieces:
            z_ref[:, i * zw:(i + 1) * zw] = _dot(u, wz_ref[0, :, i * zw:(i + 1) * zw]).astype(BF16)
        if i == z_pieces:
            dtf_ref[...] = softplus(_dot(u, wdf_ref[0]) + dtb_ref[0, 0:1, :])
            dtb_out_ref[...] = softplus(_dot(u, wdb_ref[0]) + dtb_ref[0, 1:2, :])
        conv(c0)


def _ssd_in(h, tbl, layer, nw, wz, wx, wdf, wdb, cw, cb, dtb, geom):
    nt, d = h.shape
    nblk, nctx_blk, nbatch = geom
    j = layer // 2
    d_inner = wz.shape[-1]
    conv_dim = wx.shape[-1]
    hb = BLK // HALO
    nh = nt // HALO
    kern = functools.partial(_ssd_in_kernel, nblk, nctx_blk, nbatch, d, d_inner)
    res3 = lambda a: pl.BlockSpec((1,) + a.shape[1:], lambda i: (j, 0, 0), pipeline_mode=pl.Buffered(1))
    return pl.pallas_call(
        kern,
        grid=(nt // BLK,),
        in_specs=[
            pl.BlockSpec((HALO, d), lambda i: (jnp.maximum(i * hb - 1, 0), 0)),
            pl.BlockSpec((BLK, d), lambda i: (i, 0)),
            pl.BlockSpec((HALO, d), lambda i: (jnp.minimum((i + 1) * hb, nh - 1), 0)),
            pl.BlockSpec((1, 8, N_MOD * d), lambda i: (layer, 0, 0)),
            pl.BlockSpec((1, 1, d), lambda i: (layer, 0, 0)),
            res3(wz), res3(wx), res3(wdf), res3(wdb),
            pl.BlockSpec((1, D_CONV, conv_dim), lambda i: (j, 0, 0)),
            pl.BlockSpec((1, 1, conv_dim), lambda i: (j, 0, 0)),
            pl.BlockSpec((1, 2, CHUNK), lambda i: (j, 0, 0)),
        ],
        out_specs=[
            pl.BlockSpec((BLK, d_inner), lambda i: (i, 0)),
            pl.BlockSpec((BLK, d_inner), lambda i: (i, 0)),
            pl.BlockSpec((BLK, conv_dim - d_inner), lambda i: (i, 0)),
            pl.BlockSpec((BLK, CHUNK), lambda i: (i, 0)),
            pl.BlockSpec((BLK, CHUNK), lambda i: (i, 0)),
        ],
        out_shape=[
            jax.ShapeDtypeStruct((nt, d_inner), BF16),
            jax.ShapeDtypeStruct((nt, d_inner), BF16),
            jax.ShapeDtypeStruct((nt, conv_dim - d_inner), BF16),
            jax.ShapeDtypeStruct((nt, CHUNK), F32),
            jax.ShapeDtypeStruct((nt, CHUNK), F32),
        ],
        scratch_shapes=[pltpu.VMEM((conv_dim // 128, BLK + 2 * HALO, 128), F32)],
        compiler_params=_params(("arbitrary",)),
        name=f"ssd_in_{layer}",
    )(h, h, h, tbl, nw, wz, wx, wdf, wdb, cw, cb, dtb)


def _scan_prep(dt, alog, tri, reverse):
    t = dt.shape[0]
    lane = lax.broadcasted_iota(jnp.int32, (1, CHUNK), 1)
    a_row = jnp.where(lane < N_HEADS, -jnp.exp(alog), 0.0)
    a = dt * (a_row * LOG2E)
    p0, p1, p2 = _split3(a)
    cum = _dot(tri, p0) + _dot(tri, p1) + _dot(tri, p2)
    tot = cum[0:1, :] if reverse else cum[t - 1:t, :]
    dte = jnp.exp2(tot - cum)
    ecum = jnp.exp2(cum)
    cdec = jnp.exp2(tot)
    row_t = (cum - jnp.log2(dt)).T

    q0, q1, q2 = _split3(cum)
    stacked = (q0.astype(F32) + pltpu.roll(q1.astype(F32), N_HEADS, 1)
               + pltpu.roll(q2.astype(F32), 2 * N_HEADS, 1)).astype(BF16)
    li = lax.broadcasted_iota(jnp.int32, (t, t), 0)
    si = lax.broadcasted_iota(jnp.int32, (t, t), 1)
    keep = (si >= li) if reverse else (si <= li)
    return dict(wdt=(dt * dte).astype(BF16), ecum=ecum.astype(BF16),
                cdec=_split3(jnp.broadcast_to(cdec, (8, CHUNK))), stacked=stacked, row_t=row_t, keep=keep)


def _scan_expand(g, prep, e_ref, e3_ref):
    heads = N_HEADS // N_GROUPS
    e = e_ref[:, g * heads * HEAD_DIM:(g + 1) * heads * HEAD_DIM]
    c0, c1, c2 = prep["cdec"]
    return dict(
        w_x=_dot(prep["wdt"], e).astype(BF16),
        ec_x=_dot(prep["ecum"], e),
        cd_x=(_dot(c0, e) + _dot(c1, e) + _dot(c2, e))[0:1, :],
        colb=_dot(prep["stacked"], e3_ref[:, g * heads * CHUNK:(g + 1) * heads * CHUNK]),
    )


def _scan_cb(g, bc_ref):
    gn = N_GROUPS * D_STATE
    b_g = bc_ref[:, g * D_STATE:(g + 1) * D_STATE]
    c_g = bc_ref[:, gn + g * D_STATE:gn + (g + 1) * D_STATE]
    return lax.dot_general(c_g, b_g, (((1,), (1,)), ((), ())), preferred_element_type=F32)


def _scan_group(g, prep, ex, cb, x_ref, bc_ref, s_ref, y_ref, dskip):
    t = x_ref.shape[0]
    heads = N_HEADS // N_GROUPS
    gn = N_GROUPS * D_STATE
    gp = heads * HEAD_DIM
    first_head = lax.broadcasted_iota(jnp.int32, (t, 2 * HEAD_DIM), 1) < HEAD_DIM
    b_g = bc_ref[:, g * D_STATE:(g + 1) * D_STATE]
    c_g = bc_ref[:, gn + g * D_STATE:gn + (g + 1) * D_STATE]
    ydiag = []
    for q in range(heads // 2):
        ms = []
        for hl in (2 * q, 2 * q + 1):
            hh = g * heads + hl
            seg = ex["colb"][:, hl * CHUNK:(hl + 1) * CHUNK] - prep["row_t"][hh:hh + 1, :]
            lmat = jnp.exp2(jnp.where(prep["keep"], seg, -1e30))
            ms.append((cb * lmat).astype(BF16))
        m_pair = jnp.concatenate(ms, axis=1)
        x_pair = x_ref[:, (g * heads + 2 * q) * HEAD_DIM:(g * heads + 2 * q + 2) * HEAD_DIM]
        zero = jnp.zeros_like(x_pair)
        rhs = jnp.concatenate([jnp.where(first_head, x_pair, zero),
                               jnp.where(first_head, zero, x_pair)], axis=0)
        ydiag.append(_dot(m_pair, rhs))
    sl = slice(g * gp, (g + 1) * gp)
    xg = x_ref[:, sl]
    s_old = s_ref[:, sl]
    y_g = jnp.concatenate(ydiag, axis=1) + _dot(c_g, s_old.astype(BF16)) * ex["ec_x"]
    if dskip is not None:
        y_g = y_g + xg.astype(F32) * dskip[:, sl]
    y_ref[:, sl] = y_g.astype(y_ref.dtype)
    s_new = lax.dot_general(b_g, xg * ex["w_x"], (((0,), (0,)), ((), ())), preferred_element_type=F32)
    s_ref[:, sl] = s_old * ex["cd_x"] + s_new


def _ssd_scan_kernel(xf_ref, bcf_ref, dtf_ref, xb_ref, bcb_ref, dtb_ref, alog_ref, dsk_ref,
                     tril_ref, triu_ref, e_ref, e3_ref, yf_ref, yb_ref, sf_ref, sb_ref):
    @pl.when(pl.program_id(1) == 0)
    def _():
        sf_ref[...] = jnp.zeros_like(sf_ref)
        sb_ref[...] = jnp.zeros_like(sb_ref)

    pf = _scan_prep(dtf_ref[...], alog_ref[0, 0:1, :], tril_ref[...], False)
    pb = _scan_prep(dtb_ref[...], alog_ref[0, 1:2, :], triu_ref[...], True)
    cbf = [_scan_cb(g, bcf_ref) for g in range(N_GROUPS)]
    cbb = [_scan_cb(g, bcb_ref) for g in range(N_GROUPS)]
    exf = _scan_expand(0, pf, e_ref, e3_ref)
    exb = _scan_expand(0, pb, e_ref, e3_ref)
    for g in range(N_GROUPS):
        nxf = _scan_expand(g + 1, pf, e_ref, e3_ref) if g + 1 < N_GROUPS else None
        _scan_group(g, pf, exf, cbf[g], xf_ref, bcf_ref, sf_ref, yf_ref, dsk_ref[0])
        nxb = _scan_expand(g + 1, pb, e_ref, e3_ref) if g + 1 < N_GROUPS else None
        _scan_group(g, pb, exb, cbb[g], xb_ref, bcb_ref, sb_ref, yb_ref, None)
        exf, exb = nxf, nxb


def _scan_constants():
    li = np.arange(CHUNK)[:, None]
    ti = np.arange(CHUNK)[None, :]
    tril = (ti <= li).astype(np.float32)
    triu = (ti >= li).astype(np.float32)
    e = np.zeros((CHUNK, N_HEADS * HEAD_DIM), np.float32)
    e3 = np.zeros((CHUNK, N_HEADS * CHUNK), np.float32)
    for h in range(N_HEADS):
        e[h, h * HEAD_DIM:(h + 1) * HEAD_DIM] = 1.0
        for piece in range(3):
            e3[piece * N_HEADS + h, h * CHUNK:(h + 1) * CHUNK] = 1.0
    return tuple(jnp.asarray(m, BF16) for m in (tril, triu, e, e3))


def _ssd_scan(xs, bc, dtf, dtb, alog, dskip, layer, consts, nbatch, nchunk, ncc):
    nt, d_inner = xs.shape
    bcw = bc.shape[1]
    j = layer // 2
    tril, triu, e, e3 = consts

    def fwd(b, c):
        return (b * nchunk + c, 0)

    def bwd(b, c):
        return (b * nchunk + jnp.where(c < ncc, ncc - 1 - c, nchunk - 1 - (c - ncc)), 0)

    return pl.pallas_call(
        _ssd_scan_kernel,
        grid=(nbatch, nchunk),
        in_specs=[
            pl.BlockSpec((CHUNK, d_inner), fwd), pl.BlockSpec((CHUNK, bcw), fwd), pl.BlockSpec((CHUNK, CHUNK), fwd),
            pl.BlockSpec((CHUNK, d_inner), bwd), pl.BlockSpec((CHUNK, bcw), bwd), pl.BlockSpec((CHUNK, CHUNK), bwd),
            pl.BlockSpec((1, 2, CHUNK), lambda b, c: (j, 0, 0)),
            pl.BlockSpec((1, 1, d_inner), lambda b, c: (j, 0, 0)),
            _resident(tril.shape), _resident(triu.shape), _resident(e.shape), _resident(e3.shape),
        ],
        out_specs=[pl.BlockSpec((CHUNK, d_inner), fwd), pl.BlockSpec((CHUNK, d_inner), bwd)],
        out_shape=[jax.ShapeDtypeStruct((nt, d_inner), BF16)] * 2,
        scratch_shapes=[pltpu.VMEM((D_STATE, d_inner), F32)] * 2,
        compiler_params=_params(("arbitrary", "arbitrary")),
        name=f"ssd_scan_{layer}",
    )(xs, bc, dtf, xs, bc, dtb, alog, dskip, tril, triu, e, e3)


def _ssd_out_kernel(nblk, nctx_blk, nbatch, d,
                    yf_ref, yb_ref, z_ref, h_ref, tbl_ref, nw_ref, wo_ref, o_ref):
    for sb in range(TILE_BLKS):
        rows = slice(sb * BLK, (sb + 1) * BLK)
        row, _, _ = _block_row(nblk, nctx_blk, nbatch, pl.program_id(0) * TILE_BLKS + sb)
        y = (yf_ref[rows, :].astype(F32) + yb_ref[rows, :].astype(F32)) * _silu(z_ref[rows, :].astype(F32))
        yn = _rms(y, nw_ref[0]).astype(BF16)
        gate = tbl_ref[0, pl.ds(row, 1), 2 * d:3 * d]
        o_ref[rows, :] = h_ref[rows, :] + gate * _dot(yn, wo_ref[0])


def _ssd_out(yf, yb, z, h, tbl, layer, nw, wo, geom):
    nt, d = h.shape
    d_inner = z.shape[1]
    nblk, nctx_blk, nbatch = geom
    j = layer // 2
    kern = functools.partial(_ssd_out_kernel, nblk, nctx_blk, nbatch, d)
    tm = TILE_BLKS * BLK
    big = pl.BlockSpec((tm, d_inner), lambda i: (i, 0))
    return pl.pallas_call(
        kern,
        grid=(nt // tm,),
        in_specs=[
            big, big, big,
            pl.BlockSpec((tm, d), lambda i: (i, 0)),
            pl.BlockSpec((1, 8, N_MOD * d), lambda i: (layer, 0, 0)),
            pl.BlockSpec((1, 1, d_inner), lambda i: (j, 0, 0)),
            pl.BlockSpec((1, d_inner, d), lambda i: (j, 0, 0), pipeline_mode=pl.Buffered(1)),
        ],
        out_specs=pl.BlockSpec((tm, d), lambda i: (i, 0)),
        out_shape=jax.ShapeDtypeStruct((nt, d), F32),
        compiler_params=_params(("arbitrary",)),
        name=f"ssd_out_{layer}",
    )(yf, yb, z, h, tbl, nw, wo)


def _route_kernel(nblk, nctx_blk, nbatch, d,
                  h_ref, tbl_ref, nw_ref, rw_ref, sl_ref, info_ref, cnt_ref, carry_ref):
    @pl.when(pl.program_id(0) == 0)
    def _():
        carry_ref[...] = jnp.zeros_like(carry_ref)

    row, _, _ = _block_row(nblk, nctx_blk, nbatch)
    shift = tbl_ref[0, pl.ds(row, 1), 3 * d:4 * d]
    scale = tbl_ref[0, pl.ds(row, 1), 4 * d:5 * d]
    v = _rms(h_ref[...], nw_ref[0]) * (1.0 + scale) + shift
    lane = lax.broadcasted_iota(jnp.int32, (BLK, CHUNK), 1).astype(F32)
    logits = jnp.where(lane < N_EXPERTS, _dot_hi(v, rw_ref[0]), -jnp.inf)
    m1 = jnp.max(logits, axis=1, keepdims=True)
    i1 = jnp.min(jnp.where(logits == m1, lane, float(CHUNK)), axis=1, keepdims=True)
    rest = jnp.where(lane == i1, -jnp.inf, logits)
    m2 = jnp.max(rest, axis=1, keepdims=True)
    i2 = jnp.min(jnp.where(rest == m2, lane, float(CHUNK)), axis=1, keepdims=True)
    e2 = jnp.exp(m2 - m1)
    g1 = 1.0 / (1.0 + e2)
    g2 = e2 / (1.0 + e2)
    oh1 = (lane == i1)
    oh2 = (lane == i2)
    member = jnp.where(jnp.logical_or(oh1, oh2), 1.0, 0.0)
    before = carry_ref[...] + _dot(sl_ref[...], member.astype(BF16))
    r1 = jnp.sum(jnp.where(oh1, before, 0.0), axis=1, keepdims=True)
    r2 = jnp.sum(jnp.where(oh2, before, 0.0), axis=1, keepdims=True)
    total = carry_ref[...] + jnp.sum(member, axis=0, keepdims=True)
    carry_ref[...] = total
    cnt_ref[...] = jnp.broadcast_to(total, cnt_ref.shape)
    lane8 = lax.broadcasted_iota(jnp.int32, (BLK, 8), 1)
    info = jnp.where(lane8 == 0, i1,
           jnp.where(lane8 == 1, i2,
           jnp.where(lane8 == 2, r1,
           jnp.where(lane8 == 3, r2,
           jnp.where(lane8 == 4, g1,
           jnp.where(lane8 == 5, g2, 0.0))))))
    info_ref[...] = info


def _route(h, tbl, layer, nw, rw, strict_lower, geom):
    nt, d = h.shape
    nblk, nctx_blk, nbatch = geom
    j = layer // 2
    kern = functools.partial(_route_kernel, nblk, nctx_blk, nbatch, d)
    return pl.pallas_call(
        kern,
        grid=(nt // BLK,),
        in_specs=[
            pl.BlockSpec((BLK, d), lambda i: (i, 0)),
            pl.BlockSpec((1, 8, N_MOD * d), lambda i: (layer, 0, 0)),
            pl.BlockSpec((1, 1, d), lambda i: (layer, 0, 0)),
            pl.BlockSpec((1, d, CHUNK), lambda i: (j, 0, 0)),
            _resident(strict_lower.shape),
        ],
        out_specs=[
            pl.BlockSpec((BLK, 8), lambda i: (i, 0)),
            pl.BlockSpec((8, CHUNK), lambda i: (0, 0)),
        ],
        out_shape=[jax.ShapeDtypeStruct((nt, 8), F32), jax.ShapeDtypeStruct((8, CHUNK), F32)],
        scratch_shapes=[pltpu.VMEM((1, CHUNK), F32)],
        compiler_params=_params(("arbitrary",)),
        name=f"route_{layer}",
    )(h, tbl, nw, rw, strict_lower)


def _to_tiles(ref, x, rows):
    for k in range(SUB):
        ref[pl.ds(k, rows, stride=SUB), :] = x[:, k * 128:(k + 1) * 128]


def _from_tiles(ref, rows):
    return jnp.concatenate([ref[pl.ds(k, rows, stride=SUB), :] for k in range(SUB)], axis=1)


def _tile_rows(ref, p):
    return ref.at[pl.ds(pl.multiple_of(p * SUB, SUB), SUB), :]


def _dispatch_kernel(nblk, nctx_blk, nbatch, d,
                     ztile_ref, zvalid_ref, pos_ref, h_ref, tbl_ref, nw_ref, xs_ref, v_scr, z_scr, sem, zsem):
    @pl.when(pl.program_id(0) == 0)
    def _():
        z_scr[...] = jnp.zeros_like(z_scr)
        tile_rows = MOE_TILE * SUB
        for e in range(2 * N_EXPERTS):
            @pl.when(zvalid_ref[e] == 1)
            def _():
                first = pl.multiple_of(ztile_ref[e] * tile_rows, SUB)
                pltpu.make_async_copy(z_scr, xs_ref.at[pl.ds(first, tile_rows), :], zsem).start()
        for e in range(2 * N_EXPERTS):
            @pl.when(zvalid_ref[e] == 1)
            def _():
                pltpu.make_async_copy(z_scr, xs_ref.at[pl.ds(0, tile_rows), :], zsem).wait()

    row, _, _ = _block_row(nblk, nctx_blk, nbatch)
    shift = tbl_ref[0, pl.ds(row, 1), 3 * d:4 * d]
    scale = tbl_ref[0, pl.ds(row, 1), 4 * d:5 * d]
    _to_tiles(v_scr, _rms(h_ref[...], nw_ref[0]) * (1.0 + scale) + shift, BLK)

    def start(i, c):
        for u in range(DMA_UNROLL):
            r = i * DMA_UNROLL + u
            for k in range(2):
                pltpu.make_async_copy(_tile_rows(v_scr, r), _tile_rows(xs_ref, pos_ref[0, 0, 2 * r + k]),
                                      sem).start(priority=k)
        return c

    def wait(i, c):
        for _ in range(2 * DMA_UNROLL):
            pltpu.make_async_copy(v_scr.at[pl.ds(0, SUB), :], xs_ref.at[pl.ds(0, SUB), :], sem).wait()
        return c

    lax.fori_loop(0, BLK // DMA_UNROLL, start, 0)
    lax.fori_loop(0, BLK // DMA_UNROLL, wait, 0)


def _dispatch(h, tbl, layer, nw, pos, ztile, zvalid, nslots, geom):
    nt, d = h.shape
    nblk, nctx_blk, nbatch = geom
    kern = functools.partial(_dispatch_kernel, nblk, nctx_blk, nbatch, d)
    gs = pltpu.PrefetchScalarGridSpec(
        num_scalar_prefetch=2,
        grid=(nt // BLK,),
        in_specs=[
            pl.BlockSpec((1, 1, 2 * BLK), lambda i, zt, zv: (i, 0, 0), memory_space=pltpu.SMEM),
            pl.BlockSpec((BLK, d), lambda i, zt, zv: (i, 0)),
            pl.BlockSpec((1, 8, N_MOD * d), lambda i, zt, zv: (layer, 0, 0)),
            pl.BlockSpec((1, 1, d), lambda i, zt, zv: (layer, 0, 0)),
        ],
        out_specs=pl.BlockSpec(memory_space=pl.ANY),
        scratch_shapes=[pltpu.VMEM((BLK * SUB, 128), F32), pltpu.VMEM((MOE_TILE * SUB, 128), F32),
                        pltpu.SemaphoreType.DMA(()), pltpu.SemaphoreType.DMA(())],
    )
    return pl.pallas_call(
        kern,
        grid_spec=gs,
        out_shape=jax.ShapeDtypeStruct((nslots * SUB, 128), F32),
        compiler_params=_params(("arbitrary",)),
        name=f"dispatch_{layer}",
    )(ztile, zvalid, pos.reshape(nt // BLK, 1, 2 * BLK), h, tbl, nw)


def _expert_kernel(n_fchunk, te_ref, nu_ref, x_ref, w1_ref, w3_ref, w2_ref, o_ref):
    @pl.when(pl.program_id(0) >= nu_ref[0])
    def _():
        o_ref[...] = jnp.zeros_like(o_ref)

    @pl.when(pl.program_id(0) < nu_ref[0])
    def _():
        x = _from_tiles(x_ref, MOE_TILE).astype(BF16)
        acc = None
        for k in range(n_fchunk):
            sl = slice(k * MOE_FCHUNK, (k + 1) * MOE_FCHUNK)
            a = _dot(x, w1_ref[0, 0, :, sl])
            b = _dot(x, w3_ref[0, 0, :, sl])
            act = (_silu(a) * b).astype(BF16)
            part = _dot(act, w2_ref[0, 0, sl, :])
            acc = part if acc is None else acc + part
        _to_tiles(o_ref, acc, MOE_TILE)


def _experts(x_sorted, tile_expert, n_used, j, w1, w3, w2):
    d, dffe = w1.shape[-2:]
    n_tiles = x_sorted.shape[0] // (MOE_TILE * SUB)
    kern = functools.partial(_expert_kernel, dffe // MOE_FCHUNK)
    tile = lambda i, te, nu: (jnp.minimum(i, nu[0] - 1), 0)
    wspec = lambda shape: pl.BlockSpec((1, 1) + shape, lambda i, te, nu: (j, te[i], 0, 0),
                                       pipeline_mode=pl.Buffered(1))
    gs = pltpu.PrefetchScalarGridSpec(
        num_scalar_prefetch=2,
        grid=(n_tiles,),
        in_specs=[pl.BlockSpec((MOE_TILE * SUB, 128), tile), wspec((d, dffe)), wspec((d, dffe)), wspec((dffe, d))],
        out_specs=pl.BlockSpec((MOE_TILE * SUB, 128), lambda i, te, nu: (i, 0)),
    )
    return pl.pallas_call(
        kern,
        grid_spec=gs,
        out_shape=jax.ShapeDtypeStruct(x_sorted.shape, F32),
        compiler_params=_params(("arbitrary",)),
        name=f"experts_{j}",
    )(tile_expert, n_used, x_sorted, w1, w3, w2)


def _combine_kernel(row_of_step, d, final,
                    pos_ref, h_ref, info_ref, tbl_ref, y_ref, *rest):
    if final:
        fw_ref, o_ref, buf0, buf1, sem = rest
    else:
        o_ref, buf0, buf1, sem = rest
    row = row_of_step()

    def start(i, c):
        for u in range(DMA_UNROLL):
            r = i * DMA_UNROLL + u
            for k, buf in enumerate((buf0, buf1)):
                pltpu.make_async_copy(_tile_rows(y_ref, pos_ref[0, 0, 2 * r + k]), _tile_rows(buf, r),
                                      sem).start(priority=k)
        return c

    def wait(i, c):
        for _ in range(2 * DMA_UNROLL):
            pltpu.make_async_copy(y_ref.at[pl.ds(0, SUB), :], buf0.at[pl.ds(0, SUB), :], sem).wait()
        return c

    lax.fori_loop(0, BLK // DMA_UNROLL, start, 0)
    lax.fori_loop(0, BLK // DMA_UNROLL, wait, 0)
    info = info_ref[...]
    g1 = info[:, 4:5]
    g2 = info[:, 5:6]
    gate = tbl_ref[0, pl.ds(row, 1), 5 * d:6 * d]
    out = h_ref[...] + gate * (g1 * _from_tiles(buf0, BLK) + g2 * _from_tiles(buf1, BLK))
    if final:
        o_ref[0] = _rms(out, fw_ref[...])
    else:
        o_ref[...] = out


def _combine(h, info, tbl, layer, y_sorted, pos, geom, final_w=None):
    nt, d = h.shape
    nblk, nctx_blk, nbatch = geom
    final = final_w is not None
    scratch = [pltpu.VMEM((BLK * SUB, 128), F32), pltpu.VMEM((BLK * SUB, 128), F32), pltpu.SemaphoreType.DMA(())]
    pos3 = pos.reshape(nt // BLK, 1, 2 * BLK)
    if final:
        nlat = nblk - nctx_blk
        blk = lambda b, j: b * nblk + nctx_blk + j
        kern = functools.partial(_combine_kernel, lambda: pl.program_id(0), d, True)
        return pl.pallas_call(
            kern,
            grid=(nbatch, nlat),
            in_specs=[
                pl.BlockSpec((1, 1, 2 * BLK), lambda b, j: (blk(b, j), 0, 0), memory_space=pltpu.SMEM),
                pl.BlockSpec((BLK, d), lambda b, j: (blk(b, j), 0)),
                pl.BlockSpec((BLK, 8), lambda b, j: (blk(b, j), 0)),
                pl.BlockSpec((1, 8, N_MOD * d), lambda b, j: (layer, 0, 0)),
                pl.BlockSpec(memory_space=pl.ANY),
                pl.BlockSpec((1, d), lambda b, j: (0, 0)),
            ],
            out_specs=pl.BlockSpec((1, BLK, d), lambda b, j: (b, j, 0)),
            out_shape=jax.ShapeDtypeStruct((nbatch, nlat * BLK, d), F32),
            scratch_shapes=scratch,
            compiler_params=_params(("arbitrary", "arbitrary")),
            name=f"combine_{layer}",
        )(pos3, h, info, tbl, y_sorted, final_w.reshape(1, d))
    kern = functools.partial(_combine_kernel, lambda: _block_row(nblk, nctx_blk, nbatch)[0], d, False)
    return pl.pallas_call(
        kern,
        grid=(nt // BLK,),
        in_specs=[
            pl.BlockSpec((1, 1, 2 * BLK), lambda i: (i, 0, 0), memory_space=pltpu.SMEM),
            pl.BlockSpec((BLK, d), lambda i: (i, 0)),
            pl.BlockSpec((BLK, 8), lambda i: (i, 0)),
            pl.BlockSpec((1, 8, N_MOD * d), lambda i: (layer, 0, 0)),
            pl.BlockSpec(memory_space=pl.ANY),
        ],
        out_specs=pl.BlockSpec((BLK, d), lambda i: (i, 0)),
        out_shape=jax.ShapeDtypeStruct((nt, d), F32),
        scratch_shapes=scratch,
        compiler_params=_params(("arbitrary",)),
        name=f"combine_{layer}",
    )(pos3, h, info, tbl, y_sorted)


def _moe_layer(h, tbl, layer, nw, rw, strict_lower, w1, w3, w2, geom, final_w=None):
    nt, d = h.shape
    j = layer // 2
    info, counts = _route(h, tbl, layer, nw, rw, strict_lower, geom)
    cnt = counts[0, :N_EXPERTS].astype(jnp.int32)
    tiles = (cnt + MOE_TILE - 1) // MOE_TILE
    tile_end = jnp.cumsum(tiles)
    offs = (tile_end - tiles) * MOE_TILE
    n_used = tile_end[-1]
    idx = info[:, 0:2].astype(jnp.int32)
    pos = (offs[idx] + info[:, 2:4].astype(jnp.int32)).reshape(-1)
    n_tiles = (2 * nt) // MOE_TILE + N_EXPERTS
    t = jnp.minimum(jnp.arange(n_tiles, dtype=jnp.int32), n_used - 1)
    tile_expert = jnp.sum(t[:, None] >= tile_end[None, :], axis=1).astype(jnp.int32)
    spare = n_used + jnp.arange(N_EXPERTS, dtype=jnp.int32)
    ztile = jnp.concatenate([tile_end - 1, spare]).astype(jnp.int32)
    zvalid = jnp.concatenate([tiles > 0, spare < n_tiles]).astype(jnp.int32)
    x_sorted = _dispatch(h, tbl, layer, nw, pos, ztile, zvalid, n_tiles * MOE_TILE, geom)
    y_sorted = _experts(x_sorted, tile_expert, n_used.reshape(1).astype(jnp.int32), j, w1, w3, w2)
    return _combine(h, info, tbl, layer, y_sorted, pos, geom, final_w)


def kernel(x, c, ctx, c_ctx, ada_w, ada_b, norm_mix_w, norm_ffn_w, pool_w, pool_scale, ssd_in_w, ssd_conv_w, ssd_conv_b, ssd_A_log, ssd_dt_bias, ssd_D, ssd_norm_w, ssd_out_w, ffn_w1, ffn_w3, ffn_w2, moe_router_w, moe_w1, moe_w3, moe_w2, final_norm_w):
    nbatch, seq, d = x.shape
    ctx_len = ctx.shape[1]
    depth = ada_w.shape[0]
    d_inner = ssd_norm_w.shape[-1]
    assert ctx_len % BLK == 0 and seq % BLK == 0 and nbatch < 8
    assert d_inner == N_HEADS * HEAD_DIM and d % len(POOL_WINDOWS) == 0
    nblk = (ctx_len + seq) // BLK
    nctx_blk = ctx_len // BLK
    geom = (nblk, nctx_blk, nbatch)
    nchunk = (ctx_len + seq) // CHUNK
    ncc = ctx_len // CHUNK

    h = jnp.concatenate([ctx, x], axis=1).reshape(nbatch * (ctx_len + seq), d)
    cvec = jnp.zeros((8, d), F32).at[:nbatch].set(c).at[nbatch].set(c_ctx)
    tbl = _mod_table(cvec, ada_w, ada_b)

    vec3 = lambda a: a.reshape(a.shape[0], 1, a.shape[-1])
    nmix = vec3(norm_mix_w)
    nffn = vec3(norm_ffn_w)
    ahi, alo = _pool_matrices()
    scan_consts = _scan_constants()
    strict_lower = jnp.asarray(np.tril(np.ones((BLK, BLK), np.float32), -1), BF16)

    conv_dim = ssd_conv_w.shape[-1]
    wz = ssd_in_w[:, :, :d_inner].astype(BF16)
    wx = ssd_in_w[:, :, d_inner:d_inner + conv_dim].astype(BF16)
    wdt = ssd_in_w[:, :, d_inner + conv_dim:]
    pad_dt = lambda w: jnp.pad(w, ((0, 0), (0, 0), (0, CHUNK - N_HEADS))).astype(BF16)
    wdf = pad_dt(wdt[:, :, :N_HEADS])
    wdb = pad_dt(wdt[:, :, N_HEADS:])
    pad_h = lambda a: jnp.pad(a, ((0, 0), (0, 0), (0, CHUNK - N_HEADS)))
    dtb = pad_h(ssd_dt_bias)
    alog = pad_h(ssd_A_log)
    dskip = vec3(jnp.repeat(ssd_D, HEAD_DIM, axis=-1))
    rw = jnp.pad(moe_router_w, ((0, 0), (0, 0), (0, CHUNK - N_EXPERTS)))
    pool_wb = pool_w.astype(BF16)
    ffn = [w.astype(BF16) for w in (ffn_w1, ffn_w3, ffn_w2)]
    moe = [w.astype(BF16) for w in (moe_w1, moe_w3, moe_w2)]
    wout = ssd_out_w.astype(BF16)

    for i in range(depth):
        if i % 2 == 0:
            h = _even_layer(h, tbl, i, nmix, nffn, ahi, alo, pool_wb, vec3(pool_scale), *ffn, geom)
        else:
            z, xs, bc, dtf, dtbw = _ssd_in(h, tbl, i, nmix, wz, wx, wdf, wdb, ssd_conv_w,
                                           vec3(ssd_conv_b), dtb, geom)
            yf, yb = _ssd_scan(xs, bc, dtf, dtbw, alog, dskip, i, scan_consts, nbatch, nchunk, ncc)
            h = _ssd_out(yf, yb, z, h, tbl, i, vec3(ssd_norm_w), wout, geom)
            fin = final_norm_w if (i == depth - 1) else None
            h = _moe_layer(h, tbl, i, nffn, rw, strict_lower, *moe, geom, fin)
    if depth % 2 == 1:
        h = _final_norm(h, final_norm_w, nbatch, nblk, nctx_blk)
    return h
```

```python
import functools

import numpy as np
import jax
import jax.numpy as jnp
from jax import lax
from jax.experimental import pallas as pl
from jax.experimental.pallas import tpu as pltpu

F32 = jnp.float32
BF16 = jnp.bfloat16
EPS = 1e-6

BLK = 256
TILE_BLKS = 2
CHUNK = 128
GRID_W = 64
POOL_WINDOWS = (2, 4, 8, 16)
N_MOD = 6
HEAD_DIM = 64
N_HEADS = 32
N_GROUPS = 4
D_STATE = 128
D_CONV = 4
CONV_LEFT = 2
HALO = 8
N_EXPERTS = 8
MOE_TILE = 512
MOE_FCHUNK = 512
FFN_FCHUNK = 256
VMEM_LIMIT = 56 * 2**20
LOG2E = 1.4426950408889634
SUB = 8
DMA_UNROLL = 8


def _dot(a, b):
    return jnp.dot(a, b, preferred_element_type=F32)


def _split2(x):
    hi = x.astype(BF16)
    lo = (x - hi.astype(F32)).astype(BF16)
    return hi, lo


def _split3(x):
    p0 = x.astype(BF16)
    r = x - p0.astype(F32)
    p1 = r.astype(BF16)
    p2 = (r - p1.astype(F32)).astype(BF16)
    return p0, p1, p2


def _dot_hi(a, b):
    ah, al = _split2(a)
    bh, bl = _split2(b)
    return _dot(ah, bh) + _dot(al, bh) + _dot(ah, bl)


def _sigmoid(x):
    return 1.0 / (1.0 + jnp.exp(-x))


def _silu(x):
    return x * _sigmoid(x)


def _rms(x, w):
    ms = jnp.mean(x * x, axis=-1, keepdims=True)
    return x * lax.rsqrt(ms + EPS) * w


def _params(sem):
    return pltpu.CompilerParams(dimension_semantics=sem, vmem_limit_bytes=VMEM_LIMIT)


def _resident(shape):
    nd = len(shape)
    return pl.BlockSpec(shape, lambda *_: (0,) * nd, pipeline_mode=pl.Buffered(1))


def _block_row(nblk, nctx_blk, nbatch, blk=None):
    if blk is None:
        blk = pl.program_id(0)
    b = blk // nblk
    j = blk - b * nblk
    is_ctx = j < nctx_blk
    return jnp.where(is_ctx, nbatch, b), is_ctx, j


def _mod_kernel(c_ref, w_ref, b_ref, o_ref):
    o_ref[0] = _dot_hi(_silu(c_ref[...]), w_ref[0]) + b_ref[0]


def _mod_table(cvec, ada_w, ada_b):
    depth, d, n = ada_w.shape
    tn = 512
    return pl.pallas_call(
        _mod_kernel,
        grid=(depth, n // tn),
        in_specs=[
            pl.BlockSpec((8, d), lambda l, j: (0, 0)),
            pl.BlockSpec((1, d, tn), lambda l, j: (l, 0, j)),
            pl.BlockSpec((1, 1, tn), lambda l, j: (l, 0, j)),
        ],
        out_specs=pl.BlockSpec((1, 8, tn), lambda l, j: (l, 0, j)),
        out_shape=jax.ShapeDtypeStruct((depth, 8, n), F32),
        compiler_params=_params(("arbitrary", "arbitrary")),
        name="mod_table",
    )(cvec, ada_w, ada_b.reshape(depth, 1, n))


def _even_kernel(nblk, nctx_blk, nbatch, d,
                 h_ref, tbl_ref, nw1_ref, nw2_ref, ahi_ref, alo_ref, pw_ref, ps_ref,
                 w1_ref, w3_ref, w2_ref, o_ref):
    def mods(sb):
        row, is_ctx, _ = _block_row(nblk, nctx_blk, nbatch, pl.program_id(0) * TILE_BLKS + sb)
        return (lambda k: tbl_ref[0, pl.ds(row, 1), k * d:(k + 1) * d]), is_ctx.astype(jnp.int32)

    def mixer(sb):
        mod, kind = mods(sb)
        h = h_ref[sb * BLK:(sb + 1) * BLK, :]
        u = _rms(h, nw1_ref[0]) * (1.0 + mod(1)) + mod(0)
        gw = d // len(POOL_WINDOWS)
        ys = []
        for g in range(len(POOL_WINDOWS)):
            ug = u[:, g * gw:(g + 1) * gw]
            uh, ul = _split2(ug)
            ah = ahi_ref[kind, g]
            al = alo_ref[kind, g]
            p = _dot(ah, uh) + _dot(al, uh) + _dot(ah, ul) - ug
            ys.append(_dot(p.astype(BF16), pw_ref[0, g]))
        y = jnp.concatenate(ys, axis=1) * ps_ref[0]
        h1 = h + mod(2) * y
        v = (_rms(h1, nw2_ref[0]) * (1.0 + mod(4)) + mod(3)).astype(BF16)
        return h1, v

    dff = w1_ref.shape[-1]
    n_f = dff // FFN_FCHUNK

    def up(v, k):
        cols = slice(k * FFN_FCHUNK, (k + 1) * FFN_FCHUNK)
        return _dot(v, w1_ref[0, :, cols]), _dot(v, w3_ref[0, :, cols])

    def down(ab, k):
        act = (_silu(ab[0]) * ab[1]).astype(BF16)
        return _dot(act, w2_ref[0, k * FFN_FCHUNK:(k + 1) * FFN_FCHUNK, :])

    mixed = [mixer(sb) for sb in range(TILE_BLKS)]
    v = jnp.concatenate([m[1] for m in mixed], axis=0)
    ab = up(v, 0)
    acc = None
    for k in range(n_f):
        ab_next = up(v, k + 1) if k + 1 < n_f else None
        part = down(ab, k)
        acc = part if acc is None else acc + part
        ab = ab_next
    for sb in range(TILE_BLKS):
        mod, _ = mods(sb)
        o_ref[sb * BLK:(sb + 1) * BLK, :] = mixed[sb][0] + mod(5) * acc[sb * BLK:(sb + 1) * BLK, :]


def _pool_matrices():
    mats = np.zeros((2, len(POOL_WINDOWS), BLK, BLK), np.float64)
    for kind, seg in enumerate((GRID_W, BLK)):
        for g, w in enumerate(POOL_WINDOWS):
            lo = w // 2
            hi = w - 1 - lo
            for t in range(BLK):
                base = (t // seg) * seg
                tt = t - base
                start = max(tt - lo, 0)
                end = min(tt + hi + 1, seg)
                mats[kind, g, t, base + start:base + end] = 1.0 / (end - start)
    m32 = jnp.asarray(mats, F32)
    hi = m32.astype(BF16)
    lo = (m32 - hi.astype(F32)).astype(BF16)
    return hi, lo


def _even_layer(h, tbl, layer, nw1, nw2, ahi, alo, pw, ps, w1, w3, w2, geom):
    nt, d = h.shape
    nblk, nctx_blk, nbatch = geom
    dff = w1.shape[-1]
    j = layer // 2
    gw = d // len(POOL_WINDOWS)
    kern = functools.partial(_even_kernel, nblk, nctx_blk, nbatch, d)
    vec = lambda idx: pl.BlockSpec((1, 1, d), lambda i: (idx, 0, 0))
    tm = TILE_BLKS * BLK
    return pl.pallas_call(
        kern,
        grid=(nt // tm,),
        in_specs=[
            pl.BlockSpec((tm, d), lambda i: (i, 0)),
            pl.BlockSpec((1, 8, N_MOD * d), lambda i: (layer, 0, 0)),
            vec(layer), vec(layer),
            _resident(ahi.shape), _resident(alo.shape),
            pl.BlockSpec((1, len(POOL_WINDOWS), gw, gw), lambda i: (j, 0, 0, 0)),
            vec(j),
            pl.BlockSpec((1, d, dff), lambda i: (j, 0, 0), pipeline_mode=pl.Buffered(1)),
            pl.BlockSpec((1, d, dff), lambda i: (j, 0, 0), pipeline_mode=pl.Buffered(1)),
            pl.BlockSpec((1, dff, d), lambda i: (j, 0, 0), pipeline_mode=pl.Buffered(1)),
        ],
        out_specs=pl.BlockSpec((tm, d), lambda i: (i, 0)),
        out_shape=jax.ShapeDtypeStruct((nt, d), F32),
        compiler_params=_params(("arbitrary",)),
        name=f"pool_ffn_{layer}",
    )(h, tbl, nw1, nw2, ahi, alo, pw, ps, w1, w3, w2)


def _ssd_in_kernel(nblk, nctx_blk, nbatch, d, d_inner,
                   hp_ref, h_ref, hn_ref, tbl_ref, nw_ref, wz_ref, wx_ref, wdf_ref, wdb_ref,
                   cw_ref, cb_ref, dtb_ref,
                   z_ref, xs_ref, bc_ref, dtf_ref, dtb_out_ref, xbc_scr):
    row, _, j = _block_row(nblk, nctx_blk, nbatch)
    first = jnp.logical_or(j == 0, j == nctx_blk)
    last = jnp.logical_or(j == nctx_blk - 1, j == nblk - 1)
    shift = tbl_ref[0, pl.ds(row, 1), 0:d]
    scale = tbl_ref[0, pl.ds(row, 1), d:2 * d]
    nw = nw_ref[0]

    def modn(x):
        return _rms(x, nw) * (1.0 + scale) + shift

    uf = modn(h_ref[...])
    u = uf.astype(BF16)
    up = modn(hp_ref[...]) * jnp.where(first, 0.0, 1.0)
    un = modn(hn_ref[...]) * jnp.where(last, 0.0, 1.0)
    u_ext = jnp.concatenate([up, uf, un], axis=0).astype(BF16)
    nslab = xbc_scr.shape[0]
    xs_slabs = d_inner // 128
    per = 4

    def project(c0):
        val = _dot(u_ext, wx_ref[0, :, c0 * 128:(c0 + per) * 128])
        for c in range(c0, c0 + per):
            xbc_scr[c, :, :] = val[:, (c - c0) * 128:(c - c0 + 1) * 128]

    def conv(c0):
        for c in range(c0, c0 + per):
            lanes = slice(c * 128, (c + 1) * 128)
            acc = cb_ref[0, :, lanes]
            for k in range(D_CONV):
                off = HALO - CONV_LEFT + k
                acc = acc + xbc_scr[c, off:off + BLK, :] * cw_ref[0, k:k + 1, lanes]
            y = _silu(acc).astype(BF16)
            if c < xs_slabs:
                xs_ref[:, lanes] = y
            else:
                bc_ref[:, (c - xs_slabs) * 128:(c - xs_slabs + 1) * 128] = y

    def softplus(x):
        return jnp.maximum(x, 0.0) + jnp.log1p(jnp.exp(-jnp.abs(x)))

    zw = per * 128
    z_pieces = d_inner // zw
    project(0)
    for i, c0 in enumerate(range(0, nslab, per)):
        if c0 + per < nslab:
            project(c0 + per)
        if i < z_pieces:
            z_ref[:, i * zw:(i + 1) * zw] = _dot(u, wz_ref[0, :, i * zw:(i + 1) * zw]).astype(BF16)
        if i == z_pieces:
            dtf_ref[...] = softplus(_dot(u, wdf_ref[0]) + dtb_ref[0, 0:1, :])
            dtb_out_ref[...] = softplus(_dot(u, wdb_ref[0]) + dtb_ref[0, 1:2, :])
        conv(c0)


def _ssd_in(h, tbl, layer, nw, wz, wx, wdf, wdb, cw, cb, dtb, geom):
    nt, d = h.shape
    nblk, nctx_blk, nbatch = geom
    j = layer // 2
    d_inner = wz.shape[-1]
    conv_dim = wx.shape[-1]
    hb = BLK // HALO
    nh = nt // HALO
    kern = functools.partial(_ssd_in_kernel, nblk, nctx_blk, nbatch, d, d_inner)
    res3 = lambda a: pl.BlockSpec((1,) + a.shape[1:], lambda i: (j, 0, 0), pipeline_mode=pl.Buffered(1))
    return pl.pallas_call(
        kern,
        grid=(nt // BLK,),
        in_specs=[
            pl.BlockSpec((HALO, d), lambda i: (jnp.maximum(i * hb - 1, 0), 0)),
            pl.BlockSpec((BLK, d), lambda i: (i, 0)),
            pl.BlockSpec((HALO, d), lambda i: (jnp.minimum((i + 1) * hb, nh - 1), 0)),
            pl.BlockSpec((1, 8, N_MOD * d), lambda i: (layer, 0, 0)),
            pl.BlockSpec((1, 1, d), lambda i: (layer, 0, 0)),
            res3(wz), res3(wx), res3(wdf), res3(wdb),
            pl.BlockSpec((1, D_CONV, conv_dim), lambda i: (j, 0, 0)),
            pl.BlockSpec((1, 1, conv_dim), lambda i: (j, 0, 0)),
            pl.BlockSpec((1, 2, CHUNK), lambda i: (j, 0, 0)),
        ],
        out_specs=[
            pl.BlockSpec((BLK, d_inner), lambda i: (i, 0)),
            pl.BlockSpec((BLK, d_inner), lambda i: (i, 0)),
            pl.BlockSpec((BLK, conv_dim - d_inner), lambda i: (i, 0)),
            pl.BlockSpec((BLK, CHUNK), lambda i: (i, 0)),
            pl.BlockSpec((BLK, CHUNK), lambda i: (i, 0)),
        ],
        out_shape=[
            jax.ShapeDtypeStruct((nt, d_inner), BF16),
            jax.ShapeDtypeStruct((nt, d_inner), BF16),
            jax.ShapeDtypeStruct((nt, conv_dim - d_inner), BF16),
            jax.ShapeDtypeStruct((nt, CHUNK), F32),
            jax.ShapeDtypeStruct((nt, CHUNK), F32),
        ],
        scratch_shapes=[pltpu.VMEM((conv_dim // 128, BLK + 2 * HALO, 128), F32)],
        compiler_params=_params(("arbitrary",)),
        name=f"ssd_in_{layer}",
    )(h, h, h, tbl, nw, wz, wx, wdf, wdb, cw, cb, dtb)


def _scan_prep(dt, alog, tri, reverse):
    t = dt.shape[0]
    lane = lax.broadcasted_iota(jnp.int32, (1, CHUNK), 1)
    a_row = jnp.where(lane < N_HEADS, -jnp.exp(alog), 0.0)
    a = dt * (a_row * LOG2E)
    p0, p1, p2 = _split3(a)
    cum = _dot(tri, p0) + _dot(tri, p1) + _dot(tri, p2)
    tot = cum[0:1, :] if reverse else cum[t - 1:t, :]
    dte = jnp.exp2(tot - cum)
    ecum = jnp.exp2(cum)
    cdec = jnp.exp2(tot)
    row_t = (cum - jnp.log2(dt)).T

    q0, q1, q2 = _split3(cum)
    stacked = (q0.astype(F32) + pltpu.roll(q1.astype(F32), N_HEADS, 1)
               + pltpu.roll(q2.astype(F32), 2 * N_HEADS, 1)).astype(BF16)
    li = lax.broadcasted_iota(jnp.int32, (t, t), 0)
    si = lax.broadcasted_iota(jnp.int32, (t, t), 1)
    keep = (si >= li) if reverse else (si <= li)
    return dict(wdt=(dt * dte).astype(BF16), ecum=ecum.astype(BF16),
                cdec=_split3(jnp.broadcast_to(cdec, (8, CHUNK))), stacked=stacked, row_t=row_t, keep=keep)


def _scan_expand(g, prep, e_ref, e3_ref):
    heads = N_HEADS // N_GROUPS
    e = e_ref[:, g * heads * HEAD_DIM:(g + 1) * heads * HEAD_DIM]
    c0, c1, c2 = prep["cdec"]
    return dict(
        w_x=_dot(prep["wdt"], e).astype(BF16),
        ec_x=_dot(prep["ecum"], e),
        cd_x=(_dot(c0, e) + _dot(c1, e) + _dot(c2, e))[0:1, :],
        colb=_dot(prep["stacked"], e3_ref[:, g * heads * CHUNK:(g + 1) * heads * CHUNK]),
    )


def _scan_cb(g, bc_ref):
    gn = N_GROUPS * D_STATE
    b_g = bc_ref[:, g * D_STATE:(g + 1) * D_STATE]
    c_g = bc_ref[:, gn + g * D_STATE:gn + (g + 1) * D_STATE]
    return lax.dot_general(c_g, b_g, (((1,), (1,)), ((), ())), preferred_element_type=F32)


def _scan_group(g, prep, ex, cb, x_ref, bc_ref, s_ref, y_ref, dskip):
    t = x_ref.shape[0]
    heads = N_HEADS // N_GROUPS
    gn = N_GROUPS * D_STATE
    gp = heads * HEAD_DIM
    first_head = lax.broadcasted_iota(jnp.int32, (t, 2 * HEAD_DIM), 1) < HEAD_DIM
    b_g = bc_ref[:, g * D_STATE:(g + 1) * D_STATE]
    c_g = bc_ref[:, gn + g * D_STATE:gn + (g + 1) * D_STATE]
    ydiag = []
    for q in range(heads // 2):
        ms = []
        for hl in (2 * q, 2 * q + 1):
            hh = g * heads + hl
            seg = ex["colb"][:, hl * CHUNK:(hl + 1) * CHUNK] - prep["row_t"][hh:hh + 1, :]
            lmat = jnp.exp2(jnp.where(prep["keep"], seg, -1e30))
            ms.append((cb * lmat).astype(BF16))
        m_pair = jnp.concatenate(ms, axis=1)
        x_pair = x_ref[:, (g * heads + 2 * q) * HEAD_DIM:(g * heads + 2 * q + 2) * HEAD_DIM]
        zero = jnp.zeros_like(x_pair)
        rhs = jnp.concatenate([jnp.where(first_head, x_pair, zero),
                               jnp.where(first_head, zero, x_pair)], axis=0)
        ydiag.append(_dot(m_pair, rhs))
    sl = slice(g * gp, (g + 1) * gp)
    xg = x_ref[:, sl]
    s_old = s_ref[:, sl]
    y_g = jnp.concatenate(ydiag, axis=1) + _dot(c_g, s_old.astype(BF16)) * ex["ec_x"]
    if dskip is not None:
        y_g = y_g + xg.astype(F32) * dskip[:, sl]
    y_ref[:, sl] = y_g.astype(y_ref.dtype)
    s_new = lax.dot_general(b_g, xg * ex["w_x"], (((0,), (0,)), ((), ())), preferred_element_type=F32)
    s_ref[:, sl] = s_old * ex["cd_x"] + s_new


def _ssd_scan_kernel(xf_ref, bcf_ref, dtf_ref, xb_ref, bcb_ref, dtb_ref, alog_ref, dsk_ref,
                     tril_ref, triu_ref, e_ref, e3_ref, yf_ref, yb_ref, sf_ref, sb_ref):
    @pl.when(pl.program_id(1) == 0)
    def _():
        sf_ref[...] = jnp.zeros_like(sf_ref)
        sb_ref[...] = jnp.zeros_like(sb_ref)

    pf = _scan_prep(dtf_ref[...], alog_ref[0, 0:1, :], tril_ref[...], False)
    pb = _scan_prep(dtb_ref[...], alog_ref[0, 1:2, :], triu_ref[...], True)
    cbf = [_scan_cb(g, bcf_ref) for g in range(N_GROUPS)]
    cbb = [_scan_cb(g, bcb_ref) for g in range(N_GROUPS)]
    exf = _scan_expand(0, pf, e_ref, e3_ref)
    exb = _scan_expand(0, pb, e_ref, e3_ref)
    for g in range(N_GROUPS):
        nxf = _scan_expand(g + 1, pf, e_ref, e3_ref) if g + 1 < N_GROUPS else None
        _scan_group(g, pf, exf, cbf[g], xf_ref, bcf_ref, sf_ref, yf_ref, dsk_ref[0])
        nxb = _scan_expand(g + 1, pb, e_ref, e3_ref) if g + 1 < N_GROUPS else None
        _scan_group(g, pb, exb, cbb[g], xb_ref, bcb_ref, sb_ref, yb_ref, None)
        exf, exb = nxf, nxb


def _scan_constants():
    li = np.arange(CHUNK)[:, None]
    ti = np.arange(CHUNK)[None, :]
    tril = (ti <= li).astype(np.float32)
    triu = (ti >= li).astype(np.float32)
    e = np.zeros((CHUNK, N_HEADS * HEAD_DIM), np.float32)
    e3 = np.zeros((CHUNK, N_HEADS * CHUNK), np.float32)
    for h in range(N_HEADS):
        e[h, h * HEAD_DIM:(h + 1) * HEAD_DIM] = 1.0
        for piece in range(3):
            e3[piece * N_HEADS + h, h * CHUNK:(h + 1) * CHUNK] = 1.0
    return tuple(jnp.asarray(m, BF16) for m in (tril, triu, e, e3))


def _ssd_scan(xs, bc, dtf, dtb, alog, dskip, layer, consts, nbatch, nchunk, ncc):
    nt, d_inner = xs.shape
    bcw = bc.shape[1]
    j = layer // 2
    tril, triu, e, e3 = consts

    def fwd(b, c):
        return (b * nchunk + c, 0)

    def bwd(b, c):
        return (b * nchunk + jnp.where(c < ncc, ncc - 1 - c, nchunk - 1 - (c - ncc)), 0)

    return pl.pallas_call(
        _ssd_scan_kernel,
        grid=(nbatch, nchunk),
        in_specs=[
            pl.BlockSpec((CHUNK, d_inner), fwd), pl.BlockSpec((CHUNK, bcw), fwd), pl.BlockSpec((CHUNK, CHUNK), fwd),
            pl.BlockSpec((CHUNK, d_inner), bwd), pl.BlockSpec((CHUNK, bcw), bwd), pl.BlockSpec((CHUNK, CHUNK), bwd),
            pl.BlockSpec((1, 2, CHUNK), lambda b, c: (j, 0, 0)),
            pl.BlockSpec((1, 1, d_inner), lambda b, c: (j, 0, 0)),
            _resident(tril.shape), _resident(triu.shape), _resident(e.shape), _resident(e3.shape),
        ],
        out_specs=[pl.BlockSpec((CHUNK, d_inner), fwd), pl.BlockSpec((CHUNK, d_inner), bwd)],
        out_shape=[jax.ShapeDtypeStruct((nt, d_inner), BF16)] * 2,
        scratch_shapes=[pltpu.VMEM((D_STATE, d_inner), F32)] * 2,
        compiler_params=_params(("arbitrary", "arbitrary")),
        name=f"ssd_scan_{layer}",
    )(xs, bc, dtf, xs, bc, dtb, alog, dskip, tril, triu, e, e3)


def _ssd_out_kernel(nblk, nctx_blk, nbatch, d,
                    yf_ref, yb_ref, z_ref, h_ref, tbl_ref, nw_ref, wo_ref, o_ref):
    y = (yf_ref[...].astype(F32) + yb_ref[...].astype(F32)) * _silu(z_ref[...].astype(F32))
    proj = _dot(_rms(y, nw_ref[0]).astype(BF16), wo_ref[0])
    for sb in range(TILE_BLKS):
        rows = slice(sb * BLK, (sb + 1) * BLK)
        row, _, _ = _block_row(nblk, nctx_blk, nbatch, pl.program_id(0) * TILE_BLKS + sb)
        gate = tbl_ref[0, pl.ds(row, 1), 2 * d:3 * d]
        o_ref[rows, :] = h_ref[rows, :] + gate * proj[rows, :]


def _ssd_out(yf, yb, z, h, tbl, layer, nw, wo, geom):
    nt, d = h.shape
    d_inner = z.shape[1]
    nblk, nctx_blk, nbatch = geom
    j = layer // 2
    kern = functools.partial(_ssd_out_kernel, nblk, nctx_blk, nbatch, d)
    tm = TILE_BLKS * BLK
    big = pl.BlockSpec((tm, d_inner), lambda i: (i, 0))
    return pl.pallas_call(
        kern,
        grid=(nt // tm,),
        in_specs=[
            big, big, big,
            pl.BlockSpec((tm, d), lambda i: (i, 0)),
            pl.BlockSpec((1, 8, N_MOD * d), lambda i: (layer, 0, 0)),
            pl.BlockSpec((1, 1, d_inner), lambda i: (j, 0, 0)),
            pl.BlockSpec((1, d_inner, d), lambda i: (j, 0, 0), pipeline_mode=pl.Buffered(1)),
        ],
        out_specs=pl.BlockSpec((tm, d), lambda i: (i, 0)),
        out_shape=jax.ShapeDtypeStruct((nt, d), F32),
        compiler_params=_params(("arbitrary",)),
        name=f"ssd_out_{layer}",
    )(yf, yb, z, h, tbl, nw, wo)


def _route_kernel(nblk, nctx_blk, nbatch, d,
                  h_ref, tbl_ref, nw_ref, rw_ref, sl_ref, info_ref, cnt_ref, carry_ref):
    @pl.when(pl.program_id(0) == 0)
    def _():
        carry_ref[...] = jnp.zeros_like(carry_ref)

    row, _, _ = _block_row(nblk, nctx_blk, nbatch)
    shift = tbl_ref[0, pl.ds(row, 1), 3 * d:4 * d]
    scale = tbl_ref[0, pl.ds(row, 1), 4 * d:5 * d]
    v = _rms(h_ref[...], nw_ref[0]) * (1.0 + scale) + shift
    lane = lax.broadcasted_iota(jnp.int32, (BLK, CHUNK), 1).astype(F32)
    logits = jnp.where(lane < N_EXPERTS, _dot_hi(v, rw_ref[0]), -jnp.inf)
    m1 = jnp.max(logits, axis=1, keepdims=True)
    i1 = jnp.min(jnp.where(logits == m1, lane, float(CHUNK)), axis=1, keepdims=True)
    rest = jnp.where(lane == i1, -jnp.inf, logits)
    m2 = jnp.max(rest, axis=1, keepdims=True)
    i2 = jnp.min(jnp.where(rest == m2, lane, float(CHUNK)), axis=1, keepdims=True)
    e2 = jnp.exp(m2 - m1)
    g1 = 1.0 / (1.0 + e2)
    g2 = e2 / (1.0 + e2)
    oh1 = (lane == i1)
    oh2 = (lane == i2)
    member = jnp.where(jnp.logical_or(oh1, oh2), 1.0, 0.0)
    before = carry_ref[...] + _dot(sl_ref[...], member.astype(BF16))
    r1 = jnp.sum(jnp.where(oh1, before, 0.0), axis=1, keepdims=True)
    r2 = jnp.sum(jnp.where(oh2, before, 0.0), axis=1, keepdims=True)
    total = carry_ref[...] + jnp.sum(member, axis=0, keepdims=True)
    carry_ref[...] = total
    cnt_ref[...] = jnp.broadcast_to(total, cnt_ref.shape)
    lane8 = lax.broadcasted_iota(jnp.int32, (BLK, 8), 1)
    info = jnp.where(lane8 == 0, i1,
           jnp.where(lane8 == 1, i2,
           jnp.where(lane8 == 2, r1,
           jnp.where(lane8 == 3, r2,
           jnp.where(lane8 == 4, g1,
           jnp.where(lane8 == 5, g2, 0.0))))))
    info_ref[...] = info


def _route(h, tbl, layer, nw, rw, strict_lower, geom):
    nt, d = h.shape
    nblk, nctx_blk, nbatch = geom
    j = layer // 2
    kern = functools.partial(_route_kernel, nblk, nctx_blk, nbatch, d)
    return pl.pallas_call(
        kern,
        grid=(nt // BLK,),
        in_specs=[
            pl.BlockSpec((BLK, d), lambda i: (i, 0)),
            pl.BlockSpec((1, 8, N_MOD * d), lambda i: (layer, 0, 0)),
            pl.BlockSpec((1, 1, d), lambda i: (layer, 0, 0)),
            pl.BlockSpec((1, d, CHUNK), lambda i: (j, 0, 0)),
            _resident(strict_lower.shape),
        ],
        out_specs=[
            pl.BlockSpec((BLK, 8), lambda i: (i, 0)),
            pl.BlockSpec((8, CHUNK), lambda i: (0, 0)),
        ],
        out_shape=[jax.ShapeDtypeStruct((nt, 8), F32), jax.ShapeDtypeStruct((8, CHUNK), F32)],
        scratch_shapes=[pltpu.VMEM((1, CHUNK), F32)],
        compiler_params=_params(("arbitrary",)),
        name=f"route_{layer}",
    )(h, tbl, nw, rw, strict_lower)


def _to_tiles(ref, x, rows):
    for k in range(SUB):
        ref[pl.ds(k, rows, stride=SUB), :] = x[:, k * 128:(k + 1) * 128]


def _from_tiles(ref, rows):
    return jnp.concatenate([ref[pl.ds(k, rows, stride=SUB), :] for k in range(SUB)], axis=1)


def _tile_rows(ref, p):
    return ref.at[pl.ds(pl.multiple_of(p * SUB, SUB), SUB), :]


def _dispatch_kernel(nblk, nctx_blk, nbatch, d,
                     ztile_ref, zvalid_ref, pos_ref, h_ref, tbl_ref, nw_ref, xs_ref, v_scr, z_scr, sem, zsem):
    @pl.when(pl.program_id(0) == 0)
    def _():
        z_scr[...] = jnp.zeros_like(z_scr)
        tile_rows = MOE_TILE * SUB
        for e in range(2 * N_EXPERTS):
            @pl.when(zvalid_ref[e] == 1)
            def _():
                first = pl.multiple_of(ztile_ref[e] * tile_rows, SUB)
                pltpu.make_async_copy(z_scr, xs_ref.at[pl.ds(first, tile_rows), :], zsem).start()
        for e in range(2 * N_EXPERTS):
            @pl.when(zvalid_ref[e] == 1)
            def _():
                pltpu.make_async_copy(z_scr, xs_ref.at[pl.ds(0, tile_rows), :], zsem).wait()

    step = pl.program_id(0)
    nsteps = pl.num_programs(0)
    slot = step % 2
    v_slot = v_scr.at[slot]

    def wait_block(s):
        def body(i, c):
            for _ in range(2 * DMA_UNROLL):
                pltpu.make_async_copy(v_scr.at[s, pl.ds(0, SUB), :], xs_ref.at[pl.ds(0, SUB), :], sem.at[s]).wait()
            return c
        lax.fori_loop(0, BLK // DMA_UNROLL, body, 0)

    @pl.when(step >= 2)
    def _():
        wait_block(slot)

    row, _, _ = _block_row(nblk, nctx_blk, nbatch)
    shift = tbl_ref[0, pl.ds(row, 1), 3 * d:4 * d]
    scale = tbl_ref[0, pl.ds(row, 1), 4 * d:5 * d]
    _to_tiles(v_slot, _rms(h_ref[...], nw_ref[0]) * (1.0 + scale) + shift, BLK)

    def start(i, c):
        for u in range(DMA_UNROLL):
            r = i * DMA_UNROLL + u
            for k in range(2):
                pltpu.make_async_copy(_tile_rows(v_slot, r), _tile_rows(xs_ref, pos_ref[0, 0, 2 * r + k]),
                                      sem.at[slot]).start(priority=k)
        return c

    lax.fori_loop(0, BLK // DMA_UNROLL, start, 0)

    @pl.when(step == nsteps - 1)
    def _():
        wait_block(slot)

        @pl.when(nsteps >= 2)
        def _():
            wait_block(1 - slot)


def _dispatch(h, tbl, layer, nw, pos, ztile, zvalid, nslots, geom):
    nt, d = h.shape
    nblk, nctx_blk, nbatch = geom
    kern = functools.partial(_dispatch_kernel, nblk, nctx_blk, nbatch, d)
    gs = pltpu.PrefetchScalarGridSpec(
        num_scalar_prefetch=2,
        grid=(nt // BLK,),
        in_specs=[
            pl.BlockSpec((1, 1, 2 * BLK), lambda i, zt, zv: (i, 0, 0), memory_space=pltpu.SMEM),
            pl.BlockSpec((BLK, d), lambda i, zt, zv: (i, 0)),
            pl.BlockSpec((1, 8, N_MOD * d), lambda i, zt, zv: (layer, 0, 0)),
            pl.BlockSpec((1, 1, d), lambda i, zt, zv: (layer, 0, 0)),
        ],
        out_specs=pl.BlockSpec(memory_space=pl.ANY),
        scratch_shapes=[pltpu.VMEM((2, BLK * SUB, 128), F32), pltpu.VMEM((MOE_TILE * SUB, 128), F32),
                        pltpu.SemaphoreType.DMA((2,)), pltpu.SemaphoreType.DMA(())],
    )
    return pl.pallas_call(
        kern,
        grid_spec=gs,
        out_shape=jax.ShapeDtypeStruct((nslots * SUB, 128), F32),
        compiler_params=_params(("arbitrary",)),
        name=f"dispatch_{layer}",
    )(ztile, zvalid, pos.reshape(nt // BLK, 1, 2 * BLK), h, tbl, nw)


def _expert_kernel(n_fchunk, te_ref, nu_ref, x_ref, w1_ref, w3_ref, w2_ref, o_ref):
    @pl.when(pl.program_id(0) >= nu_ref[0])
    def _():
        o_ref[...] = jnp.zeros_like(o_ref)

    @pl.when(pl.program_id(0) < nu_ref[0])
    def _():
        x = _from_tiles(x_ref, MOE_TILE).astype(BF16)
        acc = None
        for k in range(n_fchunk):
            sl = slice(k * MOE_FCHUNK, (k + 1) * MOE_FCHUNK)
            a = _dot(x, w1_ref[0, 0, :, sl])
            b = _dot(x, w3_ref[0, 0, :, sl])
            act = (_silu(a) * b).astype(BF16)
            part = _dot(act, w2_ref[0, 0, sl, :])
            acc = part if acc is None else acc + part
        _to_tiles(o_ref, acc, MOE_TILE)


def _experts(x_sorted, tile_expert, n_used, j, w1, w3, w2):
    d, dffe = w1.shape[-2:]
    n_tiles = x_sorted.shape[0] // (MOE_TILE * SUB)
    kern = functools.partial(_expert_kernel, dffe // MOE_FCHUNK)
    tile = lambda i, te, nu: (jnp.minimum(i, nu[0] - 1), 0)
    wspec = lambda shape: pl.BlockSpec((1, 1) + shape, lambda i, te, nu: (j, te[i], 0, 0),
                                       pipeline_mode=pl.Buffered(1))
    gs = pltpu.PrefetchScalarGridSpec(
        num_scalar_prefetch=2,
        grid=(n_tiles,),
        in_specs=[pl.BlockSpec((MOE_TILE * SUB, 128), tile), wspec((d, dffe)), wspec((d, dffe)), wspec((dffe, d))],
        out_specs=pl.BlockSpec((MOE_TILE * SUB, 128), lambda i, te, nu: (i, 0)),
    )
    return pl.pallas_call(
        kern,
        grid_spec=gs,
        out_shape=jax.ShapeDtypeStruct(x_sorted.shape, F32),
        compiler_params=_params(("arbitrary",)),
        name=f"experts_{j}",
    )(tile_expert, n_used, x_sorted, w1, w3, w2)


def _combine_kernel(row_of_step, d, final,
                    pos_ref, posn_ref, h_ref, info_ref, tbl_ref, y_ref, *rest):
    if final:
        fw_ref, o_ref, buf, sem = rest
    else:
        o_ref, buf, sem = rest
    step = pl.program_id(0)
    nsteps = pl.num_programs(0)
    slot = step % 2

    def gather(p_ref, s):
        def body(i, c):
            for u in range(DMA_UNROLL):
                r = i * DMA_UNROLL + u
                for k in range(2):
                    pltpu.make_async_copy(_tile_rows(y_ref, p_ref[0, 0, 2 * r + k]), _tile_rows(buf.at[s, k], r),
                                          sem.at[s]).start(priority=k)
            return c
        lax.fori_loop(0, BLK // DMA_UNROLL, body, 0)

    @pl.when(step == 0)
    def _():
        gather(pos_ref, 0)

    @pl.when(step + 1 < nsteps)
    def _():
        gather(posn_ref, 1 - slot)

    def wait(i, c):
        for _ in range(2 * DMA_UNROLL):
            pltpu.make_async_copy(y_ref.at[pl.ds(0, SUB), :], buf.at[slot, 0, pl.ds(0, SUB), :], sem.at[slot]).wait()
        return c

    lax.fori_loop(0, BLK // DMA_UNROLL, wait, 0)
    info = info_ref[...]
    g1 = info[:, 4:5]
    g2 = info[:, 5:6]
    gate = tbl_ref[0, pl.ds(row_of_step(step), 1), 5 * d:6 * d]
    out = h_ref[...] + gate * (g1 * _from_tiles(buf.at[slot, 0], BLK) + g2 * _from_tiles(buf.at[slot, 1], BLK))
    if final:
        o_ref[0] = _rms(out, fw_ref[...])
    else:
        o_ref[...] = out


def _combine(h, info, tbl, layer, y_sorted, pos, geom, final_w=None):
    nt, d = h.shape
    nblk, nctx_blk, nbatch = geom
    final = final_w is not None
    pos3 = pos.reshape(nt // BLK, 1, 2 * BLK)
    if final:
        nlat = nblk - nctx_blk
        nsteps = nbatch * nlat
        blk = lambda i: (i // nlat) * nblk + nctx_blk + i % nlat
        row_of_step = lambda step: step // nlat
        out_spec = pl.BlockSpec((1, BLK, d), lambda i: (i // nlat, i % nlat, 0))
        out_shape = jax.ShapeDtypeStruct((nbatch, nlat * BLK, d), F32)
    else:
        nsteps = nt // BLK
        blk = lambda i: i
        row_of_step = lambda step: _block_row(nblk, nctx_blk, nbatch, step)[0]
        out_spec = pl.BlockSpec((BLK, d), lambda i: (i, 0))
        out_shape = jax.ShapeDtypeStruct((nt, d), F32)
    nxt = lambda i: blk(jnp.minimum(i + 1, nsteps - 1))
    in_specs = [
        pl.BlockSpec((1, 1, 2 * BLK), lambda i: (blk(i), 0, 0), memory_space=pltpu.SMEM),
        pl.BlockSpec((1, 1, 2 * BLK), lambda i: (nxt(i), 0, 0), memory_space=pltpu.SMEM),
        pl.BlockSpec((BLK, d), lambda i: (blk(i), 0)),
        pl.BlockSpec((BLK, 8), lambda i: (blk(i), 0)),
        pl.BlockSpec((1, 8, N_MOD * d), lambda i: (layer, 0, 0)),
        pl.BlockSpec(memory_space=pl.ANY),
    ]
    args = [pos3, pos3, h, info, tbl, y_sorted]
    if final:
        in_specs.append(pl.BlockSpec((1, d), lambda i: (0, 0)))
        args.append(final_w.reshape(1, d))
    return pl.pallas_call(
        functools.partial(_combine_kernel, row_of_step, d, final),
        grid=(nsteps,),
        in_specs=in_specs,
        out_specs=out_spec,
        out_shape=out_shape,
        scratch_shapes=[pltpu.VMEM((2, 2, BLK * SUB, 128), F32), pltpu.SemaphoreType.DMA((2,))],
        compiler_params=_params(("arbitrary",)),
        name=f"combine_{layer}",
    )(*args)


def _moe_layer(h, tbl, layer, nw, rw, strict_lower, w1, w3, w2, geom, final_w=None):
    nt, d = h.shape
    j = layer // 2
    info, counts = _route(h, tbl, layer, nw, rw, strict_lower, geom)
    cnt = counts[0, :N_EXPERTS].astype(jnp.int32)
    tiles = (cnt + MOE_TILE - 1) // MOE_TILE
    tile_end = jnp.cumsum(tiles)
    offs = (tile_end - tiles) * MOE_TILE
    n_used = tile_end[-1]
    idx = info[:, 0:2].astype(jnp.int32)
    pos = (offs[idx] + info[:, 2:4].astype(jnp.int32)).reshape(-1)
    n_tiles = (2 * nt) // MOE_TILE + N_EXPERTS
    t = jnp.minimum(jnp.arange(n_tiles, dtype=jnp.int32), n_used - 1)
    tile_expert = jnp.sum(t[:, None] >= tile_end[None, :], axis=1).astype(jnp.int32)
    spare = n_used + jnp.arange(N_EXPERTS, dtype=jnp.int32)
    ztile = jnp.concatenate([tile_end - 1, spare]).astype(jnp.int32)
    zvalid = jnp.concatenate([tiles > 0, spare < n_tiles]).astype(jnp.int32)
    x_sorted = _dispatch(h, tbl, layer, nw, pos, ztile, zvalid, n_tiles * MOE_TILE, geom)
    y_sorted = _experts(x_sorted, tile_expert, n_used.reshape(1).astype(jnp.int32), j, w1, w3, w2)
    return _combine(h, info, tbl, layer, y_sorted, pos, geom, final_w)


def kernel(x, c, ctx, c_ctx, ada_w, ada_b, norm_mix_w, norm_ffn_w, pool_w, pool_scale, ssd_in_w, ssd_conv_w, ssd_conv_b, ssd_A_log, ssd_dt_bias, ssd_D, ssd_norm_w, ssd_out_w, ffn_w1, ffn_w3, ffn_w2, moe_router_w, moe_w1, moe_w3, moe_w2, final_norm_w):
    nbatch, seq, d = x.shape
    ctx_len = ctx.shape[1]
    depth = ada_w.shape[0]
    d_inner = ssd_norm_w.shape[-1]
    assert ctx_len % BLK == 0 and seq % BLK == 0 and nbatch < 8
    assert d_inner == N_HEADS * HEAD_DIM and d % len(POOL_WINDOWS) == 0
    nblk = (ctx_len + seq) // BLK
    nctx_blk = ctx_len // BLK
    geom = (nblk, nctx_blk, nbatch)
    nchunk = (ctx_len + seq) // CHUNK
    ncc = ctx_len // CHUNK

    h = jnp.concatenate([ctx, x], axis=1).reshape(nbatch * (ctx_len + seq), d)
    cvec = jnp.zeros((8, d), F32).at[:nbatch].set(c).at[nbatch].set(c_ctx)
    tbl = _mod_table(cvec, ada_w, ada_b)

    vec3 = lambda a: a.reshape(a.shape[0], 1, a.shape[-1])
    nmix = vec3(norm_mix_w)
    nffn = vec3(norm_ffn_w)
    ahi, alo = _pool_matrices()
    scan_consts = _scan_constants()
    strict_lower = jnp.asarray(np.tril(np.ones((BLK, BLK), np.float32), -1), BF16)

    conv_dim = ssd_conv_w.shape[-1]
    wz = ssd_in_w[:, :, :d_inner].astype(BF16)
    wx = ssd_in_w[:, :, d_inner:d_inner + conv_dim].astype(BF16)
    wdt = ssd_in_w[:, :, d_inner + conv_dim:]
    pad_dt = lambda w: jnp.pad(w, ((0, 0), (0, 0), (0, CHUNK - N_HEADS))).astype(BF16)
    wdf = pad_dt(wdt[:, :, :N_HEADS])
    wdb = pad_dt(wdt[:, :, N_HEADS:])
    pad_h = lambda a: jnp.pad(a, ((0, 0), (0, 0), (0, CHUNK - N_HEADS)))
    dtb = pad_h(ssd_dt_bias)
    alog = pad_h(ssd_A_log)
    dskip = vec3(jnp.repeat(ssd_D, HEAD_DIM, axis=-1))
    rw = jnp.pad(moe_router_w, ((0, 0), (0, 0), (0, CHUNK - N_EXPERTS)))
    pool_wb = pool_w.astype(BF16)
    ffn = [w.astype(BF16) for w in (ffn_w1, ffn_w3, ffn_w2)]
    moe = [w.astype(BF16) for w in (moe_w1, moe_w3, moe_w2)]
    wout = ssd_out_w.astype(BF16)

    for i in range(depth):
        if i % 2 == 0:
            h = _even_layer(h, tbl, i, nmix, nffn, ahi, alo, pool_wb, vec3(pool_scale), *ffn, geom)
        else:
            z, xs, bc, dtf, dtbw = _ssd_in(h, tbl, i, nmix, wz, wx, wdf, wdb, ssd_conv_w,
                                           vec3(ssd_conv_b), dtb, geom)
            yf, yb = _ssd_scan(xs, bc, dtf, dtbw, alog, dskip, i, scan_consts, nbatch, nchunk, ncc)
            h = _ssd_out(yf, yb, z, h, tbl, i, vec3(ssd_norm_w), wout, geom)
            fin = final_norm_w if (i == depth - 1) else None
            h = _moe_layer(h, tbl, i, nffn, rw, strict_lower, *moe, geom, fin)
    if depth % 2 == 1:
        h = _final_norm(h, final_norm_w, nbatch, nblk, nctx_blk)
    return h
```

```python
import functools

import numpy as np
import jax
import jax.numpy as jnp
from jax import lax
from jax.experimental import pallas as pl
from jax.experimental.pallas import tpu as pltpu

F32 = jnp.float32
BF16 = jnp.bfloat16
EPS = 1e-6

BLK = 256
TILE_BLKS = 2
CHUNK = 128
SCAN_CHUNKS = 2
GRID_W = 64
POOL_WINDOWS = (2, 4, 8, 16)
N_MOD = 6
HEAD_DIM = 64
N_HEADS = 32
N_GROUPS = 4
D_STATE = 128
D_CONV = 4
CONV_LEFT = 2
HALO = 8
N_EXPERTS = 8
MOE_TILE = 512
MOE_FCHUNK = 512
FFN_FCHUNK = 256
VMEM_LIMIT = 56 * 2**20
LOG2E = 1.4426950408889634
SUB = 8
DMA_UNROLL = 8


def _dot(a, b):
    return jnp.dot(a, b, preferred_element_type=F32)


def _split2(x):
    hi = x.astype(BF16)
    lo = (x - hi.astype(F32)).astype(BF16)
    return hi, lo


def _split3(x):
    p0 = x.astype(BF16)
    r = x - p0.astype(F32)
    p1 = r.astype(BF16)
    p2 = (r - p1.astype(F32)).astype(BF16)
    return p0, p1, p2


def _dot_hi(a, b):
    ah, al = _split2(a)
    bh, bl = _split2(b)
    return _dot(ah, bh) + _dot(al, bh) + _dot(ah, bl)


def _sigmoid(x):
    return 1.0 / (1.0 + jnp.exp(-x))


def _silu(x):
    return x * _sigmoid(x)


def _rms(x, w):
    ms = jnp.mean(x * x, axis=-1, keepdims=True)
    return x * lax.rsqrt(ms + EPS) * w


def _params(sem):
    return pltpu.CompilerParams(dimension_semantics=sem, vmem_limit_bytes=VMEM_LIMIT)


def _resident(shape):
    nd = len(shape)
    return pl.BlockSpec(shape, lambda *_: (0,) * nd, pipeline_mode=pl.Buffered(1))


def _block_row(nblk, nctx_blk, nbatch, blk=None):
    if blk is None:
        blk = pl.program_id(0)
    b = blk // nblk
    j = blk - b * nblk
    is_ctx = j < nctx_blk
    return jnp.where(is_ctx, nbatch, b), is_ctx, j


def _mod_kernel(c_ref, w_ref, b_ref, o_ref):
    o_ref[0] = _dot_hi(_silu(c_ref[...]), w_ref[0]) + b_ref[0]


def _mod_table(cvec, ada_w, ada_b):
    depth, d, n = ada_w.shape
    tn = 512
    return pl.pallas_call(
        _mod_kernel,
        grid=(depth, n // tn),
        in_specs=[
            pl.BlockSpec((8, d), lambda l, j: (0, 0)),
            pl.BlockSpec((1, d, tn), lambda l, j: (l, 0, j)),
            pl.BlockSpec((1, 1, tn), lambda l, j: (l, 0, j)),
        ],
        out_specs=pl.BlockSpec((1, 8, tn), lambda l, j: (l, 0, j)),
        out_shape=jax.ShapeDtypeStruct((depth, 8, n), F32),
        compiler_params=_params(("arbitrary", "arbitrary")),
        name="mod_table",
    )(cvec, ada_w, ada_b.reshape(depth, 1, n))


def _even_kernel(nblk, nctx_blk, nbatch, d,
                 h_ref, tbl_ref, nw1_ref, nw2_ref, ahi_ref, alo_ref, pw_ref, ps_ref,
                 w1_ref, w3_ref, w2_ref, o_ref):
    def mods(sb):
        row, is_ctx, _ = _block_row(nblk, nctx_blk, nbatch, pl.program_id(0) * TILE_BLKS + sb)
        return (lambda k: tbl_ref[0, pl.ds(row, 1), k * d:(k + 1) * d]), is_ctx.astype(jnp.int32)

    def mixer(sb):
        mod, kind = mods(sb)
        h = h_ref[sb * BLK:(sb + 1) * BLK, :]
        u = _rms(h, nw1_ref[0]) * (1.0 + mod(1)) + mod(0)
        gw = d // len(POOL_WINDOWS)
        ys = []
        for g in range(len(POOL_WINDOWS)):
            ug = u[:, g * gw:(g + 1) * gw]
            uh, ul = _split2(ug)
            ah = ahi_ref[kind, g]
            al = alo_ref[kind, g]
            p = _dot(ah, uh) + _dot(al, uh) + _dot(ah, ul) - ug
            ys.append(_dot(p.astype(BF16), pw_ref[0, g]))
        y = jnp.concatenate(ys, axis=1) * ps_ref[0]
        h1 = h + mod(2) * y
        v = (_rms(h1, nw2_ref[0]) * (1.0 + mod(4)) + mod(3)).astype(BF16)
        return h1, v

    dff = w1_ref.shape[-1]
    n_f = dff // FFN_FCHUNK

    def up(v, k):
        cols = slice(k * FFN_FCHUNK, (k + 1) * FFN_FCHUNK)
        return _dot(v, w1_ref[0, :, cols]), _dot(v, w3_ref[0, :, cols])

    def down(ab, k):
        act = (_silu(ab[0]) * ab[1]).astype(BF16)
        return _dot(act, w2_ref[0, k * FFN_FCHUNK:(k + 1) * FFN_FCHUNK, :])

    mixed = [mixer(sb) for sb in range(TILE_BLKS)]
    v = jnp.concatenate([m[1] for m in mixed], axis=0)
    ab = up(v, 0)
    acc = None
    for k in range(n_f):
        ab_next = up(v, k + 1) if k + 1 < n_f else None
        part = down(ab, k)
        acc = part if acc is None else acc + part
        ab = ab_next
    for sb in range(TILE_BLKS):
        mod, _ = mods(sb)
        o_ref[sb * BLK:(sb + 1) * BLK, :] = mixed[sb][0] + mod(5) * acc[sb * BLK:(sb + 1) * BLK, :]


def _pool_matrices():
    mats = np.zeros((2, len(POOL_WINDOWS), BLK, BLK), np.float64)
    for kind, seg in enumerate((GRID_W, BLK)):
        for g, w in enumerate(POOL_WINDOWS):
            lo = w // 2
            hi = w - 1 - lo
            for t in range(BLK):
                base = (t // seg) * seg
                tt = t - base
                start = max(tt - lo, 0)
                end = min(tt + hi + 1, seg)
                mats[kind, g, t, base + start:base + end] = 1.0 / (end - start)
    m32 = jnp.asarray(mats, F32)
    hi = m32.astype(BF16)
    lo = (m32 - hi.astype(F32)).astype(BF16)
    return hi, lo


def _even_layer(h, tbl, layer, nw1, nw2, ahi, alo, pw, ps, w1, w3, w2, geom):
    nt, d = h.shape
    nblk, nctx_blk, nbatch = geom
    dff = w1.shape[-1]
    j = layer // 2
    gw = d // len(POOL_WINDOWS)
    kern = functools.partial(_even_kernel, nblk, nctx_blk, nbatch, d)
    vec = lambda idx: pl.BlockSpec((1, 1, d), lambda i: (idx, 0, 0))
    tm = TILE_BLKS * BLK
    return pl.pallas_call(
        kern,
        grid=(nt // tm,),
        in_specs=[
            pl.BlockSpec((tm, d), lambda i: (i, 0)),
            pl.BlockSpec((1, 8, N_MOD * d), lambda i: (layer, 0, 0)),
            vec(layer), vec(layer),
            _resident(ahi.shape), _resident(alo.shape),
            pl.BlockSpec((1, len(POOL_WINDOWS), gw, gw), lambda i: (j, 0, 0, 0)),
            vec(j),
            pl.BlockSpec((1, d, dff), lambda i: (j, 0, 0), pipeline_mode=pl.Buffered(1)),
            pl.BlockSpec((1, d, dff), lambda i: (j, 0, 0), pipeline_mode=pl.Buffered(1)),
            pl.BlockSpec((1, dff, d), lambda i: (j, 0, 0), pipeline_mode=pl.Buffered(1)),
        ],
        out_specs=pl.BlockSpec((tm, d), lambda i: (i, 0)),
        out_shape=jax.ShapeDtypeStruct((nt, d), F32),
        compiler_params=_params(("arbitrary",)),
        name=f"pool_ffn_{layer}",
    )(h, tbl, nw1, nw2, ahi, alo, pw, ps, w1, w3, w2)


def _ssd_in_kernel(nblk, nctx_blk, nbatch, d, d_inner,
                   hp_ref, h_ref, hn_ref, tbl_ref, nw_ref, wz_ref, wx_ref, wdf_ref, wdb_ref,
                   cw_ref, cb_ref, dtb_ref,
                   z_ref, xs_ref, bc_ref, dtf_ref, dtb_out_ref, xbc_scr):
    row, _, j = _block_row(nblk, nctx_blk, nbatch)
    first = jnp.logical_or(j == 0, j == nctx_blk)
    last = jnp.logical_or(j == nctx_blk - 1, j == nblk - 1)
    shift = tbl_ref[0, pl.ds(row, 1), 0:d]
    scale = tbl_ref[0, pl.ds(row, 1), d:2 * d]
    nw = nw_ref[0]

    def modn(x):
        return _rms(x, nw) * (1.0 + scale) + shift

    uf = modn(h_ref[...])
    u = uf.astype(BF16)
    up = modn(hp_ref[...]) * jnp.where(first, 0.0, 1.0)
    un = modn(hn_ref[...]) * jnp.where(last, 0.0, 1.0)
    u_ext = jnp.concatenate([up, uf, un], axis=0).astype(BF16)
    nslab = xbc_scr.shape[0]
    xs_slabs = d_inner // 128
    per = 4

    def project(c0):
        val = _dot(u_ext, wx_ref[0, :, c0 * 128:(c0 + per) * 128])
        for c in range(c0, c0 + per):
            xbc_scr[c, :, :] = val[:, (c - c0) * 128:(c - c0 + 1) * 128]

    def conv(c0):
        for c in range(c0, c0 + per):
            lanes = slice(c * 128, (c + 1) * 128)
            acc = cb_ref[0, :, lanes]
            for k in range(D_CONV):
                off = HALO - CONV_LEFT + k
                acc = acc + xbc_scr[c, off:off + BLK, :] * cw_ref[0, k:k + 1, lanes]
            y = _silu(acc).astype(BF16)
            if c < xs_slabs:
                xs_ref[:, lanes] = y
            else:
                bc_ref[:, (c - xs_slabs) * 128:(c - xs_slabs + 1) * 128] = y

    def softplus(x):
        return jnp.maximum(x, 0.0) + jnp.log1p(jnp.exp(-jnp.abs(x)))

    zw = per * 128
    z_pieces = d_inner // zw
    project(0)
    for i, c0 in enumerate(range(0, nslab, per)):
        if c0 + per < nslab:
            project(c0 + per)
        if i < z_pieces:
            z_ref[:, i * zw:(i + 1) * zw] = _dot(u, wz_ref[0, :, i * zw:(i + 1) * zw]).astype(BF16)
        if i == z_pieces:
            dtf_ref[...] = softplus(_dot(u, wdf_ref[0]) + dtb_ref[0, 0:1, :])
            dtb_out_ref[...] = softplus(_dot(u, wdb_ref[0]) + dtb_ref[0, 1:2, :])
        conv(c0)


def _ssd_in(h, tbl, layer, nw, wz, wx, wdf, wdb, cw, cb, dtb, geom):
    nt, d = h.shape
    nblk, nctx_blk, nbatch = geom
    j = layer // 2
    d_inner = wz.shape[-1]
    conv_dim = wx.shape[-1]
    hb = BLK // HALO
    nh = nt // HALO
    kern = functools.partial(_ssd_in_kernel, nblk, nctx_blk, nbatch, d, d_inner)
    res3 = lambda a: pl.BlockSpec((1,) + a.shape[1:], lambda i: (j, 0, 0), pipeline_mode=pl.Buffered(1))
    return pl.pallas_call(
        kern,
        grid=(nt // BLK,),
        in_specs=[
            pl.BlockSpec((HALO, d), lambda i: (jnp.maximum(i * hb - 1, 0), 0)),
            pl.BlockSpec((BLK, d), lambda i: (i, 0)),
            pl.BlockSpec((HALO, d), lambda i: (jnp.minimum((i + 1) * hb, nh - 1), 0)),
            pl.BlockSpec((1, 8, N_MOD * d), lambda i: (layer, 0, 0)),
            pl.BlockSpec((1, 1, d), lambda i: (layer, 0, 0)),
            res3(wz), res3(wx), res3(wdf), res3(wdb),
            pl.BlockSpec((1, D_CONV, conv_dim), lambda i: (j, 0, 0)),
            pl.BlockSpec((1, 1, conv_dim), lambda i: (j, 0, 0)),
            pl.BlockSpec((1, 2, CHUNK), lambda i: (j, 0, 0)),
        ],
        out_specs=[
            pl.BlockSpec((BLK, d_inner), lambda i: (i, 0)),
            pl.BlockSpec((BLK, d_inner), lambda i: (i, 0)),
            pl.BlockSpec((BLK, conv_dim - d_inner), lambda i: (i, 0)),
            pl.BlockSpec((BLK, CHUNK), lambda i: (i, 0)),
            pl.BlockSpec((BLK, CHUNK), lambda i: (i, 0)),
        ],
        out_shape=[
            jax.ShapeDtypeStruct((nt, d_inner), BF16),
            jax.ShapeDtypeStruct((nt, d_inner), BF16),
            jax.ShapeDtypeStruct((nt, conv_dim - d_inner), BF16),
            jax.ShapeDtypeStruct((nt, CHUNK), F32),
            jax.ShapeDtypeStruct((nt, CHUNK), F32),
        ],
        scratch_shapes=[pltpu.VMEM((conv_dim // 128, BLK + 2 * HALO, 128), F32)],
        compiler_params=_params(("arbitrary",)),
        name=f"ssd_in_{layer}",
    )(h, h, h, tbl, nw, wz, wx, wdf, wdb, cw, cb, dtb)


def _scan_prep(dt, alog, tri, reverse):
    t = dt.shape[0]
    lane = lax.broadcasted_iota(jnp.int32, (1, CHUNK), 1)
    a_row = jnp.where(lane < N_HEADS, -jnp.exp(alog), 0.0)
    a = dt * (a_row * LOG2E)
    p0, p1, p2 = _split3(a)
    cum = _dot(tri, p0) + _dot(tri, p1) + _dot(tri, p2)
    tot = cum[0:1, :] if reverse else cum[t - 1:t, :]
    dte = jnp.exp2(tot - cum)
    ecum = jnp.exp2(cum)
    cdec = jnp.exp2(tot)
    row_t = (cum - jnp.log2(dt)).T

    q0, q1, q2 = _split3(cum)
    stacked = (q0.astype(F32) + pltpu.roll(q1.astype(F32), N_HEADS, 1)
               + pltpu.roll(q2.astype(F32), 2 * N_HEADS, 1)).astype(BF16)
    li = lax.broadcasted_iota(jnp.int32, (t, t), 0)
    si = lax.broadcasted_iota(jnp.int32, (t, t), 1)
    keep = (si >= li) if reverse else (si <= li)
    return dict(wdt=(dt * dte).astype(BF16), ecum=ecum.astype(BF16),
                cdec=_split3(jnp.broadcast_to(cdec, (8, CHUNK))), stacked=stacked, row_t=row_t, keep=keep)


def _scan_expand(g, preps, e_ref, e3_ref):
    heads = N_HEADS // N_GROUPS
    e = e_ref[:, g * heads * HEAD_DIM:(g + 1) * heads * HEAD_DIM]
    stack = lambda key: jnp.concatenate([p[key] for p in preps], axis=0)
    w_x = _dot(stack("wdt"), e).astype(BF16)
    ec_x = _dot(stack("ecum"), e)
    cd_x = sum(_dot(jnp.concatenate([p["cdec"][i] for p in preps], axis=0), e) for i in range(3))
    colb = _dot(stack("stacked"), e3_ref[:, g * heads * CHUNK:(g + 1) * heads * CHUNK])
    rows = lambda a, n: a[n * CHUNK:(n + 1) * CHUNK, :]
    return [dict(w_x=rows(w_x, n), ec_x=rows(ec_x, n), cd_x=cd_x[8 * n:8 * n + 1, :], colb=rows(colb, n))
            for n in range(len(preps))]


def _scan_cb(g, bc_ref, rows):
    gn = N_GROUPS * D_STATE
    b_g = bc_ref[rows, g * D_STATE:(g + 1) * D_STATE]
    c_g = bc_ref[rows, gn + g * D_STATE:gn + (g + 1) * D_STATE]
    return lax.dot_general(c_g, b_g, (((1,), (1,)), ((), ())), preferred_element_type=F32)


def _scan_group(g, prep, ex, cb, x_ref, bc_ref, s_ref, y_ref, dskip, rows):
    t = CHUNK
    heads = N_HEADS // N_GROUPS
    gn = N_GROUPS * D_STATE
    gp = heads * HEAD_DIM
    first_head = lax.broadcasted_iota(jnp.int32, (t, 2 * HEAD_DIM), 1) < HEAD_DIM
    b_g = bc_ref[rows, g * D_STATE:(g + 1) * D_STATE]
    c_g = bc_ref[rows, gn + g * D_STATE:gn + (g + 1) * D_STATE]
    def decay_pair(q):
        ms = []
        for hl in (2 * q, 2 * q + 1):
            hh = g * heads + hl
            seg = ex["colb"][:, hl * CHUNK:(hl + 1) * CHUNK] - prep["row_t"][hh:hh + 1, :]
            lmat = jnp.exp2(jnp.where(prep["keep"], seg, -1e30))
            ms.append((cb * lmat).astype(BF16))
        return jnp.concatenate(ms, axis=1)

    ydiag = []
    m_pair = decay_pair(0)
    for q in range(heads // 2):
        m_next = decay_pair(q + 1) if q + 1 < heads // 2 else None
        x_pair = x_ref[rows, (g * heads + 2 * q) * HEAD_DIM:(g * heads + 2 * q + 2) * HEAD_DIM]
        zero = jnp.zeros_like(x_pair)
        rhs = jnp.concatenate([jnp.where(first_head, x_pair, zero),
                               jnp.where(first_head, zero, x_pair)], axis=0)
        ydiag.append(_dot(m_pair, rhs))
        m_pair = m_next
    sl = slice(g * gp, (g + 1) * gp)
    xg = x_ref[rows, sl]
    s_old = s_ref[:, sl]
    y_g = jnp.concatenate(ydiag, axis=1) + _dot(c_g, s_old.astype(BF16)) * ex["ec_x"]
    if dskip is not None:
        y_g = y_g + xg.astype(F32) * dskip[:, sl]
    y_ref[rows, sl] = y_g.astype(y_ref.dtype)
    s_new = lax.dot_general(b_g, xg * ex["w_x"], (((0,), (0,)), ((), ())), preferred_element_type=F32)
    s_ref[:, sl] = s_old * ex["cd_x"] + s_new


def _ssd_scan_kernel(xf_ref, bcf_ref, dtf_ref, xb_ref, bcb_ref, dtb_ref, alog_ref, dsk_ref,
                     tril_ref, triu_ref, e_ref, e3_ref, yf_ref, yb_ref, sf_ref, sb_ref):
    @pl.when(pl.program_id(1) == 0)
    def _():
        sf_ref[...] = jnp.zeros_like(sf_ref)
        sb_ref[...] = jnp.zeros_like(sb_ref)

    rows_f = [slice(n * CHUNK, (n + 1) * CHUNK) for n in range(SCAN_CHUNKS)]
    rows_b = rows_f[::-1]
    pf = [_scan_prep(dtf_ref[r, :], alog_ref[0, 0:1, :], tril_ref[...], False) for r in rows_f]
    pb = [_scan_prep(dtb_ref[r, :], alog_ref[0, 1:2, :], triu_ref[...], True) for r in rows_b]
    cbf = [[_scan_cb(g, bcf_ref, r) for r in rows_f] for g in range(N_GROUPS)]
    cbb = [[_scan_cb(g, bcb_ref, r) for r in rows_b] for g in range(N_GROUPS)]
    exf = _scan_expand(0, pf, e_ref, e3_ref)
    exb = _scan_expand(0, pb, e_ref, e3_ref)
    for g in range(N_GROUPS):
        nxf = _scan_expand(g + 1, pf, e_ref, e3_ref) if g + 1 < N_GROUPS else None
        nxb = _scan_expand(g + 1, pb, e_ref, e3_ref) if g + 1 < N_GROUPS else None
        for n in range(SCAN_CHUNKS):
            _scan_group(g, pf[n], exf[n], cbf[g][n], xf_ref, bcf_ref, sf_ref, yf_ref, dsk_ref[0], rows_f[n])
            _scan_group(g, pb[n], exb[n], cbb[g][n], xb_ref, bcb_ref, sb_ref, yb_ref, None, rows_b[n])
        exf, exb = nxf, nxb


def _scan_constants():
    li = np.arange(CHUNK)[:, None]
    ti = np.arange(CHUNK)[None, :]
    tril = (ti <= li).astype(np.float32)
    triu = (ti >= li).astype(np.float32)
    e = np.zeros((CHUNK, N_HEADS * HEAD_DIM), np.float32)
    e3 = np.zeros((CHUNK, N_HEADS * CHUNK), np.float32)
    for h in range(N_HEADS):
        e[h, h * HEAD_DIM:(h + 1) * HEAD_DIM] = 1.0
        for piece in range(3):
            e3[piece * N_HEADS + h, h * CHUNK:(h + 1) * CHUNK] = 1.0
    return tuple(jnp.asarray(m, BF16) for m in (tril, triu, e, e3))


def _ssd_scan(xs, bc, dtf, dtb, alog, dskip, layer, consts, nbatch, nchunk, ncc):
    nt, d_inner = xs.shape
    bcw = bc.shape[1]
    j = layer // 2
    tril, triu, e, e3 = consts

    assert nchunk % SCAN_CHUNKS == 0 and ncc % SCAN_CHUNKS == 0
    nstep = nchunk // SCAN_CHUNKS
    ncs = ncc // SCAN_CHUNKS
    rows = SCAN_CHUNKS * CHUNK

    def fwd(b, c):
        return (b * nstep + c, 0)

    def bwd(b, c):
        return (b * nstep + jnp.where(c < ncs, ncs - 1 - c, nstep - 1 - (c - ncs)), 0)

    return pl.pallas_call(
        _ssd_scan_kernel,
        grid=(nbatch, nstep),
        in_specs=[
            pl.BlockSpec((rows, d_inner), fwd), pl.BlockSpec((rows, bcw), fwd), pl.BlockSpec((rows, CHUNK), fwd),
            pl.BlockSpec((rows, d_inner), bwd), pl.BlockSpec((rows, bcw), bwd), pl.BlockSpec((rows, CHUNK), bwd),
            pl.BlockSpec((1, 2, CHUNK), lambda b, c: (j, 0, 0)),
            pl.BlockSpec((1, 1, d_inner), lambda b, c: (j, 0, 0)),
            _resident(tril.shape), _resident(triu.shape), _resident(e.shape), _resident(e3.shape),
        ],
        out_specs=[pl.BlockSpec((rows, d_inner), fwd), pl.BlockSpec((rows, d_inner), bwd)],
        out_shape=[jax.ShapeDtypeStruct((nt, d_inner), BF16)] * 2,
        scratch_shapes=[pltpu.VMEM((D_STATE, d_inner), F32)] * 2,
        compiler_params=_params(("arbitrary", "arbitrary")),
        name=f"ssd_scan_{layer}",
    )(xs, bc, dtf, xs, bc, dtb, alog, dskip, tril, triu, e, e3)


def _ssd_out_kernel(nblk, nctx_blk, nbatch, d,
                    yf_ref, yb_ref, z_ref, h_ref, tbl_ref, nw_ref, wo_ref, o_ref):
    y = (yf_ref[...].astype(F32) + yb_ref[...].astype(F32)) * _silu(z_ref[...].astype(F32))
    proj = _dot(_rms(y, nw_ref[0]).astype(BF16), wo_ref[0])
    for sb in range(TILE_BLKS):
        rows = slice(sb * BLK, (sb + 1) * BLK)
        row, _, _ = _block_row(nblk, nctx_blk, nbatch, pl.program_id(0) * TILE_BLKS + sb)
        gate = tbl_ref[0, pl.ds(row, 1), 2 * d:3 * d]
        o_ref[rows, :] = h_ref[rows, :] + gate * proj[rows, :]


def _ssd_out(yf, yb, z, h, tbl, layer, nw, wo, geom):
    nt, d = h.shape
    d_inner = z.shape[1]
    nblk, nctx_blk, nbatch = geom
    j = layer // 2
    kern = functools.partial(_ssd_out_kernel, nblk, nctx_blk, nbatch, d)
    tm = TILE_BLKS * BLK
    big = pl.BlockSpec((tm, d_inner), lambda i: (i, 0))
    return pl.pallas_call(
        kern,
        grid=(nt // tm,),
        in_specs=[
            big, big, big,
            pl.BlockSpec((tm, d), lambda i: (i, 0)),
            pl.BlockSpec((1, 8, N_MOD * d), lambda i: (layer, 0, 0)),
            pl.BlockSpec((1, 1, d_inner), lambda i: (j, 0, 0)),
            pl.BlockSpec((1, d_inner, d), lambda i: (j, 0, 0), pipeline_mode=pl.Buffered(1)),
        ],
        out_specs=pl.BlockSpec((tm, d), lambda i: (i, 0)),
        out_shape=jax.ShapeDtypeStruct((nt, d), F32),
        compiler_params=_params(("arbitrary",)),
        name=f"ssd_out_{layer}",
    )(yf, yb, z, h, tbl, nw, wo)


def _route_kernel(nblk, nctx_blk, nbatch, d,
                  h_ref, tbl_ref, nw_ref, rw_ref, sl_ref, info_ref, cnt_ref, carry_ref):
    @pl.when(pl.program_id(0) == 0)
    def _():
        carry_ref[...] = jnp.zeros_like(carry_ref)

    row, _, _ = _block_row(nblk, nctx_blk, nbatch)
    shift = tbl_ref[0, pl.ds(row, 1), 3 * d:4 * d]
    scale = tbl_ref[0, pl.ds(row, 1), 4 * d:5 * d]
    v = _rms(h_ref[...], nw_ref[0]) * (1.0 + scale) + shift
    lane = lax.broadcasted_iota(jnp.int32, (BLK, CHUNK), 1).astype(F32)
    logits = jnp.where(lane < N_EXPERTS, _dot_hi(v, rw_ref[0]), -jnp.inf)
    m1 = jnp.max(logits, axis=1, keepdims=True)
    i1 = jnp.min(jnp.where(logits == m1, lane, float(CHUNK)), axis=1, keepdims=True)
    rest = jnp.where(lane == i1, -jnp.inf, logits)
    m2 = jnp.max(rest, axis=1, keepdims=True)
    i2 = jnp.min(jnp.where(rest == m2, lane, float(CHUNK)), axis=1, keepdims=True)
    e2 = jnp.exp(m2 - m1)
    g1 = 1.0 / (1.0 + e2)
    g2 = e2 / (1.0 + e2)
    oh1 = (lane == i1)
    oh2 = (lane == i2)
    member = jnp.where(jnp.logical_or(oh1, oh2), 1.0, 0.0)
    before = carry_ref[...] + _dot(sl_ref[...], member.astype(BF16))
    r1 = jnp.sum(jnp.where(oh1, before, 0.0), axis=1, keepdims=True)
    r2 = jnp.sum(jnp.where(oh2, before, 0.0), axis=1, keepdims=True)
    total = carry_ref[...] + jnp.sum(member, axis=0, keepdims=True)
    carry_ref[...] = total
    cnt_ref[...] = jnp.broadcast_to(total, cnt_ref.shape)
    lane8 = lax.broadcasted_iota(jnp.int32, (BLK, 8), 1)
    info = jnp.where(lane8 == 0, i1,
           jnp.where(lane8 == 1, i2,
           jnp.where(lane8 == 2, r1,
           jnp.where(lane8 == 3, r2,
           jnp.where(lane8 == 4, g1,
           jnp.where(lane8 == 5, g2, 0.0))))))
    info_ref[...] = info


def _route(h, tbl, layer, nw, rw, strict_lower, geom):
    nt, d = h.shape
    nblk, nctx_blk, nbatch = geom
    j = layer // 2
    kern = functools.partial(_route_kernel, nblk, nctx_blk, nbatch, d)
    return pl.pallas_call(
        kern,
        grid=(nt // BLK,),
        in_specs=[
            pl.BlockSpec((BLK, d), lambda i: (i, 0)),
            pl.BlockSpec((1, 8, N_MOD * d), lambda i: (layer, 0, 0)),
            pl.BlockSpec((1, 1, d), lambda i: (layer, 0, 0)),
            pl.BlockSpec((1, d, CHUNK), lambda i: (j, 0, 0)),
            _resident(strict_lower.shape),
        ],
        out_specs=[
            pl.BlockSpec((BLK, 8), lambda i: (i, 0)),
            pl.BlockSpec((8, CHUNK), lambda i: (0, 0)),
        ],
        out_shape=[jax.ShapeDtypeStruct((nt, 8), F32), jax.ShapeDtypeStruct((8, CHUNK), F32)],
        scratch_shapes=[pltpu.VMEM((1, CHUNK), F32)],
        compiler_params=_params(("arbitrary",)),
        name=f"route_{layer}",
    )(h, tbl, nw, rw, strict_lower)


def _to_tiles(ref, x, rows):
    for k in range(SUB):
        ref[pl.ds(k, rows, stride=SUB), :] = x[:, k * 128:(k + 1) * 128]


def _from_tiles(ref, rows):
    return jnp.concatenate([ref[pl.ds(k, rows, stride=SUB), :] for k in range(SUB)], axis=1)


def _tile_rows(ref, p):
    return ref.at[pl.ds(pl.multiple_of(p * SUB, SUB), SUB), :]


def _dispatch_kernel(nblk, nctx_blk, nbatch, d,
                     ztile_ref, zvalid_ref, pos_ref, h_ref, tbl_ref, nw_ref, xs_ref, v_scr, z_scr, sem, zsem):
    @pl.when(pl.program_id(0) == 0)
    def _():
        z_scr[...] = jnp.zeros_like(z_scr)
        tile_rows = MOE_TILE * SUB
        for e in range(2 * N_EXPERTS):
            @pl.when(zvalid_ref[e] == 1)
            def _():
                first = pl.multiple_of(ztile_ref[e] * tile_rows, SUB)
                pltpu.make_async_copy(z_scr, xs_ref.at[pl.ds(first, tile_rows), :], zsem).start()
        for e in range(2 * N_EXPERTS):
            @pl.when(zvalid_ref[e] == 1)
            def _():
                pltpu.make_async_copy(z_scr, xs_ref.at[pl.ds(0, tile_rows), :], zsem).wait()

    step = pl.program_id(0)
    nsteps = pl.num_programs(0)
    slot = step % 2
    v_slot = v_scr.at[slot]

    def wait_block(s):
        def body(i, c):
            for _ in range(2 * DMA_UNROLL):
                pltpu.make_async_copy(v_scr.at[s, pl.ds(0, SUB), :], xs_ref.at[pl.ds(0, SUB), :], sem.at[s]).wait()
            return c
        lax.fori_loop(0, BLK // DMA_UNROLL, body, 0)

    @pl.when(step >= 2)
    def _():
        wait_block(slot)

    row, _, _ = _block_row(nblk, nctx_blk, nbatch)
    shift = tbl_ref[0, pl.ds(row, 1), 3 * d:4 * d]
    scale = tbl_ref[0, pl.ds(row, 1), 4 * d:5 * d]
    _to_tiles(v_slot, _rms(h_ref[...], nw_ref[0]) * (1.0 + scale) + shift, BLK)

    def start(i, c):
        for u in range(DMA_UNROLL):
            r = i * DMA_UNROLL + u
            for k in range(2):
                pltpu.make_async_copy(_tile_rows(v_slot, r), _tile_rows(xs_ref, pos_ref[0, 0, 2 * r + k]),
                                      sem.at[slot]).start(priority=k)
        return c

    lax.fori_loop(0, BLK // DMA_UNROLL, start, 0)

    @pl.when(step == nsteps - 1)
    def _():
        wait_block(slot)

        @pl.when(nsteps >= 2)
        def _():
            wait_block(1 - slot)


def _dispatch(h, tbl, layer, nw, pos, ztile, zvalid, nslots, geom):
    nt, d = h.shape
    nblk, nctx_blk, nbatch = geom
    kern = functools.partial(_dispatch_kernel, nblk, nctx_blk, nbatch, d)
    gs = pltpu.PrefetchScalarGridSpec(
        num_scalar_prefetch=2,
        grid=(nt // BLK,),
        in_specs=[
            pl.BlockSpec((1, 1, 2 * BLK), lambda i, zt, zv: (i, 0, 0), memory_space=pltpu.SMEM),
            pl.BlockSpec((BLK, d), lambda i, zt, zv: (i, 0)),
            pl.BlockSpec((1, 8, N_MOD * d), lambda i, zt, zv: (layer, 0, 0)),
            pl.BlockSpec((1, 1, d), lambda i, zt, zv: (layer, 0, 0)),
        ],
        out_specs=pl.BlockSpec(memory_space=pl.ANY),
        scratch_shapes=[pltpu.VMEM((2, BLK * SUB, 128), F32), pltpu.VMEM((MOE_TILE * SUB, 128), F32),
                        pltpu.SemaphoreType.DMA((2,)), pltpu.SemaphoreType.DMA(())],
    )
    return pl.pallas_call(
        kern,
        grid_spec=gs,
        out_shape=jax.ShapeDtypeStruct((nslots * SUB, 128), F32),
        compiler_params=_params(("arbitrary",)),
        name=f"dispatch_{layer}",
    )(ztile, zvalid, pos.reshape(nt // BLK, 1, 2 * BLK), h, tbl, nw)


def _expert_kernel(n_fchunk, te_ref, nu_ref, x_ref, w1_ref, w3_ref, w2_ref, o_ref):
    @pl.when(pl.program_id(0) >= nu_ref[0])
    def _():
        o_ref[...] = jnp.zeros_like(o_ref)

    @pl.when(pl.program_id(0) < nu_ref[0])
    def _():
        x = _from_tiles(x_ref, MOE_TILE).astype(BF16)
        acc = None
        for k in range(n_fchunk):
            sl = slice(k * MOE_FCHUNK, (k + 1) * MOE_FCHUNK)
            a = _dot(x, w1_ref[0, 0, :, sl])
            b = _dot(x, w3_ref[0, 0, :, sl])
            act = (_silu(a) * b).astype(BF16)
            part = _dot(act, w2_ref[0, 0, sl, :])
            acc = part if acc is None else acc + part
        _to_tiles(o_ref, acc, MOE_TILE)


def _experts(x_sorted, tile_expert, n_used, j, w1, w3, w2):
    d, dffe = w1.shape[-2:]
    n_tiles = x_sorted.shape[0] // (MOE_TILE * SUB)
    kern = functools.partial(_expert_kernel, dffe // MOE_FCHUNK)
    tile = lambda i, te, nu: (jnp.minimum(i, nu[0] - 1), 0)
    wspec = lambda shape: pl.BlockSpec((1, 1) + shape, lambda i, te, nu: (j, te[i], 0, 0))
    gs = pltpu.PrefetchScalarGridSpec(
        num_scalar_prefetch=2,
        grid=(n_tiles,),
        in_specs=[pl.BlockSpec((MOE_TILE * SUB, 128), tile), wspec((d, dffe)), wspec((d, dffe)), wspec((dffe, d))],
        out_specs=pl.BlockSpec((MOE_TILE * SUB, 128), lambda i, te, nu: (i, 0)),
    )
    return pl.pallas_call(
        kern,
        grid_spec=gs,
        out_shape=jax.ShapeDtypeStruct(x_sorted.shape, F32),
        compiler_params=_params(("arbitrary",)),
        name=f"experts_{j}",
    )(tile_expert, n_used, x_sorted, w1, w3, w2)


def _combine_kernel(row_of_step, d, final,
                    pos_ref, posn_ref, h_ref, info_ref, tbl_ref, y_ref, *rest):
    if final:
        fw_ref, o_ref, buf, sem = rest
    else:
        o_ref, buf, sem = rest
    step = pl.program_id(0)
    nsteps = pl.num_programs(0)
    slot = step % 2

    def gather(p_ref, s):
        def body(i, c):
            for u in range(DMA_UNROLL):
                r = i * DMA_UNROLL + u
                for k in range(2):
                    pltpu.make_async_copy(_tile_rows(y_ref, p_ref[0, 0, 2 * r + k]), _tile_rows(buf.at[s, k], r),
                                          sem.at[s]).start(priority=k)
            return c
        lax.fori_loop(0, BLK // DMA_UNROLL, body, 0)

    @pl.when(step == 0)
    def _():
        gather(pos_ref, 0)

    @pl.when(step + 1 < nsteps)
    def _():
        gather(posn_ref, 1 - slot)

    def wait(i, c):
        for _ in range(2 * DMA_UNROLL):
            pltpu.make_async_copy(y_ref.at[pl.ds(0, SUB), :], buf.at[slot, 0, pl.ds(0, SUB), :], sem.at[slot]).wait()
        return c

    lax.fori_loop(0, BLK // DMA_UNROLL, wait, 0)
    info = info_ref[...]
    g1 = info[:, 4:5]
    g2 = info[:, 5:6]
    gate = tbl_ref[0, pl.ds(row_of_step(step), 1), 5 * d:6 * d]
    out = h_ref[...] + gate * (g1 * _from_tiles(buf.at[slot, 0], BLK) + g2 * _from_tiles(buf.at[slot, 1], BLK))
    if final:
        o_ref[0] = _rms(out, fw_ref[...])
    else:
        o_ref[...] = out


def _combine(h, info, tbl, layer, y_sorted, pos, geom, final_w=None):
    nt, d = h.shape
    nblk, nctx_blk, nbatch = geom
    final = final_w is not None
    pos3 = pos.reshape(nt // BLK, 1, 2 * BLK)
    if final:
        nlat = nblk - nctx_blk
        nsteps = nbatch * nlat
        blk = lambda i: (i // nlat) * nblk + nctx_blk + i % nlat
        row_of_step = lambda step: step // nlat
        out_spec = pl.BlockSpec((1, BLK, d), lambda i: (i // nlat, i % nlat, 0))
        out_shape = jax.ShapeDtypeStruct((nbatch, nlat * BLK, d), F32)
    else:
        nsteps = nt // BLK
        blk = lambda i: i
        row_of_step = lambda step: _block_row(nblk, nctx_blk, nbatch, step)[0]
        out_spec = pl.BlockSpec((BLK, d), lambda i: (i, 0))
        out_shape = jax.ShapeDtypeStruct((nt, d), F32)
    nxt = lambda i: blk(jnp.minimum(i + 1, nsteps - 1))
    in_specs = [
        pl.BlockSpec((1, 1, 2 * BLK), lambda i: (blk(i), 0, 0), memory_space=pltpu.SMEM),
        pl.BlockSpec((1, 1, 2 * BLK), lambda i: (nxt(i), 0, 0), memory_space=pltpu.SMEM),
        pl.BlockSpec((BLK, d), lambda i: (blk(i), 0)),
        pl.BlockSpec((BLK, 8), lambda i: (blk(i), 0)),
        pl.BlockSpec((1, 8, N_MOD * d), lambda i: (layer, 0, 0)),
        pl.BlockSpec(memory_space=pl.ANY),
    ]
    args = [pos3, pos3, h, info, tbl, y_sorted]
    if final:
        in_specs.append(pl.BlockSpec((1, d), lambda i: (0, 0)))
        args.append(final_w.reshape(1, d))
    return pl.pallas_call(
        functools.partial(_combine_kernel, row_of_step, d, final),
        grid=(nsteps,),
        in_specs=in_specs,
        out_specs=out_spec,
        out_shape=out_shape,
        scratch_shapes=[pltpu.VMEM((2, 2, BLK * SUB, 128), F32), pltpu.SemaphoreType.DMA((2,))],
        compiler_params=_params(("arbitrary",)),
        name=f"combine_{layer}",
    )(*args)


def _moe_layer(h, tbl, layer, nw, rw, strict_lower, w1, w3, w2, geom, final_w=None):
    nt, d = h.shape
    j = layer // 2
    info, counts = _route(h, tbl, layer, nw, rw, strict_lower, geom)
    cnt = counts[0, :N_EXPERTS].astype(jnp.int32)
    tiles = (cnt + MOE_TILE - 1) // MOE_TILE
    tile_end = jnp.cumsum(tiles)
    offs = (tile_end - tiles) * MOE_TILE
    n_used = tile_end[-1]
    idx = info[:, 0:2].astype(jnp.int32)
    pos = (offs[idx] + info[:, 2:4].astype(jnp.int32)).reshape(-1)
    n_tiles = (2 * nt) // MOE_TILE + N_EXPERTS
    t = jnp.minimum(jnp.arange(n_tiles, dtype=jnp.int32), n_used - 1)
    tile_expert = jnp.sum(t[:, None] >= tile_end[None, :], axis=1).astype(jnp.int32)
    spare = n_used + jnp.arange(N_EXPERTS, dtype=jnp.int32)
    ztile = jnp.concatenate([tile_end - 1, spare]).astype(jnp.int32)
    zvalid = jnp.concatenate([tiles > 0, spare < n_tiles]).astype(jnp.int32)
    x_sorted = _dispatch(h, tbl, layer, nw, pos, ztile, zvalid, n_tiles * MOE_TILE, geom)
    y_sorted = _experts(x_sorted, tile_expert, n_used.reshape(1).astype(jnp.int32), j, w1, w3, w2)
    return _combine(h, info, tbl, layer, y_sorted, pos, geom, final_w)


def kernel(x, c, ctx, c_ctx, ada_w, ada_b, norm_mix_w, norm_ffn_w, pool_w, pool_scale, ssd_in_w, ssd_conv_w, ssd_conv_b, ssd_A_log, ssd_dt_bias, ssd_D, ssd_norm_w, ssd_out_w, ffn_w1, ffn_w3, ffn_w2, moe_router_w, moe_w1, moe_w3, moe_w2, final_norm_w):
    nbatch, seq, d = x.shape
    ctx_len = ctx.shape[1]
    depth = ada_w.shape[0]
    d_inner = ssd_norm_w.shape[-1]
    assert ctx_len % BLK == 0 and seq % BLK == 0 and nbatch < 8
    assert d_inner == N_HEADS * HEAD_DIM and d % len(POOL_WINDOWS) == 0
    nblk = (ctx_len + seq) // BLK
    nctx_blk = ctx_len // BLK
    geom = (nblk, nctx_blk, nbatch)
    nchunk = (ctx_len + seq) // CHUNK
    ncc = ctx_len // CHUNK

    h = jnp.concatenate([ctx, x], axis=1).reshape(nbatch * (ctx_len + seq), d)
    cvec = jnp.zeros((8, d), F32).at[:nbatch].set(c).at[nbatch].set(c_ctx)
    tbl = _mod_table(cvec, ada_w, ada_b)

    vec3 = lambda a: a.reshape(a.shape[0], 1, a.shape[-1])
    nmix = vec3(norm_mix_w)
    nffn = vec3(norm_ffn_w)
    ahi, alo = _pool_matrices()
    scan_consts = _scan_constants()
    strict_lower = jnp.asarray(np.tril(np.ones((BLK, BLK), np.float32), -1), BF16)

    conv_dim = ssd_conv_w.shape[-1]
    wz = ssd_in_w[:, :, :d_inner].astype(BF16)
    wx = ssd_in_w[:, :, d_inner:d_inner + conv_dim].astype(BF16)
    wdt = ssd_in_w[:, :, d_inner + conv_dim:]
    pad_dt = lambda w: jnp.pad(w, ((0, 0), (0, 0), (0, CHUNK - N_HEADS))).astype(BF16)
    wdf = pad_dt(wdt[:, :, :N_HEADS])
    wdb = pad_dt(wdt[:, :, N_HEADS:])
    pad_h = lambda a: jnp.pad(a, ((0, 0), (0, 0), (0, CHUNK - N_HEADS)))
    dtb = pad_h(ssd_dt_bias)
    alog = pad_h(ssd_A_log)
    dskip = vec3(jnp.repeat(ssd_D, HEAD_DIM, axis=-1))
    rw = jnp.pad(moe_router_w, ((0, 0), (0, 0), (0, CHUNK - N_EXPERTS)))
    pool_wb = pool_w.astype(BF16)
    ffn = [w.astype(BF16) for w in (ffn_w1, ffn_w3, ffn_w2)]
    moe = [w.astype(BF16) for w in (moe_w1, moe_w3, moe_w2)]
    wout = ssd_out_w.astype(BF16)

    for i in range(depth):
        if i % 2 == 0:
            h = _even_layer(h, tbl, i, nmix, nffn, ahi, alo, pool_wb, vec3(pool_scale), *ffn, geom)
        else:
            z, xs, bc, dtf, dtbw = _ssd_in(h, tbl, i, nmix, wz, wx, wdf, wdb, ssd_conv_w,
                                           vec3(ssd_conv_b), dtb, geom)
            yf, yb = _ssd_scan(xs, bc, dtf, dtbw, alog, dskip, i, scan_consts, nbatch, nchunk, ncc)
            h = _ssd_out(yf, yb, z, h, tbl, i, vec3(ssd_norm_w), wout, geom)
            fin = final_norm_w if (i == depth - 1) else None
            h = _moe_layer(h, tbl, i, nffn, rw, strict_lower, *moe, geom, fin)
    if depth % 2 == 1:
        h = _final_norm(h, final_norm_w, nbatch, nblk, nctx_blk)
    return h
```

```python
import functools

import numpy as np
import jax
import jax.numpy as jnp
from jax import lax
from jax.experimental import pallas as pl
from jax.experimental.pallas import tpu as pltpu

F32 = jnp.float32
BF16 = jnp.bfloat16
EPS = 1e-6

BLK = 256
TILE_BLKS = 2
CHUNK = 128
SCAN_CHUNKS = 2
GRID_W = 64
POOL_WINDOWS = (2, 4, 8, 16)
N_MOD = 6
HEAD_DIM = 64
N_HEADS = 32
N_GROUPS = 4
D_STATE = 128
D_CONV = 4
CONV_LEFT = 2
HALO = 8
CONV_SLABS = 4
N_EXPERTS = 8
MOE_TILE = 512
MOE_FCHUNK = 512
FFN_FCHUNK = 512
VMEM_LIMIT = 56 * 2**20
LOG2E = 1.4426950408889634
SUB = 8
DMA_UNROLL = 8


def _dot(a, b):
    return jnp.dot(a, b, preferred_element_type=F32)


def _split2(x):
    hi = x.astype(BF16)
    lo = (x - hi.astype(F32)).astype(BF16)
    return hi, lo


def _split3(x):
    p0 = x.astype(BF16)
    r = x - p0.astype(F32)
    p1 = r.astype(BF16)
    p2 = (r - p1.astype(F32)).astype(BF16)
    return p0, p1, p2


def _dot_hi(a, b):
    ah, al = _split2(a)
    bh, bl = _split2(b)
    return _dot(ah, bh) + _dot(al, bh) + _dot(ah, bl)


def _sigmoid(x):
    return 1.0 / (1.0 + jnp.exp(-x))


def _silu(x):
    return x * _sigmoid(x)


def _rms(x, w):
    ms = jnp.mean(x * x, axis=-1, keepdims=True)
    return x * lax.rsqrt(ms + EPS) * w


def _params(sem):
    return pltpu.CompilerParams(dimension_semantics=sem, vmem_limit_bytes=VMEM_LIMIT)


def _resident(shape):
    nd = len(shape)
    return pl.BlockSpec(shape, lambda *_: (0,) * nd, pipeline_mode=pl.Buffered(1))


def _block_row(nblk, nctx_blk, nbatch, blk=None):
    if blk is None:
        blk = pl.program_id(0)
    b = blk // nblk
    j = blk - b * nblk
    is_ctx = j < nctx_blk
    return jnp.where(is_ctx, nbatch, b), is_ctx, j


def _mod_kernel(c_ref, w_ref, b_ref, o_ref):
    o_ref[0] = _dot_hi(_silu(c_ref[...]), w_ref[0]) + b_ref[0]


def _mod_table(cvec, ada_w, ada_b):
    depth, d, n = ada_w.shape
    tn = 512
    return pl.pallas_call(
        _mod_kernel,
        grid=(depth, n // tn),
        in_specs=[
            pl.BlockSpec((8, d), lambda l, j: (0, 0)),
            pl.BlockSpec((1, d, tn), lambda l, j: (l, 0, j)),
            pl.BlockSpec((1, 1, tn), lambda l, j: (l, 0, j)),
        ],
        out_specs=pl.BlockSpec((1, 8, tn), lambda l, j: (l, 0, j)),
        out_shape=jax.ShapeDtypeStruct((depth, 8, n), F32),
        compiler_params=_params(("arbitrary", "arbitrary")),
        name="mod_table",
    )(cvec, ada_w, ada_b.reshape(depth, 1, n))


def _even_kernel(nblk, nctx_blk, nbatch, d, from_inputs, *refs):
    n_h = 2 * TILE_BLKS if from_inputs else 1
    h_refs = refs[:n_h]
    tbl_ref, nw1_ref, nw2_ref, acat_ref, pw_ref, ps_ref, w1_ref, w3_ref, w2_ref, o_ref = refs[n_h:]

    def mods(sb):
        row, is_ctx, _ = _block_row(nblk, nctx_blk, nbatch, pl.program_id(0) * TILE_BLKS + sb)
        return (lambda k: tbl_ref[0, pl.ds(row, 1), k * d:(k + 1) * d]), is_ctx

    def mixer(sb):
        mod, is_ctx = mods(sb)
        kind = is_ctx.astype(jnp.int32)
        if from_inputs:
            h = jnp.where(is_ctx, h_refs[2 * sb][0], h_refs[2 * sb + 1][0])
        else:
            h = h_refs[0][sb * BLK:(sb + 1) * BLK, :]
        u = _rms(h, nw1_ref[0]) * (1.0 + mod(1)) + mod(0)
        gw = d // len(POOL_WINDOWS)
        ys = []
        for g in range(len(POOL_WINDOWS)):
            ug = u[:, g * gw:(g + 1) * gw]
            uh, ul = _split2(ug)
            p = _dot(acat_ref[kind, g], jnp.concatenate([uh, uh, ul], axis=0)) - ug
            ys.append(_dot(p.astype(BF16), pw_ref[0, g]))
        y = jnp.concatenate(ys, axis=1) * ps_ref[0]
        h1 = h + mod(2) * y
        v = (_rms(h1, nw2_ref[0]) * (1.0 + mod(4)) + mod(3)).astype(BF16)
        return h1, v

    dff = w1_ref.shape[-1]
    chunks = [slice(c, min(c + FFN_FCHUNK, dff)) for c in range(0, dff, FFN_FCHUNK)]
    n_f = len(chunks)

    def up(v, k):
        return _dot(v, w1_ref[0, :, chunks[k]]), _dot(v, w3_ref[0, :, chunks[k]])

    def down(ab, k):
        act = (_silu(ab[0]) * ab[1]).astype(BF16)
        return _dot(act, w2_ref[0, chunks[k], :])

    mixed = [mixer(sb) for sb in range(TILE_BLKS)]
    v = jnp.concatenate([m[1] for m in mixed], axis=0)
    ab = up(v, 0)
    acc = None
    for k in range(n_f):
        ab_next = up(v, k + 1) if k + 1 < n_f else None
        part = down(ab, k)
        acc = part if acc is None else acc + part
        ab = ab_next
    for sb in range(TILE_BLKS):
        mod, _ = mods(sb)
        o_ref[sb * BLK:(sb + 1) * BLK, :] = mixed[sb][0] + mod(5) * acc[sb * BLK:(sb + 1) * BLK, :]


def _pool_matrices():
    mats = np.zeros((2, len(POOL_WINDOWS), BLK, BLK), np.float64)
    for kind, seg in enumerate((GRID_W, BLK)):
        for g, w in enumerate(POOL_WINDOWS):
            lo = w // 2
            hi = w - 1 - lo
            for t in range(BLK):
                base = (t // seg) * seg
                tt = t - base
                start = max(tt - lo, 0)
                end = min(tt + hi + 1, seg)
                mats[kind, g, t, base + start:base + end] = 1.0 / (end - start)
    m32 = jnp.asarray(mats, F32)
    hi = m32.astype(BF16)
    lo = (m32 - hi.astype(F32)).astype(BF16)
    return jnp.concatenate([hi, lo, hi], axis=-1)


def _even_layer(h, tbl, layer, nw1, nw2, acat, pw, ps, w1, w3, w2, geom):
    nblk, nctx_blk, nbatch = geom
    from_inputs = isinstance(h, tuple)
    d = h[0].shape[-1]
    nt = nbatch * nblk * BLK
    dff = w1.shape[-1]
    j = layer // 2
    gw = d // len(POOL_WINDOWS)
    kern = functools.partial(_even_kernel, nblk, nctx_blk, nbatch, d, from_inputs)
    vec = lambda idx: pl.BlockSpec((1, 1, d), lambda i: (idx, 0, 0))
    tm = TILE_BLKS * BLK
    if from_inputs:
        def src_spec(sb, latent):
            def index(i):
                g = i * TILE_BLKS + sb
                b = g // nblk
                jb = g - b * nblk
                blk = jnp.maximum(jb - nctx_blk, 0) if latent else jnp.minimum(jb, nctx_blk - 1)
                return (b, blk, 0)
            return pl.BlockSpec((1, BLK, d), index)
        h_specs = [src_spec(sb, latent) for sb in range(TILE_BLKS) for latent in (False, True)]
        h_args = [h[0], h[1]] * TILE_BLKS
    else:
        h_specs = [pl.BlockSpec((tm, d), lambda i: (i, 0))]
        h_args = [h]
    return pl.pallas_call(
        kern,
        grid=(nt // tm,),
        in_specs=h_specs + [
            pl.BlockSpec((1, 8, N_MOD * d), lambda i: (layer, 0, 0)),
            vec(layer), vec(layer),
            _resident(acat.shape),
            pl.BlockSpec((1, len(POOL_WINDOWS), gw, gw), lambda i: (j, 0, 0, 0)),
            vec(j),
            pl.BlockSpec((1, d, dff), lambda i: (j, 0, 0), pipeline_mode=pl.Buffered(1)),
            pl.BlockSpec((1, d, dff), lambda i: (j, 0, 0), pipeline_mode=pl.Buffered(1)),
            pl.BlockSpec((1, dff, d), lambda i: (j, 0, 0), pipeline_mode=pl.Buffered(1)),
        ],
        out_specs=pl.BlockSpec((tm, d), lambda i: (i, 0)),
        out_shape=jax.ShapeDtypeStruct((nt, d), F32),
        compiler_params=_params(("arbitrary",)),
        name=f"pool_ffn_{layer}",
    )(*h_args, tbl, nw1, nw2, acat, pw, ps, w1, w3, w2)


def _ssd_in_kernel(nblk, nctx_blk, nbatch, d, d_inner,
                   hp_ref, h_ref, hn_ref, tbl_ref, nw_ref, wz_ref, wx_ref, wdf_ref, wdb_ref,
                   cw_ref, cb_ref, dtb_ref,
                   z_ref, xs_ref, bc_ref, dtf_ref, dtb_out_ref, *xbc_scrs):
    for sb in range(TILE_BLKS):
        rows = slice(sb * BLK, (sb + 1) * BLK)
        prev_rows = hp_ref[...] if sb == 0 else h_ref[sb * BLK - HALO:sb * BLK, :]
        next_rows = hn_ref[...] if sb == TILE_BLKS - 1 else h_ref[(sb + 1) * BLK:(sb + 1) * BLK + HALO, :]
        _ssd_in_block(nblk, nctx_blk, nbatch, d, d_inner, pl.program_id(0) * TILE_BLKS + sb,
                      prev_rows, h_ref[rows, :], next_rows, tbl_ref, nw_ref, wz_ref, wx_ref, wdf_ref, wdb_ref,
                      cw_ref, cb_ref, dtb_ref, z_ref, xs_ref, bc_ref, dtf_ref, dtb_out_ref, xbc_scrs[sb], rows)


def _ssd_in_block(nblk, nctx_blk, nbatch, d, d_inner, blk, h_prev, h_blk, h_next,
                  tbl_ref, nw_ref, wz_ref, wx_ref, wdf_ref, wdb_ref, cw_ref, cb_ref, dtb_ref,
                  z_ref, xs_ref, bc_ref, dtf_ref, dtb_out_ref, xbc_scr, rows):
    row, _, j = _block_row(nblk, nctx_blk, nbatch, blk)
    first = jnp.logical_or(j == 0, j == nctx_blk)
    last = jnp.logical_or(j == nctx_blk - 1, j == nblk - 1)
    shift = tbl_ref[0, pl.ds(row, 1), 0:d]
    scale = tbl_ref[0, pl.ds(row, 1), d:2 * d]
    nw = nw_ref[0]

    def modn(x):
        return _rms(x, nw) * (1.0 + scale) + shift

    uf = modn(h_blk)
    u = uf.astype(BF16)
    up = modn(h_prev) * jnp.where(first, 0.0, 1.0)
    un = modn(h_next) * jnp.where(last, 0.0, 1.0)
    u_ext = jnp.concatenate([up, uf, un], axis=0).astype(BF16)
    nslab = xbc_scr.shape[0]
    xs_slabs = d_inner // 128
    per = CONV_SLABS

    def project(c0):
        val = _dot(u_ext, wx_ref[0, :, c0 * 128:(c0 + per) * 128])
        for c in range(c0, c0 + per):
            xbc_scr[c, :, :] = val[:, (c - c0) * 128:(c - c0 + 1) * 128]

    def conv(c0):
        for c in range(c0, c0 + per):
            lanes = slice(c * 128, (c + 1) * 128)
            acc = cb_ref[0, :, lanes]
            for k in range(D_CONV):
                off = HALO - CONV_LEFT + k
                acc = acc + xbc_scr[c, off:off + BLK, :] * cw_ref[0, k:k + 1, lanes]
            y = _silu(acc).astype(BF16)
            if c < xs_slabs:
                xs_ref[rows, lanes] = y
            else:
                bc_ref[rows, (c - xs_slabs) * 128:(c - xs_slabs + 1) * 128] = y

    def softplus(x):
        return jnp.maximum(x, 0.0) + jnp.log1p(jnp.exp(-jnp.abs(x)))

    zw = per * 128
    z_pieces = d_inner // zw
    project(0)
    for i, c0 in enumerate(range(0, nslab, per)):
        if c0 + per < nslab:
            project(c0 + per)
        if i < z_pieces:
            z_ref[rows, i * zw:(i + 1) * zw] = _dot(u, wz_ref[0, :, i * zw:(i + 1) * zw]).astype(BF16)
        if i == z_pieces:
            dtf_ref[rows, :] = softplus(_dot(u, wdf_ref[0]) + dtb_ref[0, 0:1, :])
            dtb_out_ref[rows, :] = softplus(_dot(u, wdb_ref[0]) + dtb_ref[0, 1:2, :])
        conv(c0)


def _ssd_in(h, tbl, layer, nw, wz, wx, wdf, wdb, cw, cb, dtb, geom):
    nt, d = h.shape
    nblk, nctx_blk, nbatch = geom
    j = layer // 2
    d_inner = wz.shape[-1]
    conv_dim = wx.shape[-1]
    tm = TILE_BLKS * BLK
    hb = tm // HALO
    nh = nt // HALO
    kern = functools.partial(_ssd_in_kernel, nblk, nctx_blk, nbatch, d, d_inner)
    res3 = lambda a: pl.BlockSpec((1,) + a.shape[1:], lambda i: (j, 0, 0), pipeline_mode=pl.Buffered(1))
    return pl.pallas_call(
        kern,
        grid=(nt // tm,),
        in_specs=[
            pl.BlockSpec((HALO, d), lambda i: (jnp.maximum(i * hb - 1, 0), 0)),
            pl.BlockSpec((tm, d), lambda i: (i, 0)),
            pl.BlockSpec((HALO, d), lambda i: (jnp.minimum((i + 1) * hb, nh - 1), 0)),
            pl.BlockSpec((1, 8, N_MOD * d), lambda i: (layer, 0, 0)),
            pl.BlockSpec((1, 1, d), lambda i: (layer, 0, 0)),
            res3(wz), res3(wx), res3(wdf), res3(wdb),
            pl.BlockSpec((1, D_CONV, conv_dim), lambda i: (j, 0, 0)),
            pl.BlockSpec((1, 1, conv_dim), lambda i: (j, 0, 0)),
            pl.BlockSpec((1, 2, CHUNK), lambda i: (j, 0, 0)),
        ],
        out_specs=[
            pl.BlockSpec((tm, d_inner), lambda i: (i, 0)),
            pl.BlockSpec((tm, d_inner), lambda i: (i, 0)),
            pl.BlockSpec((tm, conv_dim - d_inner), lambda i: (i, 0)),
            pl.BlockSpec((tm, CHUNK), lambda i: (i, 0)),
            pl.BlockSpec((tm, CHUNK), lambda i: (i, 0)),
        ],
        out_shape=[
            jax.ShapeDtypeStruct((nt, d_inner), BF16),
            jax.ShapeDtypeStruct((nt, d_inner), BF16),
            jax.ShapeDtypeStruct((nt, conv_dim - d_inner), BF16),
            jax.ShapeDtypeStruct((nt, CHUNK), F32),
            jax.ShapeDtypeStruct((nt, CHUNK), F32),
        ],
        scratch_shapes=[pltpu.VMEM((conv_dim // 128, BLK + 2 * HALO, 128), F32)] * TILE_BLKS,
        compiler_params=_params(("arbitrary",)),
        name=f"ssd_in_{layer}",
    )(h, h, h, tbl, nw, wz, wx, wdf, wdb, cw, cb, dtb)


def _scan_prep(dt, alog, tri, reverse):
    t = dt.shape[0]
    lane = lax.broadcasted_iota(jnp.int32, (1, CHUNK), 1)
    a_row = jnp.where(lane < N_HEADS, -jnp.exp(alog), 0.0)
    a = dt * (a_row * LOG2E)
    p0, p1, p2 = _split3(a)
    cum = _dot(tri, p0) + _dot(tri, p1) + _dot(tri, p2)
    tot = cum[0:1, :] if reverse else cum[t - 1:t, :]
    dte = jnp.exp2(tot - cum)
    ecum = jnp.exp2(cum)
    cdec = jnp.exp2(tot)
    row_t = (cum - jnp.log2(dt)).T

    q0, q1, q2 = _split3(cum)
    stacked = (q0.astype(F32) + pltpu.roll(q1.astype(F32), N_HEADS, 1)
               + pltpu.roll(q2.astype(F32), 2 * N_HEADS, 1)).astype(BF16)
    li = lax.broadcasted_iota(jnp.int32, (t, t), 0)
    si = lax.broadcasted_iota(jnp.int32, (t, t), 1)
    keep = (si >= li) if reverse else (si <= li)
    return dict(wdt=(dt * dte).astype(BF16), ecum=ecum.astype(BF16),
                cdec=_split3(jnp.broadcast_to(cdec, (8, CHUNK))), stacked=stacked, row_t=row_t, keep=keep)


def _scan_expand(g, preps, e_ref, e3_ref):
    heads = N_HEADS // N_GROUPS
    e = e_ref[:, g * heads * HEAD_DIM:(g + 1) * heads * HEAD_DIM]
    stack = lambda key: jnp.concatenate([p[key] for p in preps], axis=0)
    w_x = _dot(stack("wdt"), e).astype(BF16)
    ec_x = _dot(stack("ecum"), e)
    cd_x = sum(_dot(jnp.concatenate([p["cdec"][i] for p in preps], axis=0), e) for i in range(3))
    colb = _dot(stack("stacked"), e3_ref[:, g * heads * CHUNK:(g + 1) * heads * CHUNK])
    rows = lambda a, n: a[n * CHUNK:(n + 1) * CHUNK, :]
    return [dict(w_x=rows(w_x, n), ec_x=rows(ec_x, n), cd_x=cd_x[8 * n:8 * n + 1, :], colb=rows(colb, n))
            for n in range(len(preps))]


def _scan_cb(g, bc_ref, rows):
    gn = N_GROUPS * D_STATE
    b_g = bc_ref[rows, g * D_STATE:(g + 1) * D_STATE]
    c_g = bc_ref[rows, gn + g * D_STATE:gn + (g + 1) * D_STATE]
    return lax.dot_general(c_g, b_g, (((1,), (1,)), ((), ())), preferred_element_type=F32)


def _scan_group(g, prep, ex, cb, x_ref, bc_ref, s_ref, y_ref, dskip, rows):
    t = CHUNK
    heads = N_HEADS // N_GROUPS
    gn = N_GROUPS * D_STATE
    gp = heads * HEAD_DIM
    first_head = lax.broadcasted_iota(jnp.int32, (t, 2 * HEAD_DIM), 1) < HEAD_DIM
    b_g = bc_ref[rows, g * D_STATE:(g + 1) * D_STATE]
    c_g = bc_ref[rows, gn + g * D_STATE:gn + (g + 1) * D_STATE]
    def decay_pair(q):
        ms = []
        for hl in (2 * q, 2 * q + 1):
            hh = g * heads + hl
            seg = ex["colb"][:, hl * CHUNK:(hl + 1) * CHUNK] - prep["row_t"][hh:hh + 1, :]
            lmat = jnp.exp2(jnp.where(prep["keep"], seg, -1e30))
            ms.append((cb * lmat).astype(BF16))
        return jnp.concatenate(ms, axis=1)

    ydiag = []
    m_pair = decay_pair(0)
    for q in range(heads // 2):
        m_next = decay_pair(q + 1) if q + 1 < heads // 2 else None
        x_pair = x_ref[rows, (g * heads + 2 * q) * HEAD_DIM:(g * heads + 2 * q + 2) * HEAD_DIM]
        zero = jnp.zeros_like(x_pair)
        rhs = jnp.concatenate([jnp.where(first_head, x_pair, zero),
                               jnp.where(first_head, zero, x_pair)], axis=0)
        ydiag.append(_dot(m_pair, rhs))
        m_pair = m_next
    sl = slice(g * gp, (g + 1) * gp)
    xg = x_ref[rows, sl]
    s_old = s_ref[:, sl]
    y_g = jnp.concatenate(ydiag, axis=1) + _dot(c_g, s_old.astype(BF16)) * ex["ec_x"]
    if dskip is not None:
        y_g = y_g + xg.astype(F32) * dskip[:, sl]
    y_ref[rows, sl] = y_g.astype(y_ref.dtype)
    s_new = lax.dot_general(b_g, xg * ex["w_x"], (((0,), (0,)), ((), ())), preferred_element_type=F32)
    s_ref[:, sl] = s_old * ex["cd_x"] + s_new


def _ssd_scan_kernel(xf_ref, bcf_ref, dtf_ref, xb_ref, bcb_ref, dtb_ref, alog_ref, dsk_ref,
                     tril_ref, triu_ref, e_ref, e3_ref, yf_ref, yb_ref, sf_ref, sb_ref):
    @pl.when(pl.program_id(1) == 0)
    def _():
        sf_ref[...] = jnp.zeros_like(sf_ref)
        sb_ref[...] = jnp.zeros_like(sb_ref)

    rows_f = [slice(n * CHUNK, (n + 1) * CHUNK) for n in range(SCAN_CHUNKS)]
    rows_b = rows_f[::-1]
    pf = [_scan_prep(dtf_ref[r, :], alog_ref[0, 0:1, :], tril_ref[...], False) for r in rows_f]
    pb = [_scan_prep(dtb_ref[r, :], alog_ref[0, 1:2, :], triu_ref[...], True) for r in rows_b]
    cbf = [[_scan_cb(g, bcf_ref, r) for r in rows_f] for g in range(N_GROUPS)]
    cbb = [[_scan_cb(g, bcb_ref, r) for r in rows_b] for g in range(N_GROUPS)]
    exf = _scan_expand(0, pf, e_ref, e3_ref)
    exb = _scan_expand(0, pb, e_ref, e3_ref)
    for g in range(N_GROUPS):
        nxf = _scan_expand(g + 1, pf, e_ref, e3_ref) if g + 1 < N_GROUPS else None
        nxb = _scan_expand(g + 1, pb, e_ref, e3_ref) if g + 1 < N_GROUPS else None
        for n in range(SCAN_CHUNKS):
            _scan_group(g, pf[n], exf[n], cbf[g][n], xf_ref, bcf_ref, sf_ref, yf_ref, dsk_ref[0], rows_f[n])
            _scan_group(g, pb[n], exb[n], cbb[g][n], xb_ref, bcb_ref, sb_ref, yb_ref, None, rows_b[n])
        exf, exb = nxf, nxb


def _scan_constants():
    li = np.arange(CHUNK)[:, None]
    ti = np.arange(CHUNK)[None, :]
    tril = (ti <= li).astype(np.float32)
    triu = (ti >= li).astype(np.float32)
    e = np.zeros((CHUNK, N_HEADS * HEAD_DIM), np.float32)
    e3 = np.zeros((CHUNK, N_HEADS * CHUNK), np.float32)
    for h in range(N_HEADS):
        e[h, h * HEAD_DIM:(h + 1) * HEAD_DIM] = 1.0
        for piece in range(3):
            e3[piece * N_HEADS + h, h * CHUNK:(h + 1) * CHUNK] = 1.0
    return tuple(jnp.asarray(m, BF16) for m in (tril, triu, e, e3))


def _ssd_scan(xs, bc, dtf, dtb, alog, dskip, layer, consts, nbatch, nchunk, ncc):
    nt, d_inner = xs.shape
    bcw = bc.shape[1]
    j = layer // 2
    tril, triu, e, e3 = consts

    assert nchunk % SCAN_CHUNKS == 0 and ncc % SCAN_CHUNKS == 0
    nstep = nchunk // SCAN_CHUNKS
    ncs = ncc // SCAN_CHUNKS
    rows = SCAN_CHUNKS * CHUNK

    def fwd(b, c):
        return (b * nstep + c, 0)

    def bwd(b, c):
        return (b * nstep + jnp.where(c < ncs, ncs - 1 - c, nstep - 1 - (c - ncs)), 0)

    return pl.pallas_call(
        _ssd_scan_kernel,
        grid=(nbatch, nstep),
        in_specs=[
            pl.BlockSpec((rows, d_inner), fwd), pl.BlockSpec((rows, bcw), fwd), pl.BlockSpec((rows, CHUNK), fwd),
            pl.BlockSpec((rows, d_inner), bwd), pl.BlockSpec((rows, bcw), bwd), pl.BlockSpec((rows, CHUNK), bwd),
            pl.BlockSpec((1, 2, CHUNK), lambda b, c: (j, 0, 0)),
            pl.BlockSpec((1, 1, d_inner), lambda b, c: (j, 0, 0)),
            _resident(tril.shape), _resident(triu.shape), _resident(e.shape), _resident(e3.shape),
        ],
        out_specs=[pl.BlockSpec((rows, d_inner), fwd), pl.BlockSpec((rows, d_inner), bwd)],
        out_shape=[jax.ShapeDtypeStruct((nt, d_inner), BF16)] * 2,
        scratch_shapes=[pltpu.VMEM((D_STATE, d_inner), F32)] * 2,
        compiler_params=_params(("arbitrary", "arbitrary")),
        name=f"ssd_scan_{layer}",
    )(xs, bc, dtf, xs, bc, dtb, alog, dskip, tril, triu, e, e3)


def _ssd_out_kernel(nblk, nctx_blk, nbatch, d,
                    yf_ref, yb_ref, z_ref, h_ref, tbl_ref, nw_ref, wo_ref, o_ref):
    y = (yf_ref[...].astype(F32) + yb_ref[...].astype(F32)) * _silu(z_ref[...].astype(F32))
    proj = _dot(_rms(y, nw_ref[0]).astype(BF16), wo_ref[0])
    for sb in range(TILE_BLKS):
        rows = slice(sb * BLK, (sb + 1) * BLK)
        row, _, _ = _block_row(nblk, nctx_blk, nbatch, pl.program_id(0) * TILE_BLKS + sb)
        gate = tbl_ref[0, pl.ds(row, 1), 2 * d:3 * d]
        o_ref[rows, :] = h_ref[rows, :] + gate * proj[rows, :]


def _ssd_out(yf, yb, z, h, tbl, layer, nw, wo, geom):
    nt, d = h.shape
    d_inner = z.shape[1]
    nblk, nctx_blk, nbatch = geom
    j = layer // 2
    kern = functools.partial(_ssd_out_kernel, nblk, nctx_blk, nbatch, d)
    tm = TILE_BLKS * BLK
    big = pl.BlockSpec((tm, d_inner), lambda i: (i, 0))
    return pl.pallas_call(
        kern,
        grid=(nt // tm,),
        in_specs=[
            big, big, big,
            pl.BlockSpec((tm, d), lambda i: (i, 0)),
            pl.BlockSpec((1, 8, N_MOD * d), lambda i: (layer, 0, 0)),
            pl.BlockSpec((1, 1, d_inner), lambda i: (j, 0, 0)),
            pl.BlockSpec((1, d_inner, d), lambda i: (j, 0, 0), pipeline_mode=pl.Buffered(1)),
        ],
        out_specs=pl.BlockSpec((tm, d), lambda i: (i, 0)),
        out_shape=jax.ShapeDtypeStruct((nt, d), F32),
        compiler_params=_params(("arbitrary",)),
        name=f"ssd_out_{layer}",
    )(yf, yb, z, h, tbl, nw, wo)


def _route_kernel(nblk, nctx_blk, nbatch, d,
                  h_ref, tbl_ref, nw_ref, rw_ref, sl_ref, info_ref, cnt_ref, carry_ref):
    @pl.when(pl.program_id(0) == 0)
    def _():
        carry_ref[...] = jnp.zeros_like(carry_ref)

    row, _, _ = _block_row(nblk, nctx_blk, nbatch)
    shift = tbl_ref[0, pl.ds(row, 1), 3 * d:4 * d]
    scale = tbl_ref[0, pl.ds(row, 1), 4 * d:5 * d]
    v = _rms(h_ref[...], nw_ref[0]) * (1.0 + scale) + shift
    lane = lax.broadcasted_iota(jnp.int32, (BLK, CHUNK), 1).astype(F32)
    logits = jnp.where(lane < N_EXPERTS, _dot_hi(v, rw_ref[0]), -jnp.inf)
    m1 = jnp.max(logits, axis=1, keepdims=True)
    i1 = jnp.min(jnp.where(logits == m1, lane, float(CHUNK)), axis=1, keepdims=True)
    rest = jnp.where(lane == i1, -jnp.inf, logits)
    m2 = jnp.max(rest, axis=1, keepdims=True)
    i2 = jnp.min(jnp.where(rest == m2, lane, float(CHUNK)), axis=1, keepdims=True)
    e2 = jnp.exp(m2 - m1)
    g1 = 1.0 / (1.0 + e2)
    g2 = e2 / (1.0 + e2)
    oh1 = (lane == i1)
    oh2 = (lane == i2)
    member = jnp.where(jnp.logical_or(oh1, oh2), 1.0, 0.0)
    before = carry_ref[...] + _dot(sl_ref[...], member.astype(BF16))
    r1 = jnp.sum(jnp.where(oh1, before, 0.0), axis=1, keepdims=True)
    r2 = jnp.sum(jnp.where(oh2, before, 0.0), axis=1, keepdims=True)
    total = carry_ref[...] + jnp.sum(member, axis=0, keepdims=True)
    carry_ref[...] = total
    cnt_ref[...] = jnp.broadcast_to(total, cnt_ref.shape)
    lane8 = lax.broadcasted_iota(jnp.int32, (BLK, 8), 1)
    info = jnp.where(lane8 == 0, i1,
           jnp.where(lane8 == 1, i2,
           jnp.where(lane8 == 2, r1,
           jnp.where(lane8 == 3, r2,
           jnp.where(lane8 == 4, g1,
           jnp.where(lane8 == 5, g2, 0.0))))))
    info_ref[...] = info


def _route(h, tbl, layer, nw, rw, strict_lower, geom):
    nt, d = h.shape
    nblk, nctx_blk, nbatch = geom
    j = layer // 2
    kern = functools.partial(_route_kernel, nblk, nctx_blk, nbatch, d)
    return pl.pallas_call(
        kern,
        grid=(nt // BLK,),
        in_specs=[
            pl.BlockSpec((BLK, d), lambda i: (i, 0)),
            pl.BlockSpec((1, 8, N_MOD * d), lambda i: (layer, 0, 0)),
            pl.BlockSpec((1, 1, d), lambda i: (layer, 0, 0)),
            pl.BlockSpec((1, d, CHUNK), lambda i: (j, 0, 0)),
            _resident(strict_lower.shape),
        ],
        out_specs=[
            pl.BlockSpec((BLK, 8), lambda i: (i, 0)),
            pl.BlockSpec((8, CHUNK), lambda i: (0, 0)),
        ],
        out_shape=[jax.ShapeDtypeStruct((nt, 8), F32), jax.ShapeDtypeStruct((8, CHUNK), F32)],
        scratch_shapes=[pltpu.VMEM((1, CHUNK), F32)],
        compiler_params=_params(("arbitrary",)),
        name=f"route_{layer}",
    )(h, tbl, nw, rw, strict_lower)


def _to_tiles(ref, x, rows):
    for k in range(SUB):
        ref[pl.ds(k, rows, stride=SUB), :] = x[:, k * 128:(k + 1) * 128]


def _from_tiles(ref, rows):
    return jnp.concatenate([ref[pl.ds(k, rows, stride=SUB), :] for k in range(SUB)], axis=1)


def _tile_rows(ref, p):
    return ref.at[pl.ds(pl.multiple_of(p * SUB, SUB), SUB), :]


def _dispatch_kernel(nblk, nctx_blk, nbatch, d,
                     ztile_ref, zvalid_ref, pos_ref, h_ref, tbl_ref, nw_ref, xs_ref, v_scr, z_scr, sem, zsem):
    @pl.when(pl.program_id(0) == 0)
    def _():
        z_scr[...] = jnp.zeros_like(z_scr)
        tile_rows = MOE_TILE * SUB
        for e in range(2 * N_EXPERTS):
            @pl.when(zvalid_ref[e] == 1)
            def _():
                first = pl.multiple_of(ztile_ref[e] * tile_rows, SUB)
                pltpu.make_async_copy(z_scr, xs_ref.at[pl.ds(first, tile_rows), :], zsem).start()
        for e in range(2 * N_EXPERTS):
            @pl.when(zvalid_ref[e] == 1)
            def _():
                pltpu.make_async_copy(z_scr, xs_ref.at[pl.ds(0, tile_rows), :], zsem).wait()

    step = pl.program_id(0)
    nsteps = pl.num_programs(0)
    slot = step % 2
    v_slot = v_scr.at[slot]

    def wait_block(s):
        def body(i, c):
            for _ in range(2 * DMA_UNROLL):
                pltpu.make_async_copy(v_scr.at[s, pl.ds(0, SUB), :], xs_ref.at[pl.ds(0, SUB), :], sem.at[s]).wait()
            return c
        lax.fori_loop(0, BLK // DMA_UNROLL, body, 0)

    @pl.when(step >= 2)
    def _():
        wait_block(slot)

    row, _, _ = _block_row(nblk, nctx_blk, nbatch)
    shift = tbl_ref[0, pl.ds(row, 1), 3 * d:4 * d]
    scale = tbl_ref[0, pl.ds(row, 1), 4 * d:5 * d]
    _to_tiles(v_slot, _rms(h_ref[...], nw_ref[0]) * (1.0 + scale) + shift, BLK)

    def start(i, c):
        for u in range(DMA_UNROLL):
            r = i * DMA_UNROLL + u
            for k in range(2):
                pltpu.make_async_copy(_tile_rows(v_slot, r), _tile_rows(xs_ref, pos_ref[0, 0, 2 * r + k]),
                                      sem.at[slot]).start(priority=k)
        return c

    lax.fori_loop(0, BLK // DMA_UNROLL, start, 0)

    @pl.when(step == nsteps - 1)
    def _():
        wait_block(slot)

        @pl.when(nsteps >= 2)
        def _():
            wait_block(1 - slot)


def _dispatch(h, tbl, layer, nw, pos, ztile, zvalid, nslots, geom):
    nt, d = h.shape
    nblk, nctx_blk, nbatch = geom
    kern = functools.partial(_dispatch_kernel, nblk, nctx_blk, nbatch, d)
    gs = pltpu.PrefetchScalarGridSpec(
        num_scalar_prefetch=2,
        grid=(nt // BLK,),
        in_specs=[
            pl.BlockSpec((1, 1, 2 * BLK), lambda i, zt, zv: (i, 0, 0), memory_space=pltpu.SMEM),
            pl.BlockSpec((BLK, d), lambda i, zt, zv: (i, 0)),
            pl.BlockSpec((1, 8, N_MOD * d), lambda i, zt, zv: (layer, 0, 0)),
            pl.BlockSpec((1, 1, d), lambda i, zt, zv: (layer, 0, 0)),
        ],
        out_specs=pl.BlockSpec(memory_space=pl.ANY),
        scratch_shapes=[pltpu.VMEM((2, BLK * SUB, 128), F32), pltpu.VMEM((MOE_TILE * SUB, 128), F32),
                        pltpu.SemaphoreType.DMA((2,)), pltpu.SemaphoreType.DMA(())],
    )
    return pl.pallas_call(
        kern,
        grid_spec=gs,
        out_shape=jax.ShapeDtypeStruct((nslots * SUB, 128), F32),
        compiler_params=_params(("arbitrary",)),
        name=f"dispatch_{layer}",
    )(ztile, zvalid, pos.reshape(nt // BLK, 1, 2 * BLK), h, tbl, nw)


def _expert_kernel(n_fchunk, te_ref, nu_ref, x_ref, w1_ref, w3_ref, w2_ref, o_ref):
    @pl.when(pl.program_id(0) >= nu_ref[0])
    def _():
        o_ref[...] = jnp.zeros_like(o_ref)

    @pl.when(pl.program_id(0) < nu_ref[0])
    def _():
        x = _from_tiles(x_ref, MOE_TILE).astype(BF16)
        acc = None
        for k in range(n_fchunk):
            sl = slice(k * MOE_FCHUNK, (k + 1) * MOE_FCHUNK)
            a = _dot(x, w1_ref[0, 0, :, sl])
            b = _dot(x, w3_ref[0, 0, :, sl])
            act = (_silu(a) * b).astype(BF16)
            part = _dot(act, w2_ref[0, 0, sl, :])
            acc = part if acc is None else acc + part
        _to_tiles(o_ref, acc, MOE_TILE)


def _experts(x_sorted, tile_expert, n_used, j, w1, w3, w2):
    d, dffe = w1.shape[-2:]
    n_tiles = x_sorted.shape[0] // (MOE_TILE * SUB)
    kern = functools.partial(_expert_kernel, dffe // MOE_FCHUNK)
    tile = lambda i, te, nu: (jnp.minimum(i, nu[0] - 1), 0)
    wspec = lambda shape: pl.BlockSpec((1, 1) + shape, lambda i, te, nu: (j, te[i], 0, 0))
    gs = pltpu.PrefetchScalarGridSpec(
        num_scalar_prefetch=2,
        grid=(n_tiles,),
        in_specs=[pl.BlockSpec((MOE_TILE * SUB, 128), tile), wspec((d, dffe)), wspec((d, dffe)), wspec((dffe, d))],
        out_specs=pl.BlockSpec((MOE_TILE * SUB, 128), lambda i, te, nu: (i, 0)),
    )
    return pl.pallas_call(
        kern,
        grid_spec=gs,
        out_shape=jax.ShapeDtypeStruct(x_sorted.shape, F32),
        compiler_params=_params(("arbitrary",)),
        name=f"experts_{j}",
    )(tile_expert, n_used, x_sorted, w1, w3, w2)


def _combine_kernel(row_of_step, d, final,
                    pos_ref, posn_ref, h_ref, info_ref, tbl_ref, y_ref, *rest):
    if final:
        fw_ref, o_ref, buf, sem = rest
    else:
        o_ref, buf, sem = rest
    step = pl.program_id(0)
    nsteps = pl.num_programs(0)
    slot = step % 2

    def gather(p_ref, s):
        def body(i, c):
            for u in range(DMA_UNROLL):
                r = i * DMA_UNROLL + u
                for k in range(2):
                    pltpu.make_async_copy(_tile_rows(y_ref, p_ref[0, 0, 2 * r + k]), _tile_rows(buf.at[s, k], r),
                                          sem.at[s]).start(priority=k)
            return c
        lax.fori_loop(0, BLK // DMA_UNROLL, body, 0)

    @pl.when(step == 0)
    def _():
        gather(pos_ref, 0)

    @pl.when(step + 1 < nsteps)
    def _():
        gather(posn_ref, 1 - slot)

    def wait(i, c):
        for _ in range(2 * DMA_UNROLL):
            pltpu.make_async_copy(y_ref.at[pl.ds(0, SUB), :], buf.at[slot, 0, pl.ds(0, SUB), :], sem.at[slot]).wait()
        return c

    lax.fori_loop(0, BLK // DMA_UNROLL, wait, 0)
    info = info_ref[...]
    g1 = info[:, 4:5]
    g2 = info[:, 5:6]
    gate = tbl_ref[0, pl.ds(row_of_step(step), 1), 5 * d:6 * d]
    out = h_ref[...] + gate * (g1 * _from_tiles(buf.at[slot, 0], BLK) + g2 * _from_tiles(buf.at[slot, 1], BLK))
    if final:
        o_ref[0] = _rms(out, fw_ref[...])
    else:
        o_ref[...] = out


def _combine(h, info, tbl, layer, y_sorted, pos, geom, final_w=None):
    nt, d = h.shape
    nblk, nctx_blk, nbatch = geom
    final = final_w is not None
    pos3 = pos.reshape(nt // BLK, 1, 2 * BLK)
    if final:
        nlat = nblk - nctx_blk
        nsteps = nbatch * nlat
        blk = lambda i: (i // nlat) * nblk + nctx_blk + i % nlat
        row_of_step = lambda step: step // nlat
        out_spec = pl.BlockSpec((1, BLK, d), lambda i: (i // nlat, i % nlat, 0))
        out_shape = jax.ShapeDtypeStruct((nbatch, nlat * BLK, d), F32)
    else:
        nsteps = nt // BLK
        blk = lambda i: i
        row_of_step = lambda step: _block_row(nblk, nctx_blk, nbatch, step)[0]
        out_spec = pl.BlockSpec((BLK, d), lambda i: (i, 0))
        out_shape = jax.ShapeDtypeStruct((nt, d), F32)
    nxt = lambda i: blk(jnp.minimum(i + 1, nsteps - 1))
    in_specs = [
        pl.BlockSpec((1, 1, 2 * BLK), lambda i: (blk(i), 0, 0), memory_space=pltpu.SMEM),
        pl.BlockSpec((1, 1, 2 * BLK), lambda i: (nxt(i), 0, 0), memory_space=pltpu.SMEM),
        pl.BlockSpec((BLK, d), lambda i: (blk(i), 0)),
        pl.BlockSpec((BLK, 8), lambda i: (blk(i), 0)),
        pl.BlockSpec((1, 8, N_MOD * d), lambda i: (layer, 0, 0)),
        pl.BlockSpec(memory_space=pl.ANY),
    ]
    args = [pos3, pos3, h, info, tbl, y_sorted]
    if final:
        in_specs.append(pl.BlockSpec((1, d), lambda i: (0, 0)))
        args.append(final_w.reshape(1, d))
    return pl.pallas_call(
        functools.partial(_combine_kernel, row_of_step, d, final),
        grid=(nsteps,),
        in_specs=in_specs,
        out_specs=out_spec,
        out_shape=out_shape,
        scratch_shapes=[pltpu.VMEM((2, 2, BLK * SUB, 128), F32), pltpu.SemaphoreType.DMA((2,))],
        compiler_params=_params(("arbitrary",)),
        name=f"combine_{layer}",
    )(*args)


def _moe_layer(h, tbl, layer, nw, rw, strict_lower, w1, w3, w2, geom, final_w=None):
    nt, d = h.shape
    j = layer // 2
    info, counts = _route(h, tbl, layer, nw, rw, strict_lower, geom)
    cnt = counts[0, :N_EXPERTS].astype(jnp.int32)
    tiles = (cnt + MOE_TILE - 1) // MOE_TILE
    tile_end = jnp.cumsum(tiles)
    offs = (tile_end - tiles) * MOE_TILE
    n_used = tile_end[-1]
    idx = info[:, 0:2].astype(jnp.int32)
    group_start = jnp.sum(jnp.where(idx[..., None] == jnp.arange(N_EXPERTS), offs, 0), axis=-1)
    pos = (group_start + info[:, 2:4].astype(jnp.int32)).reshape(-1)
    n_tiles = (2 * nt) // MOE_TILE + N_EXPERTS
    t = jnp.minimum(jnp.arange(n_tiles, dtype=jnp.int32), n_used - 1)
    tile_expert = jnp.sum(t[:, None] >= tile_end[None, :], axis=1).astype(jnp.int32)
    spare = n_used + jnp.arange(N_EXPERTS, dtype=jnp.int32)
    ztile = jnp.concatenate([tile_end - 1, spare]).astype(jnp.int32)
    zvalid = jnp.concatenate([tiles > 0, spare < n_tiles]).astype(jnp.int32)
    x_sorted = _dispatch(h, tbl, layer, nw, pos, ztile, zvalid, n_tiles * MOE_TILE, geom)
    y_sorted = _experts(x_sorted, tile_expert, n_used.reshape(1).astype(jnp.int32), j, w1, w3, w2)
    return _combine(h, info, tbl, layer, y_sorted, pos, geom, final_w)


def _final_kernel(h_ref, w_ref, o_ref):
    o_ref[0] = _rms(h_ref[...], w_ref[...])


def _final_norm(h, w, nbatch, nblk, nctx_blk):
    nt, d = h.shape
    nlat = nblk - nctx_blk
    return pl.pallas_call(
        _final_kernel,
        grid=(nbatch, nlat),
        in_specs=[
            pl.BlockSpec((BLK, d), lambda b, j: (b * nblk + nctx_blk + j, 0)),
            pl.BlockSpec((1, d), lambda b, j: (0, 0)),
        ],
        out_specs=pl.BlockSpec((1, BLK, d), lambda b, j: (b, j, 0)),
        out_shape=jax.ShapeDtypeStruct((nbatch, nlat * BLK, d), F32),
        compiler_params=_params(("arbitrary", "arbitrary")),
        name="final_norm",
    )(h, w.reshape(1, d))


def kernel(x, c, ctx, c_ctx, ada_w, ada_b, norm_mix_w, norm_ffn_w, pool_w, pool_scale, ssd_in_w, ssd_conv_w, ssd_conv_b, ssd_A_log, ssd_dt_bias, ssd_D, ssd_norm_w, ssd_out_w, ffn_w1, ffn_w3, ffn_w2, moe_router_w, moe_w1, moe_w3, moe_w2, final_norm_w):
    nbatch, seq, d = x.shape
    ctx_len = ctx.shape[1]
    depth = ada_w.shape[0]
    d_inner = ssd_norm_w.shape[-1]
    assert ctx_len % BLK == 0 and seq % BLK == 0 and nbatch < 8
    assert d_inner == N_HEADS * HEAD_DIM and d % len(POOL_WINDOWS) == 0
    nblk = (ctx_len + seq) // BLK
    nctx_blk = ctx_len // BLK
    geom = (nblk, nctx_blk, nbatch)
    nchunk = (ctx_len + seq) // CHUNK
    ncc = ctx_len // CHUNK

    h = (ctx, x)
    cvec = jnp.zeros((8, d), F32).at[:nbatch].set(c).at[nbatch].set(c_ctx)
    tbl = _mod_table(cvec, ada_w, ada_b)

    vec3 = lambda a: a.reshape(a.shape[0], 1, a.shape[-1])
    nmix = vec3(norm_mix_w)
    nffn = vec3(norm_ffn_w)
    acat = _pool_matrices()
    scan_consts = _scan_constants()
    strict_lower = jnp.asarray(np.tril(np.ones((BLK, BLK), np.float32), -1), BF16)

    conv_dim = ssd_conv_w.shape[-1]
    wz = ssd_in_w[:, :, :d_inner].astype(BF16)
    wx = ssd_in_w[:, :, d_inner:d_inner + conv_dim].astype(BF16)
    wdt = ssd_in_w[:, :, d_inner + conv_dim:]
    pad_dt = lambda w: jnp.pad(w, ((0, 0), (0, 0), (0, CHUNK - N_HEADS))).astype(BF16)
    wdf = pad_dt(wdt[:, :, :N_HEADS])
    wdb = pad_dt(wdt[:, :, N_HEADS:])
    pad_h = lambda a: jnp.pad(a, ((0, 0), (0, 0), (0, CHUNK - N_HEADS)))
    dtb = pad_h(ssd_dt_bias)
    alog = pad_h(ssd_A_log)
    dskip = vec3(jnp.repeat(ssd_D, HEAD_DIM, axis=-1))
    rw = jnp.pad(moe_router_w, ((0, 0), (0, 0), (0, CHUNK - N_EXPERTS)))
    pool_wb = pool_w.astype(BF16)
    ffn = [w.astype(BF16) for w in (ffn_w1, ffn_w3, ffn_w2)]
    moe = [w.astype(BF16) for w in (moe_w1, moe_w3, moe_w2)]
    wout = ssd_out_w.astype(BF16)

    for i in range(depth):
        if i % 2 == 0:
            h = _even_layer(h, tbl, i, nmix, nffn, acat, pool_wb, vec3(pool_scale), *ffn, geom)
        else:
            z, xs, bc, dtf, dtbw = _ssd_in(h, tbl, i, nmix, wz, wx, wdf, wdb, ssd_conv_w,
                                           vec3(ssd_conv_b), dtb, geom)
            yf, yb = _ssd_scan(xs, bc, dtf, dtbw, alog, dskip, i, scan_consts, nbatch, nchunk, ncc)
            h = _ssd_out(yf, yb, z, h, tbl, i, vec3(ssd_norm_w), wout, geom)
            fin = final_norm_w if (i == depth - 1) else None
            h = _moe_layer(h, tbl, i, nffn, rw, strict_lower, *moe, geom, fin)
    if depth % 2 == 1:
        h = _final_norm(h, final_norm_w, nbatch, nblk, nctx_blk)
    return h
```

```python
import functools

import numpy as np
import jax
import jax.numpy as jnp
from jax import lax
from jax.experimental import pallas as pl
from jax.experimental.pallas import tpu as pltpu

F32 = jnp.float32
BF16 = jnp.bfloat16
EPS = 1e-6

BLK = 256
TILE_BLKS = 2
CHUNK = 128
SCAN_CHUNKS = 2
GRID_W = 64
POOL_WINDOWS = (2, 4, 8, 16)
N_MOD = 6
HEAD_DIM = 64
N_HEADS = 32
N_GROUPS = 4
D_STATE = 128
D_CONV = 4
CONV_LEFT = 2
HALO = 8
CONV_SLABS = 2
N_EXPERTS = 8
MOE_TILE = 512
MOE_FCHUNK = 512
FFN_FCHUNK = 512
VMEM_LIMIT = 56 * 2**20
LOG2E = 1.4426950408889634
SUB = 8
DMA_UNROLL = 8


def _dot(a, b):
    return jnp.dot(a, b, preferred_element_type=F32)


def _split2(x):
    hi = x.astype(BF16)
    lo = (x - hi.astype(F32)).astype(BF16)
    return hi, lo


def _split3(x):
    p0 = x.astype(BF16)
    r = x - p0.astype(F32)
    p1 = r.astype(BF16)
    p2 = (r - p1.astype(F32)).astype(BF16)
    return p0, p1, p2


def _dot_hi(a, b):
    ah, al = _split2(a)
    bh, bl = _split2(b)
    return _dot(ah, bh) + _dot(al, bh) + _dot(ah, bl)


def _sigmoid(x):
    return 1.0 / (1.0 + jnp.exp(-x))


def _silu(x):
    return x * _sigmoid(x)


def _rms(x, w):
    ms = jnp.mean(x * x, axis=-1, keepdims=True)
    return x * lax.rsqrt(ms + EPS) * w


def _params(sem):
    return pltpu.CompilerParams(dimension_semantics=sem, vmem_limit_bytes=VMEM_LIMIT)


def _resident(shape):
    nd = len(shape)
    return pl.BlockSpec(shape, lambda *_: (0,) * nd, pipeline_mode=pl.Buffered(1))


def _block_row(nblk, nctx_blk, nbatch, blk=None):
    if blk is None:
        blk = pl.program_id(0)
    b = blk // nblk
    j = blk - b * nblk
    is_ctx = j < nctx_blk
    return jnp.where(is_ctx, nbatch, b), is_ctx, j


def _mod_kernel(c_ref, w_ref, b_ref, o_ref):
    o_ref[0] = _dot_hi(_silu(c_ref[...]), w_ref[0]) + b_ref[0]


def _mod_table(cvec, ada_w, ada_b):
    depth, d, n = ada_w.shape
    tn = n // 4
    return pl.pallas_call(
        _mod_kernel,
        grid=(depth, n // tn),
        in_specs=[
            pl.BlockSpec((8, d), lambda l, j: (0, 0)),
            pl.BlockSpec((1, d, tn), lambda l, j: (l, 0, j)),
            pl.BlockSpec((1, 1, tn), lambda l, j: (l, 0, j)),
        ],
        out_specs=pl.BlockSpec((1, 8, tn), lambda l, j: (l, 0, j)),
        out_shape=jax.ShapeDtypeStruct((depth, 8, n), F32),
        compiler_params=_params(("arbitrary", "arbitrary")),
        name="mod_table",
    )(cvec, ada_w, ada_b.reshape(depth, 1, n))


def _even_kernel(nblk, nctx_blk, nbatch, d, from_inputs, *refs):
    n_h = 2 * TILE_BLKS if from_inputs else 1
    h_refs = refs[:n_h]
    tbl_ref, nw1_ref, nw2_ref, acat_ref, pw_ref, ps_ref, w1_ref, w3_ref, w2_ref, o_ref = refs[n_h:]

    def mods(sb):
        row, is_ctx, _ = _block_row(nblk, nctx_blk, nbatch, pl.program_id(0) * TILE_BLKS + sb)
        return (lambda k: tbl_ref[0, pl.ds(row, 1), k * d:(k + 1) * d]), is_ctx

    def mixer(sb):
        mod, is_ctx = mods(sb)
        kind = is_ctx.astype(jnp.int32)
        if from_inputs:
            h = jnp.where(is_ctx, h_refs[2 * sb][0], h_refs[2 * sb + 1][0])
        else:
            h = h_refs[0][sb * BLK:(sb + 1) * BLK, :]
        u = _rms(h, nw1_ref[0]) * (1.0 + mod(1)) + mod(0)
        gw = d // len(POOL_WINDOWS)
        ys = []
        for g in range(len(POOL_WINDOWS)):
            ug = u[:, g * gw:(g + 1) * gw]
            uh, ul = _split2(ug)
            p = _dot(acat_ref[kind, g], jnp.concatenate([uh, uh, ul], axis=0)) - ug
            ys.append(_dot(p.astype(BF16), pw_ref[0, g]))
        y = jnp.concatenate(ys, axis=1) * ps_ref[0]
        h1 = h + mod(2) * y
        v = (_rms(h1, nw2_ref[0]) * (1.0 + mod(4)) + mod(3)).astype(BF16)
        return h1, v

    dff = w1_ref.shape[-1]
    chunks = [slice(c, min(c + FFN_FCHUNK, dff)) for c in range(0, dff, FFN_FCHUNK)]
    n_f = len(chunks)

    def up(v, k):
        return _dot(v, w1_ref[0, :, chunks[k]]), _dot(v, w3_ref[0, :, chunks[k]])

    def down(ab, k):
        act = (_silu(ab[0]) * ab[1]).astype(BF16)
        return _dot(act, w2_ref[0, chunks[k], :])

    mixed = [mixer(sb) for sb in range(TILE_BLKS)]
    v = jnp.concatenate([m[1] for m in mixed], axis=0)
    ab = up(v, 0)
    acc = None
    for k in range(n_f):
        ab_next = up(v, k + 1) if k + 1 < n_f else None
        part = down(ab, k)
        acc = part if acc is None else acc + part
        ab = ab_next
    for sb in range(TILE_BLKS):
        mod, _ = mods(sb)
        o_ref[sb * BLK:(sb + 1) * BLK, :] = mixed[sb][0] + mod(5) * acc[sb * BLK:(sb + 1) * BLK, :]


def _pool_matrices():
    mats = np.zeros((2, len(POOL_WINDOWS), BLK, BLK), np.float64)
    for kind, seg in enumerate((GRID_W, BLK)):
        for g, w in enumerate(POOL_WINDOWS):
            lo = w // 2
            hi = w - 1 - lo
            for t in range(BLK):
                base = (t // seg) * seg
                tt = t - base
                start = max(tt - lo, 0)
                end = min(tt + hi + 1, seg)
                mats[kind, g, t, base + start:base + end] = 1.0 / (end - start)
    m32 = jnp.asarray(mats, F32)
    hi = m32.astype(BF16)
    lo = (m32 - hi.astype(F32)).astype(BF16)
    return jnp.concatenate([hi, lo, hi], axis=-1)


def _even_layer(h, tbl, layer, nw1, nw2, acat, pw, ps, w1, w3, w2, geom):
    nblk, nctx_blk, nbatch = geom
    from_inputs = isinstance(h, tuple)
    d = h[0].shape[-1]
    nt = nbatch * nblk * BLK
    dff = w1.shape[-1]
    j = layer // 2
    gw = d // len(POOL_WINDOWS)
    kern = functools.partial(_even_kernel, nblk, nctx_blk, nbatch, d, from_inputs)
    vec = lambda idx: pl.BlockSpec((1, 1, d), lambda i: (idx, 0, 0))
    tm = TILE_BLKS * BLK
    if from_inputs:
        def src_spec(sb, latent):
            def index(i):
                g = i * TILE_BLKS + sb
                b = g // nblk
                jb = g - b * nblk
                blk = jnp.maximum(jb - nctx_blk, 0) if latent else jnp.minimum(jb, nctx_blk - 1)
                return (b, blk, 0)
            return pl.BlockSpec((1, BLK, d), index)
        h_specs = [src_spec(sb, latent) for sb in range(TILE_BLKS) for latent in (False, True)]
        h_args = [h[0], h[1]] * TILE_BLKS
    else:
        h_specs = [pl.BlockSpec((tm, d), lambda i: (i, 0))]
        h_args = [h]
    return pl.pallas_call(
        kern,
        grid=(nt // tm,),
        in_specs=h_specs + [
            pl.BlockSpec((1, 8, N_MOD * d), lambda i: (layer, 0, 0)),
            vec(layer), vec(layer),
            _resident(acat.shape),
            pl.BlockSpec((1, len(POOL_WINDOWS), gw, gw), lambda i: (j, 0, 0, 0)),
            vec(j),
            pl.BlockSpec((1, d, dff), lambda i: (j, 0, 0), pipeline_mode=pl.Buffered(1)),
            pl.BlockSpec((1, d, dff), lambda i: (j, 0, 0), pipeline_mode=pl.Buffered(1)),
            pl.BlockSpec((1, dff, d), lambda i: (j, 0, 0), pipeline_mode=pl.Buffered(1)),
        ],
        out_specs=pl.BlockSpec((tm, d), lambda i: (i, 0)),
        out_shape=jax.ShapeDtypeStruct((nt, d), F32),
        compiler_params=_params(("arbitrary",)),
        name=f"pool_ffn_{layer}",
    )(*h_args, tbl, nw1, nw2, acat, pw, ps, w1, w3, w2)


def _ssd_in_kernel(nblk, nctx_blk, nbatch, d, d_inner,
                   hp_ref, h_ref, hn_ref, tbl_ref, nw_ref, wz_ref, wx_ref, wdf_ref, wdb_ref,
                   cw_ref, cb_ref, dtb_ref,
                   z_ref, xs_ref, bc_ref, dtf_ref, dtb_out_ref, *xbc_scrs):
    for sb in range(TILE_BLKS):
        rows = slice(sb * BLK, (sb + 1) * BLK)
        prev_rows = hp_ref[...] if sb == 0 else h_ref[sb * BLK - HALO:sb * BLK, :]
        next_rows = hn_ref[...] if sb == TILE_BLKS - 1 else h_ref[(sb + 1) * BLK:(sb + 1) * BLK + HALO, :]
        _ssd_in_block(nblk, nctx_blk, nbatch, d, d_inner, pl.program_id(0) * TILE_BLKS + sb,
                      prev_rows, h_ref[rows, :], next_rows, tbl_ref, nw_ref, wz_ref, wx_ref, wdf_ref, wdb_ref,
                      cw_ref, cb_ref, dtb_ref, z_ref, xs_ref, bc_ref, dtf_ref, dtb_out_ref, xbc_scrs[sb], rows)


def _ssd_in_block(nblk, nctx_blk, nbatch, d, d_inner, blk, h_prev, h_blk, h_next,
                  tbl_ref, nw_ref, wz_ref, wx_ref, wdf_ref, wdb_ref, cw_ref, cb_ref, dtb_ref,
                  z_ref, xs_ref, bc_ref, dtf_ref, dtb_out_ref, xbc_scr, rows):
    row, _, j = _block_row(nblk, nctx_blk, nbatch, blk)
    first = jnp.logical_or(j == 0, j == nctx_blk)
    last = jnp.logical_or(j == nctx_blk - 1, j == nblk - 1)
    shift = tbl_ref[0, pl.ds(row, 1), 0:d]
    scale = tbl_ref[0, pl.ds(row, 1), d:2 * d]
    nw = nw_ref[0]

    def modn(x):
        return _rms(x, nw) * (1.0 + scale) + shift

    uf = modn(h_blk)
    u = uf.astype(BF16)
    up = modn(h_prev) * jnp.where(first, 0.0, 1.0)
    un = modn(h_next) * jnp.where(last, 0.0, 1.0)
    u_ext = jnp.concatenate([up, uf, un], axis=0).astype(BF16)
    nslab = xbc_scr.shape[0]
    xs_slabs = d_inner // 128
    per = CONV_SLABS

    def project(c0):
        val = _dot(u_ext, wx_ref[0, :, c0 * 128:(c0 + per) * 128])
        for c in range(c0, c0 + per):
            xbc_scr[c, :, :] = val[:, (c - c0) * 128:(c - c0 + 1) * 128]

    def conv(c0):
        for c in range(c0, c0 + per):
            lanes = slice(c * 128, (c + 1) * 128)
            acc = cb_ref[0, :, lanes]
            for k in range(D_CONV):
                off = HALO - CONV_LEFT + k
                acc = acc + xbc_scr[c, off:off + BLK, :] * cw_ref[0, k:k + 1, lanes]
            y = _silu(acc).astype(BF16)
            if c < xs_slabs:
                xs_ref[rows, lanes] = y
            else:
                bc_ref[rows, (c - xs_slabs) * 128:(c - xs_slabs + 1) * 128] = y

    def softplus(x):
        return jnp.maximum(x, 0.0) + jnp.log1p(jnp.exp(-jnp.abs(x)))

    zw = per * 128
    z_pieces = d_inner // zw
    project(0)
    for i, c0 in enumerate(range(0, nslab, per)):
        if c0 + per < nslab:
            project(c0 + per)
        if i < z_pieces:
            z_ref[rows, i * zw:(i + 1) * zw] = _dot(u, wz_ref[0, :, i * zw:(i + 1) * zw]).astype(BF16)
        if i == z_pieces:
            dtf_ref[rows, :] = softplus(_dot(u, wdf_ref[0]) + dtb_ref[0, 0:1, :])
            dtb_out_ref[rows, :] = softplus(_dot(u, wdb_ref[0]) + dtb_ref[0, 1:2, :])
        conv(c0)


def _ssd_in(h, tbl, layer, nw, wz, wx, wdf, wdb, cw, cb, dtb, geom):
    nt, d = h.shape
    nblk, nctx_blk, nbatch = geom
    j = layer // 2
    d_inner = wz.shape[-1]
    conv_dim = wx.shape[-1]
    tm = TILE_BLKS * BLK
    hb = tm // HALO
    nh = nt // HALO
    kern = functools.partial(_ssd_in_kernel, nblk, nctx_blk, nbatch, d, d_inner)
    res3 = lambda a: pl.BlockSpec((1,) + a.shape[1:], lambda i: (j, 0, 0), pipeline_mode=pl.Buffered(1))
    return pl.pallas_call(
        kern,
        grid=(nt // tm,),
        in_specs=[
            pl.BlockSpec((HALO, d), lambda i: (jnp.maximum(i * hb - 1, 0), 0)),
            pl.BlockSpec((tm, d), lambda i: (i, 0)),
            pl.BlockSpec((HALO, d), lambda i: (jnp.minimum((i + 1) * hb, nh - 1), 0)),
            pl.BlockSpec((1, 8, N_MOD * d), lambda i: (layer, 0, 0)),
            pl.BlockSpec((1, 1, d), lambda i: (layer, 0, 0)),
            res3(wz), res3(wx), res3(wdf), res3(wdb),
            pl.BlockSpec((1, D_CONV, conv_dim), lambda i: (j, 0, 0)),
            pl.BlockSpec((1, 1, conv_dim), lambda i: (j, 0, 0)),
            pl.BlockSpec((1, 2, CHUNK), lambda i: (j, 0, 0)),
        ],
        out_specs=[
            pl.BlockSpec((tm, d_inner), lambda i: (i, 0)),
            pl.BlockSpec((tm, d_inner), lambda i: (i, 0)),
            pl.BlockSpec((tm, conv_dim - d_inner), lambda i: (i, 0)),
            pl.BlockSpec((tm, CHUNK), lambda i: (i, 0)),
            pl.BlockSpec((tm, CHUNK), lambda i: (i, 0)),
        ],
        out_shape=[
            jax.ShapeDtypeStruct((nt, d_inner), BF16),
            jax.ShapeDtypeStruct((nt, d_inner), BF16),
            jax.ShapeDtypeStruct((nt, conv_dim - d_inner), BF16),
            jax.ShapeDtypeStruct((nt, CHUNK), F32),
            jax.ShapeDtypeStruct((nt, CHUNK), F32),
        ],
        scratch_shapes=[pltpu.VMEM((conv_dim // 128, BLK + 2 * HALO, 128), F32)] * TILE_BLKS,
        compiler_params=_params(("arbitrary",)),
        name=f"ssd_in_{layer}",
    )(h, h, h, tbl, nw, wz, wx, wdf, wdb, cw, cb, dtb)


def _scan_prep(dt, alog, tri, reverse):
    t = dt.shape[0]
    lane = lax.broadcasted_iota(jnp.int32, (1, CHUNK), 1)
    a_row = jnp.where(lane < N_HEADS, -jnp.exp(alog), 0.0)
    a = dt * (a_row * LOG2E)
    p0, p1, p2 = _split3(a)
    cum = _dot(tri, p0) + _dot(tri, p1) + _dot(tri, p2)
    tot = cum[0:1, :] if reverse else cum[t - 1:t, :]
    dte = jnp.exp2(tot - cum)
    ecum = jnp.exp2(cum)
    cdec = jnp.exp2(tot)
    row_t = (cum - jnp.log2(dt)).T

    q0, q1, q2 = _split3(cum)
    stacked = (q0.astype(F32) + pltpu.roll(q1.astype(F32), N_HEADS, 1)
               + pltpu.roll(q2.astype(F32), 2 * N_HEADS, 1)).astype(BF16)
    li = lax.broadcasted_iota(jnp.int32, (t, t), 0)
    si = lax.broadcasted_iota(jnp.int32, (t, t), 1)
    keep = (si >= li) if reverse else (si <= li)
    return dict(wdt=(dt * dte).astype(BF16), ecum=ecum.astype(BF16),
                cdec=_split3(jnp.broadcast_to(cdec, (8, CHUNK))), stacked=stacked, row_t=row_t, keep=keep)


def _scan_expand(g, preps, e_ref, e3_ref):
    heads = N_HEADS // N_GROUPS
    e = e_ref[:, g * heads * HEAD_DIM:(g + 1) * heads * HEAD_DIM]
    stack = lambda key: jnp.concatenate([p[key] for p in preps], axis=0)
    w_x = _dot(stack("wdt"), e).astype(BF16)
    ec_x = _dot(stack("ecum"), e)
    cd_x = sum(_dot(jnp.concatenate([p["cdec"][i] for p in preps], axis=0), e) for i in range(3))
    colb = _dot(stack("stacked"), e3_ref[:, g * heads * CHUNK:(g + 1) * heads * CHUNK])
    rows = lambda a, n: a[n * CHUNK:(n + 1) * CHUNK, :]
    return [dict(w_x=rows(w_x, n), ec_x=rows(ec_x, n), cd_x=cd_x[8 * n:8 * n + 1, :], colb=rows(colb, n))
            for n in range(len(preps))]


def _scan_cb(g, bc_ref, rows):
    gn = N_GROUPS * D_STATE
    b_g = bc_ref[rows, g * D_STATE:(g + 1) * D_STATE]
    c_g = bc_ref[rows, gn + g * D_STATE:gn + (g + 1) * D_STATE]
    return lax.dot_general(c_g, b_g, (((1,), (1,)), ((), ())), preferred_element_type=F32)


def _scan_group(g, prep, ex, cb, x_ref, bc_ref, s_ref, y_ref, dskip, rows):
    t = CHUNK
    heads = N_HEADS // N_GROUPS
    gn = N_GROUPS * D_STATE
    gp = heads * HEAD_DIM
    first_head = lax.broadcasted_iota(jnp.int32, (t, 2 * HEAD_DIM), 1) < HEAD_DIM
    b_g = bc_ref[rows, g * D_STATE:(g + 1) * D_STATE]
    c_g = bc_ref[rows, gn + g * D_STATE:gn + (g + 1) * D_STATE]
    def decay_pair(q):
        ms = []
        for hl in (2 * q, 2 * q + 1):
            hh = g * heads + hl
            seg = ex["colb"][:, hl * CHUNK:(hl + 1) * CHUNK] - prep["row_t"][hh:hh + 1, :]
            lmat = jnp.exp2(jnp.where(prep["keep"], seg, -1e30))
            ms.append((cb * lmat).astype(BF16))
        return jnp.concatenate(ms, axis=1)

    ydiag = []
    m_pair = decay_pair(0)
    for q in range(heads // 2):
        m_next = decay_pair(q + 1) if q + 1 < heads // 2 else None
        x_pair = x_ref[rows, (g * heads + 2 * q) * HEAD_DIM:(g * heads + 2 * q + 2) * HEAD_DIM]
        zero = jnp.zeros_like(x_pair)
        rhs = jnp.concatenate([jnp.where(first_head, x_pair, zero),
                               jnp.where(first_head, zero, x_pair)], axis=0)
        ydiag.append(_dot(m_pair, rhs))
        m_pair = m_next
    sl = slice(g * gp, (g + 1) * gp)
    xg = x_ref[rows, sl]
    s_old = s_ref[:, sl]
    y_g = jnp.concatenate(ydiag, axis=1) + _dot(c_g, s_old.astype(BF16)) * ex["ec_x"]
    if dskip is not None:
        y_g = y_g + xg.astype(F32) * dskip[:, sl]
    y_ref[rows, sl] = y_g.astype(y_ref.dtype)
    s_new = lax.dot_general(b_g, xg * ex["w_x"], (((0,), (0,)), ((), ())), preferred_element_type=F32)
    s_ref[:, sl] = s_old * ex["cd_x"] + s_new


def _ssd_scan_kernel(xf_ref, bcf_ref, dtf_ref, xb_ref, bcb_ref, dtb_ref, alog_ref, dsk_ref,
                     tril_ref, triu_ref, e_ref, e3_ref, yf_ref, yb_ref, sf_ref, sb_ref):
    @pl.when(pl.program_id(1) == 0)
    def _():
        sf_ref[...] = jnp.zeros_like(sf_ref)
        sb_ref[...] = jnp.zeros_like(sb_ref)

    rows_f = [slice(n * CHUNK, (n + 1) * CHUNK) for n in range(SCAN_CHUNKS)]
    rows_b = rows_f[::-1]
    pf = [_scan_prep(dtf_ref[r, :], alog_ref[0, 0:1, :], tril_ref[...], False) for r in rows_f]
    pb = [_scan_prep(dtb_ref[r, :], alog_ref[0, 1:2, :], triu_ref[...], True) for r in rows_b]
    cbf = [[_scan_cb(g, bcf_ref, r) for r in rows_f] for g in range(N_GROUPS)]
    cbb = [[_scan_cb(g, bcb_ref, r) for r in rows_b] for g in range(N_GROUPS)]
    exf = _scan_expand(0, pf, e_ref, e3_ref)
    exb = _scan_expand(0, pb, e_ref, e3_ref)
    for g in range(N_GROUPS):
        nxf = _scan_expand(g + 1, pf, e_ref, e3_ref) if g + 1 < N_GROUPS else None
        nxb = _scan_expand(g + 1, pb, e_ref, e3_ref) if g + 1 < N_GROUPS else None
        for n in range(SCAN_CHUNKS):
            _scan_group(g, pf[n], exf[n], cbf[g][n], xf_ref, bcf_ref, sf_ref, yf_ref, dsk_ref[0], rows_f[n])
            _scan_group(g, pb[n], exb[n], cbb[g][n], xb_ref, bcb_ref, sb_ref, yb_ref, None, rows_b[n])
        exf, exb = nxf, nxb


def _scan_constants():
    li = np.arange(CHUNK)[:, None]
    ti = np.arange(CHUNK)[None, :]
    tril = (ti <= li).astype(np.float32)
    triu = (ti >= li).astype(np.float32)
    e = np.zeros((CHUNK, N_HEADS * HEAD_DIM), np.float32)
    e3 = np.zeros((CHUNK, N_HEADS * CHUNK), np.float32)
    for h in range(N_HEADS):
        e[h, h * HEAD_DIM:(h + 1) * HEAD_DIM] = 1.0
        for piece in range(3):
            e3[piece * N_HEADS + h, h * CHUNK:(h + 1) * CHUNK] = 1.0
    return tuple(jnp.asarray(m, BF16) for m in (tril, triu, e, e3))


def _ssd_scan(xs, bc, dtf, dtb, alog, dskip, layer, consts, nbatch, nchunk, ncc):
    nt, d_inner = xs.shape
    bcw = bc.shape[1]
    j = layer // 2
    tril, triu, e, e3 = consts

    assert nchunk % SCAN_CHUNKS == 0 and ncc % SCAN_CHUNKS == 0
    nstep = nchunk // SCAN_CHUNKS
    ncs = ncc // SCAN_CHUNKS
    rows = SCAN_CHUNKS * CHUNK

    def fwd(b, c):
        return (b * nstep + c, 0)

    def bwd(b, c):
        return (b * nstep + jnp.where(c < ncs, ncs - 1 - c, nstep - 1 - (c - ncs)), 0)

    return pl.pallas_call(
        _ssd_scan_kernel,
        grid=(nbatch, nstep),
        in_specs=[
            pl.BlockSpec((rows, d_inner), fwd), pl.BlockSpec((rows, bcw), fwd), pl.BlockSpec((rows, CHUNK), fwd),
            pl.BlockSpec((rows, d_inner), bwd), pl.BlockSpec((rows, bcw), bwd), pl.BlockSpec((rows, CHUNK), bwd),
            pl.BlockSpec((1, 2, CHUNK), lambda b, c: (j, 0, 0)),
            pl.BlockSpec((1, 1, d_inner), lambda b, c: (j, 0, 0)),
            _resident(tril.shape), _resident(triu.shape), _resident(e.shape), _resident(e3.shape),
        ],
        out_specs=[pl.BlockSpec((rows, d_inner), fwd), pl.BlockSpec((rows, d_inner), bwd)],
        out_shape=[jax.ShapeDtypeStruct((nt, d_inner), BF16)] * 2,
        scratch_shapes=[pltpu.VMEM((D_STATE, d_inner), F32)] * 2,
        compiler_params=_params(("arbitrary", "arbitrary")),
        name=f"ssd_scan_{layer}",
    )(xs, bc, dtf, xs, bc, dtb, alog, dskip, tril, triu, e, e3)


def _ssd_out_kernel(nblk, nctx_blk, nbatch, d,
                    yf_ref, yb_ref, z_ref, h_ref, tbl_ref, nw_ref, wo_ref, o_ref):
    y = (yf_ref[...].astype(F32) + yb_ref[...].astype(F32)) * _silu(z_ref[...].astype(F32))
    proj = _dot(_rms(y, nw_ref[0]).astype(BF16), wo_ref[0])
    for sb in range(TILE_BLKS):
        rows = slice(sb * BLK, (sb + 1) * BLK)
        row, _, _ = _block_row(nblk, nctx_blk, nbatch, pl.program_id(0) * TILE_BLKS + sb)
        gate = tbl_ref[0, pl.ds(row, 1), 2 * d:3 * d]
        o_ref[rows, :] = h_ref[rows, :] + gate * proj[rows, :]


def _ssd_out(yf, yb, z, h, tbl, layer, nw, wo, geom):
    nt, d = h.shape
    d_inner = z.shape[1]
    nblk, nctx_blk, nbatch = geom
    j = layer // 2
    kern = functools.partial(_ssd_out_kernel, nblk, nctx_blk, nbatch, d)
    tm = TILE_BLKS * BLK
    big = pl.BlockSpec((tm, d_inner), lambda i: (i, 0))
    return pl.pallas_call(
        kern,
        grid=(nt // tm,),
        in_specs=[
            big, big, big,
            pl.BlockSpec((tm, d), lambda i: (i, 0)),
            pl.BlockSpec((1, 8, N_MOD * d), lambda i: (layer, 0, 0)),
            pl.BlockSpec((1, 1, d_inner), lambda i: (j, 0, 0)),
            pl.BlockSpec((1, d_inner, d), lambda i: (j, 0, 0), pipeline_mode=pl.Buffered(1)),
        ],
        out_specs=pl.BlockSpec((tm, d), lambda i: (i, 0)),
        out_shape=jax.ShapeDtypeStruct((nt, d), F32),
        compiler_params=_params(("arbitrary",)),
        name=f"ssd_out_{layer}",
    )(yf, yb, z, h, tbl, nw, wo)


def _route_kernel(nblk, nctx_blk, nbatch, d,
                  h_ref, tbl_ref, nw_ref, rw_ref, sl_ref, info_ref, cnt_ref, carry_ref):
    @pl.when(pl.program_id(0) == 0)
    def _():
        carry_ref[...] = jnp.zeros_like(carry_ref)

    row, _, _ = _block_row(nblk, nctx_blk, nbatch)
    shift = tbl_ref[0, pl.ds(row, 1), 3 * d:4 * d]
    scale = tbl_ref[0, pl.ds(row, 1), 4 * d:5 * d]
    v = _rms(h_ref[...], nw_ref[0]) * (1.0 + scale) + shift
    lane = lax.broadcasted_iota(jnp.int32, (BLK, CHUNK), 1).astype(F32)
    logits = jnp.where(lane < N_EXPERTS, _dot_hi(v, rw_ref[0]), -jnp.inf)
    m1 = jnp.max(logits, axis=1, keepdims=True)
    i1 = jnp.min(jnp.where(logits == m1, lane, float(CHUNK)), axis=1, keepdims=True)
    rest = jnp.where(lane == i1, -jnp.inf, logits)
    m2 = jnp.max(rest, axis=1, keepdims=True)
    i2 = jnp.min(jnp.where(rest == m2, lane, float(CHUNK)), axis=1, keepdims=True)
    e2 = jnp.exp(m2 - m1)
    g1 = 1.0 / (1.0 + e2)
    g2 = e2 / (1.0 + e2)
    oh1 = (lane == i1)
    oh2 = (lane == i2)
    member = jnp.where(jnp.logical_or(oh1, oh2), 1.0, 0.0)
    before = carry_ref[...] + _dot(sl_ref[...], member.astype(BF16))
    r1 = jnp.sum(jnp.where(oh1, before, 0.0), axis=1, keepdims=True)
    r2 = jnp.sum(jnp.where(oh2, before, 0.0), axis=1, keepdims=True)
    total = carry_ref[...] + jnp.sum(member, axis=0, keepdims=True)
    carry_ref[...] = total
    cnt_ref[...] = jnp.broadcast_to(total, cnt_ref.shape)
    lane8 = lax.broadcasted_iota(jnp.int32, (BLK, 8), 1)
    info = jnp.where(lane8 == 0, i1,
           jnp.where(lane8 == 1, i2,
           jnp.where(lane8 == 2, r1,
           jnp.where(lane8 == 3, r2,
           jnp.where(lane8 == 4, g1,
           jnp.where(lane8 == 5, g2, 0.0))))))
    info_ref[...] = info


def _route(h, tbl, layer, nw, rw, strict_lower, geom):
    nt, d = h.shape
    nblk, nctx_blk, nbatch = geom
    j = layer // 2
    kern = functools.partial(_route_kernel, nblk, nctx_blk, nbatch, d)
    return pl.pallas_call(
        kern,
        grid=(nt // BLK,),
        in_specs=[
            pl.BlockSpec((BLK, d), lambda i: (i, 0)),
            pl.BlockSpec((1, 8, N_MOD * d), lambda i: (layer, 0, 0)),
            pl.BlockSpec((1, 1, d), lambda i: (layer, 0, 0)),
            pl.BlockSpec((1, d, CHUNK), lambda i: (j, 0, 0)),
            _resident(strict_lower.shape),
        ],
        out_specs=[
            pl.BlockSpec((BLK, 8), lambda i: (i, 0)),
            pl.BlockSpec((8, CHUNK), lambda i: (0, 0)),
        ],
        out_shape=[jax.ShapeDtypeStruct((nt, 8), F32), jax.ShapeDtypeStruct((8, CHUNK), F32)],
        scratch_shapes=[pltpu.VMEM((1, CHUNK), F32)],
        compiler_params=_params(("arbitrary",)),
        name=f"route_{layer}",
    )(h, tbl, nw, rw, strict_lower)


def _to_tiles(ref, x, rows):
    for k in range(SUB):
        ref[pl.ds(k, rows, stride=SUB), :] = x[:, k * 128:(k + 1) * 128]


def _from_tiles(ref, rows):
    return jnp.concatenate([ref[pl.ds(k, rows, stride=SUB), :] for k in range(SUB)], axis=1)


def _tile_rows(ref, p):
    return ref.at[pl.ds(pl.multiple_of(p * SUB, SUB), SUB), :]


def _dispatch_kernel(nblk, nctx_blk, nbatch, d,
                     ztile_ref, zvalid_ref, pos_ref, h_ref, tbl_ref, nw_ref, xs_ref, v_scr, z_scr, sem, zsem):
    @pl.when(pl.program_id(0) == 0)
    def _():
        z_scr[...] = jnp.zeros_like(z_scr)
        tile_rows = MOE_TILE * SUB
        for e in range(2 * N_EXPERTS):
            @pl.when(zvalid_ref[e] == 1)
            def _():
                first = pl.multiple_of(ztile_ref[e] * tile_rows, SUB)
                pltpu.make_async_copy(z_scr, xs_ref.at[pl.ds(first, tile_rows), :], zsem).start()
        for e in range(2 * N_EXPERTS):
            @pl.when(zvalid_ref[e] == 1)
            def _():
                pltpu.make_async_copy(z_scr, xs_ref.at[pl.ds(0, tile_rows), :], zsem).wait()

    step = pl.program_id(0)
    nsteps = pl.num_programs(0)
    slot = step % 2
    v_slot = v_scr.at[slot]

    def wait_block(s):
        def body(i, c):
            for _ in range(2 * DMA_UNROLL):
                pltpu.make_async_copy(v_scr.at[s, pl.ds(0, SUB), :], xs_ref.at[pl.ds(0, SUB), :], sem.at[s]).wait()
            return c
        lax.fori_loop(0, BLK // DMA_UNROLL, body, 0)

    @pl.when(step >= 2)
    def _():
        wait_block(slot)

    row, _, _ = _block_row(nblk, nctx_blk, nbatch)
    shift = tbl_ref[0, pl.ds(row, 1), 3 * d:4 * d]
    scale = tbl_ref[0, pl.ds(row, 1), 4 * d:5 * d]
    _to_tiles(v_slot, _rms(h_ref[...], nw_ref[0]) * (1.0 + scale) + shift, BLK)

    def start(i, c):
        for u in range(DMA_UNROLL):
            r = i * DMA_UNROLL + u
            for k in range(2):
                pltpu.make_async_copy(_tile_rows(v_slot, r), _tile_rows(xs_ref, pos_ref[0, 0, 2 * r + k]),
                                      sem.at[slot]).start(priority=k)
        return c

    lax.fori_loop(0, BLK // DMA_UNROLL, start, 0)

    @pl.when(step == nsteps - 1)
    def _():
        wait_block(slot)

        @pl.when(nsteps >= 2)
        def _():
            wait_block(1 - slot)


def _dispatch(h, tbl, layer, nw, pos, ztile, zvalid, nslots, geom):
    nt, d = h.shape
    nblk, nctx_blk, nbatch = geom
    kern = functools.partial(_dispatch_kernel, nblk, nctx_blk, nbatch, d)
    gs = pltpu.PrefetchScalarGridSpec(
        num_scalar_prefetch=2,
        grid=(nt // BLK,),
        in_specs=[
            pl.BlockSpec((1, 1, 2 * BLK), lambda i, zt, zv: (i, 0, 0), memory_space=pltpu.SMEM),
            pl.BlockSpec((BLK, d), lambda i, zt, zv: (i, 0)),
            pl.BlockSpec((1, 8, N_MOD * d), lambda i, zt, zv: (layer, 0, 0)),
            pl.BlockSpec((1, 1, d), lambda i, zt, zv: (layer, 0, 0)),
        ],
        out_specs=pl.BlockSpec(memory_space=pl.ANY),
        scratch_shapes=[pltpu.VMEM((2, BLK * SUB, 128), F32), pltpu.VMEM((MOE_TILE * SUB, 128), F32),
                        pltpu.SemaphoreType.DMA((2,)), pltpu.SemaphoreType.DMA(())],
    )
    return pl.pallas_call(
        kern,
        grid_spec=gs,
        out_shape=jax.ShapeDtypeStruct((nslots * SUB, 128), F32),
        compiler_params=_params(("arbitrary",)),
        name=f"dispatch_{layer}",
    )(ztile, zvalid, pos.reshape(nt // BLK, 1, 2 * BLK), h, tbl, nw)


def _expert_kernel(n_fchunk, te_ref, nu_ref, x_ref, w1_ref, w3_ref, w2_ref, o_ref):
    @pl.when(pl.program_id(0) >= nu_ref[0])
    def _():
        o_ref[...] = jnp.zeros_like(o_ref)

    @pl.when(pl.program_id(0) < nu_ref[0])
    def _():
        x = _from_tiles(x_ref, MOE_TILE).astype(BF16)
        acc = None
        for k in range(n_fchunk):
            sl = slice(k * MOE_FCHUNK, (k + 1) * MOE_FCHUNK)
            a = _dot(x, w1_ref[0, 0, :, sl])
            b = _dot(x, w3_ref[0, 0, :, sl])
            act = (_silu(a) * b).astype(BF16)
            part = _dot(act, w2_ref[0, 0, sl, :])
            acc = part if acc is None else acc + part
        _to_tiles(o_ref, acc, MOE_TILE)


def _experts(x_sorted, tile_expert, n_used, j, w1, w3, w2):
    d, dffe = w1.shape[-2:]
    n_tiles = x_sorted.shape[0] // (MOE_TILE * SUB)
    kern = functools.partial(_expert_kernel, dffe // MOE_FCHUNK)
    tile = lambda i, te, nu: (jnp.minimum(i, nu[0] - 1), 0)
    wspec = lambda shape: pl.BlockSpec((1, 1) + shape, lambda i, te, nu: (j, te[i], 0, 0))
    gs = pltpu.PrefetchScalarGridSpec(
        num_scalar_prefetch=2,
        grid=(n_tiles,),
        in_specs=[pl.BlockSpec((MOE_TILE * SUB, 128), tile), wspec((d, dffe)), wspec((d, dffe)), wspec((dffe, d))],
        out_specs=pl.BlockSpec((MOE_TILE * SUB, 128), lambda i, te, nu: (i, 0)),
    )
    return pl.pallas_call(
        kern,
        grid_spec=gs,
        out_shape=jax.ShapeDtypeStruct(x_sorted.shape, F32),
        compiler_params=_params(("arbitrary",)),
        name=f"experts_{j}",
    )(tile_expert, n_used, x_sorted, w1, w3, w2)


def _combine_kernel(row_of_step, d, final,
                    pos_ref, posn_ref, h_ref, info_ref, tbl_ref, y_ref, *rest):
    if final:
        fw_ref, o_ref, buf, sem = rest
    else:
        o_ref, buf, sem = rest
    step = pl.program_id(0)
    nsteps = pl.num_programs(0)
    slot = step % 2

    def gather(p_ref, s):
        def body(i, c):
            for u in range(DMA_UNROLL):
                r = i * DMA_UNROLL + u
                for k in range(2):
                    pltpu.make_async_copy(_tile_rows(y_ref, p_ref[0, 0, 2 * r + k]), _tile_rows(buf.at[s, k], r),
                                          sem.at[s]).start(priority=k)
            return c
        lax.fori_loop(0, BLK // DMA_UNROLL, body, 0)

    @pl.when(step == 0)
    def _():
        gather(pos_ref, 0)

    @pl.when(step + 1 < nsteps)
    def _():
        gather(posn_ref, 1 - slot)

    def wait(i, c):
        for _ in range(2 * DMA_UNROLL):
            pltpu.make_async_copy(y_ref.at[pl.ds(0, SUB), :], buf.at[slot, 0, pl.ds(0, SUB), :], sem.at[slot]).wait()
        return c

    lax.fori_loop(0, BLK // DMA_UNROLL, wait, 0)
    info = info_ref[...]
    g1 = info[:, 4:5]
    g2 = info[:, 5:6]
    gate = tbl_ref[0, pl.ds(row_of_step(step), 1), 5 * d:6 * d]
    out = h_ref[...] + gate * (g1 * _from_tiles(buf.at[slot, 0], BLK) + g2 * _from_tiles(buf.at[slot, 1], BLK))
    if final:
        o_ref[0] = _rms(out, fw_ref[...])
    else:
        o_ref[...] = out


def _combine(h, info, tbl, layer, y_sorted, pos, geom, final_w=None):
    nt, d = h.shape
    nblk, nctx_blk, nbatch = geom
    final = final_w is not None
    pos3 = pos.reshape(nt // BLK, 1, 2 * BLK)
    if final:
        nlat = nblk - nctx_blk
        nsteps = nbatch * nlat
        blk = lambda i: (i // nlat) * nblk + nctx_blk + i % nlat
        row_of_step = lambda step: step // nlat
        out_spec = pl.BlockSpec((1, BLK, d), lambda i: (i // nlat, i % nlat, 0))
        out_shape = jax.ShapeDtypeStruct((nbatch, nlat * BLK, d), F32)
    else:
        nsteps = nt // BLK
        blk = lambda i: i
        row_of_step = lambda step: _block_row(nblk, nctx_blk, nbatch, step)[0]
        out_spec = pl.BlockSpec((BLK, d), lambda i: (i, 0))
        out_shape = jax.ShapeDtypeStruct((nt, d), F32)
    nxt = lambda i: blk(jnp.minimum(i + 1, nsteps - 1))
    in_specs = [
        pl.BlockSpec((1, 1, 2 * BLK), lambda i: (blk(i), 0, 0), memory_space=pltpu.SMEM),
        pl.BlockSpec((1, 1, 2 * BLK), lambda i: (nxt(i), 0, 0), memory_space=pltpu.SMEM),
        pl.BlockSpec((BLK, d), lambda i: (blk(i), 0)),
        pl.BlockSpec((BLK, 8), lambda i: (blk(i), 0)),
        pl.BlockSpec((1, 8, N_MOD * d), lambda i: (layer, 0, 0)),
        pl.BlockSpec(memory_space=pl.ANY),
    ]
    args = [pos3, pos3, h, info, tbl, y_sorted]
    if final:
        in_specs.append(pl.BlockSpec((1, d), lambda i: (0, 0)))
        args.append(final_w.reshape(1, d))
    return pl.pallas_call(
        functools.partial(_combine_kernel, row_of_step, d, final),
        grid=(nsteps,),
        in_specs=in_specs,
        out_specs=out_spec,
        out_shape=out_shape,
        scratch_shapes=[pltpu.VMEM((2, 2, BLK * SUB, 128), F32), pltpu.SemaphoreType.DMA((2,))],
        compiler_params=_params(("arbitrary",)),
        name=f"combine_{layer}",
    )(*args)


def _moe_layer(h, tbl, layer, nw, rw, strict_lower, w1, w3, w2, geom, final_w=None):
    nt, d = h.shape
    j = layer // 2
    info, counts = _route(h, tbl, layer, nw, rw, strict_lower, geom)
    cnt = counts[0, :N_EXPERTS].astype(jnp.int32)
    tiles = (cnt + MOE_TILE - 1) // MOE_TILE
    tile_end = jnp.cumsum(tiles)
    offs = (tile_end - tiles) * MOE_TILE
    n_used = tile_end[-1]
    idx = info[:, 0:2].astype(jnp.int32)
    group_start = jnp.sum(jnp.where(idx[..., None] == jnp.arange(N_EXPERTS), offs, 0), axis=-1)
    pos = (group_start + info[:, 2:4].astype(jnp.int32)).reshape(-1)
    n_tiles = (2 * nt) // MOE_TILE + N_EXPERTS
    t = jnp.minimum(jnp.arange(n_tiles, dtype=jnp.int32), n_used - 1)
    tile_expert = jnp.sum(t[:, None] >= tile_end[None, :], axis=1).astype(jnp.int32)
    spare = n_used + jnp.arange(N_EXPERTS, dtype=jnp.int32)
    ztile = jnp.concatenate([tile_end - 1, spare]).astype(jnp.int32)
    zvalid = jnp.concatenate([tiles > 0, spare < n_tiles]).astype(jnp.int32)
    x_sorted = _dispatch(h, tbl, layer, nw, pos, ztile, zvalid, n_tiles * MOE_TILE, geom)
    y_sorted = _experts(x_sorted, tile_expert, n_used.reshape(1).astype(jnp.int32), j, w1, w3, w2)
    return _combine(h, info, tbl, layer, y_sorted, pos, geom, final_w)


def _final_kernel(h_ref, w_ref, o_ref):
    o_ref[0] = _rms(h_ref[...], w_ref[...])


def _final_norm(h, w, nbatch, nblk, nctx_blk):
    nt, d = h.shape
    nlat = nblk - nctx_blk
    return pl.pallas_call(
        _final_kernel,
        grid=(nbatch, nlat),
        in_specs=[
            pl.BlockSpec((BLK, d), lambda b, j: (b * nblk + nctx_blk + j, 0)),
            pl.BlockSpec((1, d), lambda b, j: (0, 0)),
        ],
        out_specs=pl.BlockSpec((1, BLK, d), lambda b, j: (b, j, 0)),
        out_shape=jax.ShapeDtypeStruct((nbatch, nlat * BLK, d), F32),
        compiler_params=_params(("arbitrary", "arbitrary")),
        name="final_norm",
    )(h, w.reshape(1, d))


def kernel(x, c, ctx, c_ctx, ada_w, ada_b, norm_mix_w, norm_ffn_w, pool_w, pool_scale, ssd_in_w, ssd_conv_w, ssd_conv_b, ssd_A_log, ssd_dt_bias, ssd_D, ssd_norm_w, ssd_out_w, ffn_w1, ffn_w3, ffn_w2, moe_router_w, moe_w1, moe_w3, moe_w2, final_norm_w):
    nbatch, seq, d = x.shape
    ctx_len = ctx.shape[1]
    depth = ada_w.shape[0]
    d_inner = ssd_norm_w.shape[-1]
    assert ctx_len % BLK == 0 and seq % BLK == 0 and nbatch < 8
    assert d_inner == N_HEADS * HEAD_DIM and d % len(POOL_WINDOWS) == 0
    nblk = (ctx_len + seq) // BLK
    nctx_blk = ctx_len // BLK
    geom = (nblk, nctx_blk, nbatch)
    nchunk = (ctx_len + seq) // CHUNK
    ncc = ctx_len // CHUNK

    h = (ctx, x)
    cvec = jnp.zeros((8, d), F32).at[:nbatch].set(c).at[nbatch].set(c_ctx)
    tbl = _mod_table(cvec, ada_w, ada_b)

    vec3 = lambda a: a.reshape(a.shape[0], 1, a.shape[-1])
    nmix = vec3(norm_mix_w)
    nffn = vec3(norm_ffn_w)
    acat = _pool_matrices()
    scan_consts = _scan_constants()
    strict_lower = jnp.asarray(np.tril(np.ones((BLK, BLK), np.float32), -1), BF16)

    conv_dim = ssd_conv_w.shape[-1]
    wz = ssd_in_w[:, :, :d_inner].astype(BF16)
    wx = ssd_in_w[:, :, d_inner:d_inner + conv_dim].astype(BF16)
    wdt = ssd_in_w[:, :, d_inner + conv_dim:]
    pad_dt = lambda w: jnp.pad(w, ((0, 0), (0, 0), (0, CHUNK - N_HEADS))).astype(BF16)
    wdf = pad_dt(wdt[:, :, :N_HEADS])
    wdb = pad_dt(wdt[:, :, N_HEADS:])
    pad_h = lambda a: jnp.pad(a, ((0, 0), (0, 0), (0, CHUNK - N_HEADS)))
    dtb = pad_h(ssd_dt_bias)
    alog = pad_h(ssd_A_log)
    dskip = vec3(jnp.repeat(ssd_D, HEAD_DIM, axis=-1))
    rw = jnp.pad(moe_router_w, ((0, 0), (0, 0), (0, CHUNK - N_EXPERTS)))
    pool_wb = pool_w.astype(BF16)
    ffn = [w.astype(BF16) for w in (ffn_w1, ffn_w3, ffn_w2)]
    moe = [w.astype(BF16) for w in (moe_w1, moe_w3, moe_w2)]
    wout = ssd_out_w.astype(BF16)

    for i in range(depth):
        if i % 2 == 0:
            h = _even_layer(h, tbl, i, nmix, nffn, acat, pool_wb, vec3(pool_scale), *ffn, geom)
        else:
            z, xs, bc, dtf, dtbw = _ssd_in(h, tbl, i, nmix, wz, wx, wdf, wdb, ssd_conv_w,
                                           vec3(ssd_conv_b), dtb, geom)
            yf, yb = _ssd_scan(xs, bc, dtf, dtbw, alog, dskip, i, scan_consts, nbatch, nchunk, ncc)
            h = _ssd_out(yf, yb, z, h, tbl, i, vec3(ssd_norm_w), wout, geom)
            fin = final_norm_w if (i == depth - 1) else None
            h = _moe_layer(h, tbl, i, nffn, rw, strict_lower, *moe, geom, fin)
    if depth % 2 == 1:
        h = _final_norm(h, final_norm_w, nbatch, nblk, nctx_blk)
    return h
```

```python
import functools

import numpy as np
import jax
import jax.numpy as jnp
from jax import lax
from jax.experimental import pallas as pl
from jax.experimental.pallas import tpu as pltpu

F32 = jnp.float32
BF16 = jnp.bfloat16
EPS = 1e-6

BLK = 256
TILE_BLKS = 2
CHUNK = 128
SCAN_CHUNKS = 2
GRID_W = 64
POOL_WINDOWS = (2, 4, 8, 16)
N_MOD = 6
HEAD_DIM = 64
N_HEADS = 32
N_GROUPS = 4
D_STATE = 128
D_CONV = 4
CONV_LEFT = 2
HALO = 8
CONV_SLABS = 2
N_EXPERTS = 8
MOE_TILE = 512
MOE_FCHUNK = 512
FFN_FCHUNK = 512
VMEM_LIMIT = 56 * 2**20
LOG2E = 1.4426950408889634
SUB = 8
DMA_UNROLL = 8


def _dot(a, b):
    return jnp.dot(a, b, preferred_element_type=F32)


def _split2(x):
    hi = x.astype(BF16)
    lo = (x - hi.astype(F32)).astype(BF16)
    return hi, lo


def _split3(x):
    p0 = x.astype(BF16)
    r = x - p0.astype(F32)
    p1 = r.astype(BF16)
    p2 = (r - p1.astype(F32)).astype(BF16)
    return p0, p1, p2


def _dot_hi(a, b):
    ah, al = _split2(a)
    bh, bl = _split2(b)
    return _dot(ah, bh) + _dot(al, bh) + _dot(ah, bl)


def _sigmoid(x):
    return 1.0 / (1.0 + jnp.exp(-x))


def _silu(x):
    return x * _sigmoid(x)


def _rms(x, w):
    ms = jnp.mean(x * x, axis=-1, keepdims=True)
    return x * lax.rsqrt(ms + EPS) * w


def _params(sem):
    return pltpu.CompilerParams(dimension_semantics=sem, vmem_limit_bytes=VMEM_LIMIT)


def _resident(shape):
    nd = len(shape)
    return pl.BlockSpec(shape, lambda *_: (0,) * nd, pipeline_mode=pl.Buffered(1))


def _block_row(nblk, nctx_blk, nbatch, blk=None):
    if blk is None:
        blk = pl.program_id(0)
    b = blk // nblk
    j = blk - b * nblk
    is_ctx = j < nctx_blk
    return jnp.where(is_ctx, nbatch, b), is_ctx, j


def _mod_kernel(c_ref, w_ref, b_ref, o_ref):
    o_ref[0] = _dot_hi(_silu(c_ref[...]), w_ref[0]) + b_ref[0]


def _mod_table(cvec, ada_w, ada_b):
    depth, d, n = ada_w.shape
    tn = n // 4
    return pl.pallas_call(
        _mod_kernel,
        grid=(depth, n // tn),
        in_specs=[
            pl.BlockSpec((8, d), lambda l, j: (0, 0)),
            pl.BlockSpec((1, d, tn), lambda l, j: (l, 0, j)),
            pl.BlockSpec((1, 1, tn), lambda l, j: (l, 0, j)),
        ],
        out_specs=pl.BlockSpec((1, 8, tn), lambda l, j: (l, 0, j)),
        out_shape=jax.ShapeDtypeStruct((depth, 8, n), F32),
        compiler_params=_params(("arbitrary", "arbitrary")),
        name="mod_table",
    )(cvec, ada_w, ada_b.reshape(depth, 1, n))


def _even_kernel(nblk, nctx_blk, nbatch, d, from_inputs, *refs):
    n_h = 2 * TILE_BLKS if from_inputs else 1
    h_refs = refs[:n_h]
    tbl_ref, nw1_ref, nw2_ref, acat_ref, pw_ref, ps_ref, w1_ref, w3_ref, w2_ref, o_ref = refs[n_h:]

    def mods(sb):
        row, is_ctx, _ = _block_row(nblk, nctx_blk, nbatch, pl.program_id(0) * TILE_BLKS + sb)
        return (lambda k: tbl_ref[0, pl.ds(row, 1), k * d:(k + 1) * d]), is_ctx

    def mixer(sb):
        mod, is_ctx = mods(sb)
        kind = is_ctx.astype(jnp.int32)
        if from_inputs:
            h = jnp.where(is_ctx, h_refs[2 * sb][0], h_refs[2 * sb + 1][0])
        else:
            h = h_refs[0][sb * BLK:(sb + 1) * BLK, :]
        u = _rms(h, nw1_ref[0]) * (1.0 + mod(1)) + mod(0)
        gw = d // len(POOL_WINDOWS)
        ys = []
        for g in range(len(POOL_WINDOWS)):
            ug = u[:, g * gw:(g + 1) * gw]
            uh, ul = _split2(ug)
            p = _dot(acat_ref[kind, g], jnp.concatenate([uh, uh, ul], axis=0)) - ug
            ys.append(_dot(p.astype(BF16), pw_ref[0, g]))
        y = jnp.concatenate(ys, axis=1) * ps_ref[0]
        h1 = h + mod(2) * y
        v = (_rms(h1, nw2_ref[0]) * (1.0 + mod(4)) + mod(3)).astype(BF16)
        return h1, v

    dff = w1_ref.shape[-1]
    chunks = [slice(c, min(c + FFN_FCHUNK, dff)) for c in range(0, dff, FFN_FCHUNK)]
    n_f = len(chunks)

    def up(v, k):
        return _dot(v, w1_ref[0, :, chunks[k]]), _dot(v, w3_ref[0, :, chunks[k]])

    def down(ab, k):
        act = (_silu(ab[0]) * ab[1]).astype(BF16)
        return _dot(act, w2_ref[0, chunks[k], :])

    mixed = [mixer(sb) for sb in range(TILE_BLKS)]
    v = jnp.concatenate([m[1] for m in mixed], axis=0)
    ab = up(v, 0)
    acc = None
    for k in range(n_f):
        ab_next = up(v, k + 1) if k + 1 < n_f else None
        part = down(ab, k)
        acc = part if acc is None else acc + part
        ab = ab_next
    for sb in range(TILE_BLKS):
        mod, _ = mods(sb)
        o_ref[sb * BLK:(sb + 1) * BLK, :] = mixed[sb][0] + mod(5) * acc[sb * BLK:(sb + 1) * BLK, :]


def _pool_matrices():
    mats = np.zeros((2, len(POOL_WINDOWS), BLK, BLK), np.float64)
    for kind, seg in enumerate((GRID_W, BLK)):
        for g, w in enumerate(POOL_WINDOWS):
            lo = w // 2
            hi = w - 1 - lo
            for t in range(BLK):
                base = (t // seg) * seg
                tt = t - base
                start = max(tt - lo, 0)
                end = min(tt + hi + 1, seg)
                mats[kind, g, t, base + start:base + end] = 1.0 / (end - start)
    m32 = jnp.asarray(mats, F32)
    hi = m32.astype(BF16)
    lo = (m32 - hi.astype(F32)).astype(BF16)
    return jnp.concatenate([hi, lo, hi], axis=-1)


def _even_layer(h, tbl, layer, nw1, nw2, acat, pw, ps, w1, w3, w2, geom):
    nblk, nctx_blk, nbatch = geom
    from_inputs = isinstance(h, tuple)
    d = h[0].shape[-1]
    nt = nbatch * nblk * BLK
    dff = w1.shape[-1]
    j = layer // 2
    gw = d // len(POOL_WINDOWS)
    kern = functools.partial(_even_kernel, nblk, nctx_blk, nbatch, d, from_inputs)
    vec = lambda idx: pl.BlockSpec((1, 1, d), lambda i: (idx, 0, 0))
    tm = TILE_BLKS * BLK
    if from_inputs:
        def src_spec(sb, latent):
            def index(i):
                g = i * TILE_BLKS + sb
                b = g // nblk
                jb = g - b * nblk
                blk = jnp.maximum(jb - nctx_blk, 0) if latent else jnp.minimum(jb, nctx_blk - 1)
                return (b, blk, 0)
            return pl.BlockSpec((1, BLK, d), index)
        h_specs = [src_spec(sb, latent) for sb in range(TILE_BLKS) for latent in (False, True)]
        h_args = [h[0], h[1]] * TILE_BLKS
    else:
        h_specs = [pl.BlockSpec((tm, d), lambda i: (i, 0))]
        h_args = [h]
    return pl.pallas_call(
        kern,
        grid=(nt // tm,),
        in_specs=h_specs + [
            pl.BlockSpec((1, 8, N_MOD * d), lambda i: (layer, 0, 0)),
            vec(layer), vec(layer),
            _resident(acat.shape),
            pl.BlockSpec((1, len(POOL_WINDOWS), gw, gw), lambda i: (j, 0, 0, 0)),
            vec(j),
            pl.BlockSpec((1, d, dff), lambda i: (j, 0, 0), pipeline_mode=pl.Buffered(1)),
            pl.BlockSpec((1, d, dff), lambda i: (j, 0, 0), pipeline_mode=pl.Buffered(1)),
            pl.BlockSpec((1, dff, d), lambda i: (j, 0, 0), pipeline_mode=pl.Buffered(1)),
        ],
        out_specs=pl.BlockSpec((tm, d), lambda i: (i, 0)),
        out_shape=jax.ShapeDtypeStruct((nt, d), F32),
        compiler_params=_params(("arbitrary",)),
        name=f"pool_ffn_{layer}",
    )(*h_args, tbl, nw1, nw2, acat, pw, ps, w1, w3, w2)


def _ssd_in_kernel(nblk, nctx_blk, nbatch, d, d_inner,
                   hp_ref, h_ref, hn_ref, tbl_ref, nw_ref, wz_ref, wx_ref, wdf_ref, wdb_ref,
                   cw_ref, cb_ref, dtb_ref,
                   z_ref, xs_ref, bc_ref, dtf_ref, dtb_out_ref, *xbc_scrs):
    for sb in range(TILE_BLKS):
        rows = slice(sb * BLK, (sb + 1) * BLK)
        prev_rows = hp_ref[...] if sb == 0 else h_ref[sb * BLK - HALO:sb * BLK, :]
        next_rows = hn_ref[...] if sb == TILE_BLKS - 1 else h_ref[(sb + 1) * BLK:(sb + 1) * BLK + HALO, :]
        _ssd_in_block(nblk, nctx_blk, nbatch, d, d_inner, pl.program_id(0) * TILE_BLKS + sb,
                      prev_rows, h_ref[rows, :], next_rows, tbl_ref, nw_ref, wz_ref, wx_ref, wdf_ref, wdb_ref,
                      cw_ref, cb_ref, dtb_ref, z_ref, xs_ref, bc_ref, dtf_ref, dtb_out_ref, xbc_scrs[sb], rows)


def _ssd_in_block(nblk, nctx_blk, nbatch, d, d_inner, blk, h_prev, h_blk, h_next,
                  tbl_ref, nw_ref, wz_ref, wx_ref, wdf_ref, wdb_ref, cw_ref, cb_ref, dtb_ref,
                  z_ref, xs_ref, bc_ref, dtf_ref, dtb_out_ref, xbc_scr, rows):
    row, _, j = _block_row(nblk, nctx_blk, nbatch, blk)
    first = jnp.logical_or(j == 0, j == nctx_blk)
    last = jnp.logical_or(j == nctx_blk - 1, j == nblk - 1)
    shift = tbl_ref[0, pl.ds(row, 1), 0:d]
    scale = tbl_ref[0, pl.ds(row, 1), d:2 * d]
    nw = nw_ref[0]

    def modn(x):
        return _rms(x, nw) * (1.0 + scale) + shift

    uf = modn(h_blk)
    u = uf.astype(BF16)
    up = modn(h_prev) * jnp.where(first, 0.0, 1.0)
    un = modn(h_next) * jnp.where(last, 0.0, 1.0)
    u_ext = jnp.concatenate([up, uf, un], axis=0).astype(BF16)
    nslab = xbc_scr.shape[0]
    xs_slabs = d_inner // 128
    per = CONV_SLABS

    def project(c0):
        val = _dot(u_ext, wx_ref[0, :, c0 * 128:(c0 + per) * 128])
        for c in range(c0, c0 + per):
            xbc_scr[c, :, :] = val[:, (c - c0) * 128:(c - c0 + 1) * 128]

    def conv(c0):
        for c in range(c0, c0 + per):
            lanes = slice(c * 128, (c + 1) * 128)
            acc = cb_ref[0, :, lanes]
            for k in range(D_CONV):
                off = HALO - CONV_LEFT + k
                acc = acc + xbc_scr[c, off:off + BLK, :] * cw_ref[0, k:k + 1, lanes]
            y = _silu(acc).astype(BF16)
            if c < xs_slabs:
                xs_ref[rows, lanes] = y
            else:
                bc_ref[rows, (c - xs_slabs) * 128:(c - xs_slabs + 1) * 128] = y

    def softplus(x):
        return jnp.maximum(x, 0.0) + jnp.log1p(jnp.exp(-jnp.abs(x)))

    zw = per * 128
    z_pieces = d_inner // zw
    project(0)
    for i, c0 in enumerate(range(0, nslab, per)):
        if c0 + per < nslab:
            project(c0 + per)
        if i < z_pieces:
            z_ref[rows, i * zw:(i + 1) * zw] = _dot(u, wz_ref[0, :, i * zw:(i + 1) * zw]).astype(BF16)
        if i == z_pieces:
            dtf_ref[rows, :] = softplus(_dot(u, wdf_ref[0]) + dtb_ref[0, 0:1, :])
            dtb_out_ref[rows, :] = softplus(_dot(u, wdb_ref[0]) + dtb_ref[0, 1:2, :])
        conv(c0)


def _ssd_in(h, tbl, layer, nw, wz, wx, wdf, wdb, cw, cb, dtb, geom):
    nt, d = h.shape
    nblk, nctx_blk, nbatch = geom
    j = layer // 2
    d_inner = wz.shape[-1]
    conv_dim = wx.shape[-1]
    tm = TILE_BLKS * BLK
    hb = tm // HALO
    nh = nt // HALO
    kern = functools.partial(_ssd_in_kernel, nblk, nctx_blk, nbatch, d, d_inner)
    res3 = lambda a: pl.BlockSpec((1,) + a.shape[1:], lambda i: (j, 0, 0), pipeline_mode=pl.Buffered(1))
    return pl.pallas_call(
        kern,
        grid=(nt // tm,),
        in_specs=[
            pl.BlockSpec((HALO, d), lambda i: (jnp.maximum(i * hb - 1, 0), 0)),
            pl.BlockSpec((tm, d), lambda i: (i, 0)),
            pl.BlockSpec((HALO, d), lambda i: (jnp.minimum((i + 1) * hb, nh - 1), 0)),
            pl.BlockSpec((1, 8, N_MOD * d), lambda i: (layer, 0, 0)),
            pl.BlockSpec((1, 1, d), lambda i: (layer, 0, 0)),
            res3(wz), res3(wx), res3(wdf), res3(wdb),
            pl.BlockSpec((1, D_CONV, conv_dim), lambda i: (j, 0, 0)),
            pl.BlockSpec((1, 1, conv_dim), lambda i: (j, 0, 0)),
            pl.BlockSpec((1, 2, CHUNK), lambda i: (j, 0, 0)),
        ],
        out_specs=[
            pl.BlockSpec((tm, d_inner), lambda i: (i, 0)),
            pl.BlockSpec((tm, d_inner), lambda i: (i, 0)),
            pl.BlockSpec((tm, conv_dim - d_inner), lambda i: (i, 0)),
            pl.BlockSpec((tm, CHUNK), lambda i: (i, 0)),
            pl.BlockSpec((tm, CHUNK), lambda i: (i, 0)),
        ],
        out_shape=[
            jax.ShapeDtypeStruct((nt, d_inner), BF16),
            jax.ShapeDtypeStruct((nt, d_inner), BF16),
            jax.ShapeDtypeStruct((nt, conv_dim - d_inner), BF16),
            jax.ShapeDtypeStruct((nt, CHUNK), F32),
            jax.ShapeDtypeStruct((nt, CHUNK), F32),
        ],
        scratch_shapes=[pltpu.VMEM((conv_dim // 128, BLK + 2 * HALO, 128), F32)] * TILE_BLKS,
        compiler_params=_params(("arbitrary",)),
        name=f"ssd_in_{layer}",
    )(h, h, h, tbl, nw, wz, wx, wdf, wdb, cw, cb, dtb)


def _scan_prep(dt, alog, tri, reverse):
    t = dt.shape[0]
    lane = lax.broadcasted_iota(jnp.int32, (1, CHUNK), 1)
    a_row = jnp.where(lane < N_HEADS, -jnp.exp(alog), 0.0)
    a = dt * (a_row * LOG2E)
    p0, p1, p2 = _split3(a)
    cum = _dot(tri, p0) + _dot(tri, p1) + _dot(tri, p2)
    tot = cum[0:1, :] if reverse else cum[t - 1:t, :]
    dte = jnp.exp2(tot - cum)
    ecum = jnp.exp2(cum)
    cdec = jnp.exp2(tot)
    row_t = (cum - jnp.log2(dt)).T

    q0, q1, q2 = _split3(cum)
    stacked = (q0.astype(F32) + pltpu.roll(q1.astype(F32), N_HEADS, 1)
               + pltpu.roll(q2.astype(F32), 2 * N_HEADS, 1)).astype(BF16)
    li = lax.broadcasted_iota(jnp.int32, (t, t), 0)
    si = lax.broadcasted_iota(jnp.int32, (t, t), 1)
    keep = (si >= li) if reverse else (si <= li)
    return dict(wdt=(dt * dte).astype(BF16), ecum=ecum.astype(BF16),
                cdec=_split3(jnp.broadcast_to(cdec, (8, CHUNK))), stacked=stacked, row_t=row_t, keep=keep)


def _scan_expand(g, preps, e_ref):
    heads = N_HEADS // N_GROUPS
    e = e_ref[:, g * heads * HEAD_DIM:(g + 1) * heads * HEAD_DIM]
    stack = lambda key: jnp.concatenate([p[key] for p in preps], axis=0)
    w_x = _dot(stack("wdt"), e).astype(BF16)
    ec_x = _dot(stack("ecum"), e)
    cd_x = sum(_dot(jnp.concatenate([p["cdec"][i] for p in preps], axis=0), e) for i in range(3))
    rows = lambda a, n: a[n * CHUNK:(n + 1) * CHUNK, :]
    return [dict(w_x=rows(w_x, n), ec_x=rows(ec_x, n), cd_x=cd_x[8 * n:8 * n + 1, :])
            for n in range(len(preps))]


def _scan_cb(g, bc_ref, rows):
    gn = N_GROUPS * D_STATE
    b_g = bc_ref[rows, g * D_STATE:(g + 1) * D_STATE]
    c_g = bc_ref[rows, gn + g * D_STATE:gn + (g + 1) * D_STATE]
    return lax.dot_general(c_g, b_g, (((1,), (1,)), ((), ())), preferred_element_type=F32)


def _scan_decay_pair(g, q, prep, cb, e3_ref):
    heads = N_HEADS // N_GROUPS
    h0 = g * heads + 2 * q
    colb = _dot(prep["stacked"], e3_ref[:, h0 * CHUNK:(h0 + 2) * CHUNK])
    ms = []
    for i in range(2):
        seg = colb[:, i * CHUNK:(i + 1) * CHUNK] - prep["row_t"][h0 + i:h0 + i + 1, :]
        lmat = jnp.exp2(jnp.where(prep["keep"], seg, -1e30))
        ms.append((cb * lmat).astype(BF16))
    return jnp.concatenate(ms, axis=1)


def _scan_pair_dot(g, q, m_pair, x_ref, rows):
    heads = N_HEADS // N_GROUPS
    first_head = lax.broadcasted_iota(jnp.int32, (CHUNK, 2 * HEAD_DIM), 1) < HEAD_DIM
    x_pair = x_ref[rows, (g * heads + 2 * q) * HEAD_DIM:(g * heads + 2 * q + 2) * HEAD_DIM]
    zero = jnp.zeros_like(x_pair)
    rhs = jnp.concatenate([jnp.where(first_head, x_pair, zero),
                           jnp.where(first_head, zero, x_pair)], axis=0)
    return _dot(m_pair, rhs)


def _scan_finish(g, ydiag, ex, x_ref, bc_ref, s_ref, y_ref, dskip, rows):
    heads = N_HEADS // N_GROUPS
    gn = N_GROUPS * D_STATE
    gp = heads * HEAD_DIM
    b_g = bc_ref[rows, g * D_STATE:(g + 1) * D_STATE]
    c_g = bc_ref[rows, gn + g * D_STATE:gn + (g + 1) * D_STATE]
    sl = slice(g * gp, (g + 1) * gp)
    xg = x_ref[rows, sl]
    s_old = s_ref[:, sl]
    y_g = jnp.concatenate(ydiag, axis=1) + _dot(c_g, s_old.astype(BF16)) * ex["ec_x"]
    if dskip is not None:
        y_g = y_g + xg.astype(F32) * dskip[:, sl]
    y_ref[rows, sl] = y_g.astype(y_ref.dtype)
    s_new = lax.dot_general(b_g, xg * ex["w_x"], (((0,), (0,)), ((), ())), preferred_element_type=F32)
    s_ref[:, sl] = s_old * ex["cd_x"] + s_new


def _ssd_scan_kernel(xf_ref, bcf_ref, dtf_ref, xb_ref, bcb_ref, dtb_ref, alog_ref, dsk_ref,
                     tril_ref, triu_ref, e_ref, e3_ref, yf_ref, yb_ref, sf_ref, sb_ref):
    @pl.when(pl.program_id(1) == 0)
    def _():
        sf_ref[...] = jnp.zeros_like(sf_ref)
        sb_ref[...] = jnp.zeros_like(sb_ref)

    rows_f = [slice(n * CHUNK, (n + 1) * CHUNK) for n in range(SCAN_CHUNKS)]
    rows_b = rows_f[::-1]
    pf = [_scan_prep(dtf_ref[r, :], alog_ref[0, 0:1, :], tril_ref[...], False) for r in rows_f]
    pb = [_scan_prep(dtb_ref[r, :], alog_ref[0, 1:2, :], triu_ref[...], True) for r in rows_b]
    cb_of = lambda g: ([_scan_cb(g, bcf_ref, r) for r in rows_f], [_scan_cb(g, bcb_ref, r) for r in rows_b])
    cbs = {0: cb_of(0)}
    dirs = ((pf, xf_ref, bcf_ref, sf_ref, yf_ref, dsk_ref[0], rows_f),
            (pb, xb_ref, bcb_ref, sb_ref, yb_ref, None, rows_b))
    pairs = N_HEADS // N_GROUPS // 2
    items = [(g, n, dr, q) for g in range(N_GROUPS) for n in range(SCAN_CHUNKS) for dr in (0, 1)
             for q in range(pairs)]

    def decay(item):
        g, n, dr, q = item
        return _scan_decay_pair(g, q, dirs[dr][0][n], cbs[g][dr][n], e3_ref)

    ex = [_scan_expand(0, pf, e_ref), _scan_expand(0, pb, e_ref)]
    ex_next = None
    m_pair = decay(items[0])
    ydiag = []
    for i, (g, n, dr, q) in enumerate(items):
        prep, x_ref, bc_ref, s_ref, y_ref, dskip, rows = dirs[dr]
        if (n, dr, q) == (0, 0, 0) and g + 1 < N_GROUPS:
            cbs[g + 1] = cb_of(g + 1)
            ex_next = [_scan_expand(g + 1, pf, e_ref), _scan_expand(g + 1, pb, e_ref)]
        m_next = decay(items[i + 1]) if i + 1 < len(items) else None
        ydiag.append(_scan_pair_dot(g, q, m_pair, x_ref, rows[n]))
        m_pair = m_next
        if q == pairs - 1:
            _scan_finish(g, ydiag, ex[dr][n], x_ref, bc_ref, s_ref, y_ref, dskip, rows[n])
            ydiag = []
            if (n, dr) == (SCAN_CHUNKS - 1, 1):
                ex = ex_next


def _scan_constants():
    li = np.arange(CHUNK)[:, None]
    ti = np.arange(CHUNK)[None, :]
    tril = (ti <= li).astype(np.float32)
    triu = (ti >= li).astype(np.float32)
    e = np.zeros((CHUNK, N_HEADS * HEAD_DIM), np.float32)
    e3 = np.zeros((CHUNK, N_HEADS * CHUNK), np.float32)
    for h in range(N_HEADS):
        e[h, h * HEAD_DIM:(h + 1) * HEAD_DIM] = 1.0
        for piece in range(3):
            e3[piece * N_HEADS + h, h * CHUNK:(h + 1) * CHUNK] = 1.0
    return tuple(jnp.asarray(m, BF16) for m in (tril, triu, e, e3))


def _ssd_scan(xs, bc, dtf, dtb, alog, dskip, layer, consts, nbatch, nchunk, ncc):
    nt, d_inner = xs.shape
    bcw = bc.shape[1]
    j = layer // 2
    tril, triu, e, e3 = consts

    assert nchunk % SCAN_CHUNKS == 0 and ncc % SCAN_CHUNKS == 0
    nstep = nchunk // SCAN_CHUNKS
    ncs = ncc // SCAN_CHUNKS
    rows = SCAN_CHUNKS * CHUNK

    def fwd(b, c):
        return (b * nstep + c, 0)

    def bwd(b, c):
        return (b * nstep + jnp.where(c < ncs, ncs - 1 - c, nstep - 1 - (c - ncs)), 0)

    return pl.pallas_call(
        _ssd_scan_kernel,
        grid=(nbatch, nstep),
        in_specs=[
            pl.BlockSpec((rows, d_inner), fwd), pl.BlockSpec((rows, bcw), fwd), pl.BlockSpec((rows, CHUNK), fwd),
            pl.BlockSpec((rows, d_inner), bwd), pl.BlockSpec((rows, bcw), bwd), pl.BlockSpec((rows, CHUNK), bwd),
            pl.BlockSpec((1, 2, CHUNK), lambda b, c: (j, 0, 0)),
            pl.BlockSpec((1, 1, d_inner), lambda b, c: (j, 0, 0)),
            _resident(tril.shape), _resident(triu.shape), _resident(e.shape), _resident(e3.shape),
        ],
        out_specs=[pl.BlockSpec((rows, d_inner), fwd), pl.BlockSpec((rows, d_inner), bwd)],
        out_shape=[jax.ShapeDtypeStruct((nt, d_inner), BF16)] * 2,
        scratch_shapes=[pltpu.VMEM((D_STATE, d_inner), F32)] * 2,
        compiler_params=_params(("arbitrary", "arbitrary")),
        name=f"ssd_scan_{layer}",
    )(xs, bc, dtf, xs, bc, dtb, alog, dskip, tril, triu, e, e3)


def _ssd_out_kernel(nblk, nctx_blk, nbatch, d,
                    yf_ref, yb_ref, z_ref, h_ref, tbl_ref, nw_ref, wo_ref, o_ref):
    y = (yf_ref[...].astype(F32) + yb_ref[...].astype(F32)) * _silu(z_ref[...].astype(F32))
    proj = _dot(_rms(y, nw_ref[0]).astype(BF16), wo_ref[0])
    for sb in range(TILE_BLKS):
        rows = slice(sb * BLK, (sb + 1) * BLK)
        row, _, _ = _block_row(nblk, nctx_blk, nbatch, pl.program_id(0) * TILE_BLKS + sb)
        gate = tbl_ref[0, pl.ds(row, 1), 2 * d:3 * d]
        o_ref[rows, :] = h_ref[rows, :] + gate * proj[rows, :]


def _ssd_out(yf, yb, z, h, tbl, layer, nw, wo, geom):
    nt, d = h.shape
    d_inner = z.shape[1]
    nblk, nctx_blk, nbatch = geom
    j = layer // 2
    kern = functools.partial(_ssd_out_kernel, nblk, nctx_blk, nbatch, d)
    tm = TILE_BLKS * BLK
    big = pl.BlockSpec((tm, d_inner), lambda i: (i, 0))
    return pl.pallas_call(
        kern,
        grid=(nt // tm,),
        in_specs=[
            big, big, big,
            pl.BlockSpec((tm, d), lambda i: (i, 0)),
            pl.BlockSpec((1, 8, N_MOD * d), lambda i: (layer, 0, 0)),
            pl.BlockSpec((1, 1, d_inner), lambda i: (j, 0, 0)),
            pl.BlockSpec((1, d_inner, d), lambda i: (j, 0, 0), pipeline_mode=pl.Buffered(1)),
        ],
        out_specs=pl.BlockSpec((tm, d), lambda i: (i, 0)),
        out_shape=jax.ShapeDtypeStruct((nt, d), F32),
        compiler_params=_params(("arbitrary",)),
        name=f"ssd_out_{layer}",
    )(yf, yb, z, h, tbl, nw, wo)


def _route_kernel(nblk, nctx_blk, nbatch, d,
                  h_ref, tbl_ref, nw_ref, rw_ref, sl_ref, info_ref, cnt_ref, carry_ref):
    @pl.when(pl.program_id(0) == 0)
    def _():
        carry_ref[...] = jnp.zeros_like(carry_ref)

    row, _, _ = _block_row(nblk, nctx_blk, nbatch)
    shift = tbl_ref[0, pl.ds(row, 1), 3 * d:4 * d]
    scale = tbl_ref[0, pl.ds(row, 1), 4 * d:5 * d]
    v = _rms(h_ref[...], nw_ref[0]) * (1.0 + scale) + shift
    lane = lax.broadcasted_iota(jnp.int32, (BLK, CHUNK), 1).astype(F32)
    logits = jnp.where(lane < N_EXPERTS, _dot_hi(v, rw_ref[0]), -jnp.inf)
    m1 = jnp.max(logits, axis=1, keepdims=True)
    i1 = jnp.min(jnp.where(logits == m1, lane, float(CHUNK)), axis=1, keepdims=True)
    rest = jnp.where(lane == i1, -jnp.inf, logits)
    m2 = jnp.max(rest, axis=1, keepdims=True)
    i2 = jnp.min(jnp.where(rest == m2, lane, float(CHUNK)), axis=1, keepdims=True)
    e2 = jnp.exp(m2 - m1)
    g1 = 1.0 / (1.0 + e2)
    g2 = e2 / (1.0 + e2)
    oh1 = (lane == i1)
    oh2 = (lane == i2)
    member = jnp.where(jnp.logical_or(oh1, oh2), 1.0, 0.0)
    before = carry_ref[...] + _dot(sl_ref[...], member.astype(BF16))
    r1 = jnp.sum(jnp.where(oh1, before, 0.0), axis=1, keepdims=True)
    r2 = jnp.sum(jnp.where(oh2, before, 0.0), axis=1, keepdims=True)
    total = carry_ref[...] + jnp.sum(member, axis=0, keepdims=True)
    carry_ref[...] = total
    cnt_ref[...] = jnp.broadcast_to(total, cnt_ref.shape)
    lane8 = lax.broadcasted_iota(jnp.int32, (BLK, 8), 1)
    info = jnp.where(lane8 == 0, i1,
           jnp.where(lane8 == 1, i2,
           jnp.where(lane8 == 2, r1,
           jnp.where(lane8 == 3, r2,
           jnp.where(lane8 == 4, g1,
           jnp.where(lane8 == 5, g2, 0.0))))))
    info_ref[...] = info


def _route(h, tbl, layer, nw, rw, strict_lower, geom):
    nt, d = h.shape
    nblk, nctx_blk, nbatch = geom
    j = layer // 2
    kern = functools.partial(_route_kernel, nblk, nctx_blk, nbatch, d)
    return pl.pallas_call(
        kern,
        grid=(nt // BLK,),
        in_specs=[
            pl.BlockSpec((BLK, d), lambda i: (i, 0)),
            pl.BlockSpec((1, 8, N_MOD * d), lambda i: (layer, 0, 0)),
            pl.BlockSpec((1, 1, d), lambda i: (layer, 0, 0)),
            pl.BlockSpec((1, d, CHUNK), lambda i: (j, 0, 0)),
            _resident(strict_lower.shape),
        ],
        out_specs=[
            pl.BlockSpec((BLK, 8), lambda i: (i, 0)),
            pl.BlockSpec((8, CHUNK), lambda i: (0, 0)),
        ],
        out_shape=[jax.ShapeDtypeStruct((nt, 8), F32), jax.ShapeDtypeStruct((8, CHUNK), F32)],
        scratch_shapes=[pltpu.VMEM((1, CHUNK), F32)],
        compiler_params=_params(("arbitrary",)),
        name=f"route_{layer}",
    )(h, tbl, nw, rw, strict_lower)


def _to_tiles(ref, x, rows):
    for k in range(SUB):
        ref[pl.ds(k, rows, stride=SUB), :] = x[:, k * 128:(k + 1) * 128]


def _from_tiles(ref, rows):
    return jnp.concatenate([ref[pl.ds(k, rows, stride=SUB), :] for k in range(SUB)], axis=1)


def _tile_rows(ref, p):
    return ref.at[pl.ds(pl.multiple_of(p * SUB, SUB), SUB), :]


def _dispatch_kernel(nblk, nctx_blk, nbatch, d,
                     ztile_ref, zvalid_ref, pos_ref, h_ref, tbl_ref, nw_ref, xs_ref, v_scr, z_scr, sem, zsem):
    @pl.when(pl.program_id(0) == 0)
    def _():
        z_scr[...] = jnp.zeros_like(z_scr)
        tile_rows = MOE_TILE * SUB
        for e in range(2 * N_EXPERTS):
            @pl.when(zvalid_ref[e] == 1)
            def _():
                first = pl.multiple_of(ztile_ref[e] * tile_rows, SUB)
                pltpu.make_async_copy(z_scr, xs_ref.at[pl.ds(first, tile_rows), :], zsem).start()
        for e in range(2 * N_EXPERTS):
            @pl.when(zvalid_ref[e] == 1)
            def _():
                pltpu.make_async_copy(z_scr, xs_ref.at[pl.ds(0, tile_rows), :], zsem).wait()

    step = pl.program_id(0)
    nsteps = pl.num_programs(0)
    slot = step % 2
    v_slot = v_scr.at[slot]

    def wait_block(s):
        def body(i, c):
            for _ in range(2 * DMA_UNROLL):
                pltpu.make_async_copy(v_scr.at[s, pl.ds(0, SUB), :], xs_ref.at[pl.ds(0, SUB), :], sem.at[s]).wait()
            return c
        lax.fori_loop(0, BLK // DMA_UNROLL, body, 0)

    @pl.when(step >= 2)
    def _():
        wait_block(slot)

    row, _, _ = _block_row(nblk, nctx_blk, nbatch)
    shift = tbl_ref[0, pl.ds(row, 1), 3 * d:4 * d]
    scale = tbl_ref[0, pl.ds(row, 1), 4 * d:5 * d]
    _to_tiles(v_slot, _rms(h_ref[...], nw_ref[0]) * (1.0 + scale) + shift, BLK)

    def start(i, c):
        for u in range(DMA_UNROLL):
            r = i * DMA_UNROLL + u
            for k in range(2):
                pltpu.make_async_copy(_tile_rows(v_slot, r), _tile_rows(xs_ref, pos_ref[0, 0, 2 * r + k]),
                                      sem.at[slot]).start(priority=k)
        return c

    lax.fori_loop(0, BLK // DMA_UNROLL, start, 0)

    @pl.when(step == nsteps - 1)
    def _():
        wait_block(slot)

        @pl.when(nsteps >= 2)
        def _():
            wait_block(1 - slot)


def _dispatch(h, tbl, layer, nw, pos, ztile, zvalid, nslots, geom):
    nt, d = h.shape
    nblk, nctx_blk, nbatch = geom
    kern = functools.partial(_dispatch_kernel, nblk, nctx_blk, nbatch, d)
    gs = pltpu.PrefetchScalarGridSpec(
        num_scalar_prefetch=2,
        grid=(nt // BLK,),
        in_specs=[
            pl.BlockSpec((1, 1, 2 * BLK), lambda i, zt, zv: (i, 0, 0), memory_space=pltpu.SMEM),
            pl.BlockSpec((BLK, d), lambda i, zt, zv: (i, 0)),
            pl.BlockSpec((1, 8, N_MOD * d), lambda i, zt, zv: (layer, 0, 0)),
            pl.BlockSpec((1, 1, d), lambda i, zt, zv: (layer, 0, 0)),
        ],
        out_specs=pl.BlockSpec(memory_space=pl.ANY),
        scratch_shapes=[pltpu.VMEM((2, BLK * SUB, 128), F32), pltpu.VMEM((MOE_TILE * SUB, 128), F32),
                        pltpu.SemaphoreType.DMA((2,)), pltpu.SemaphoreType.DMA(())],
    )
    return pl.pallas_call(
        kern,
        grid_spec=gs,
        out_shape=jax.ShapeDtypeStruct((nslots * SUB, 128), F32),
        compiler_params=_params(("arbitrary",)),
        name=f"dispatch_{layer}",
    )(ztile, zvalid, pos.reshape(nt // BLK, 1, 2 * BLK), h, tbl, nw)


def _expert_kernel(n_fchunk, te_ref, nu_ref, x_ref, w1_ref, w3_ref, w2_ref, o_ref):
    @pl.when(pl.program_id(0) >= nu_ref[0])
    def _():
        o_ref[...] = jnp.zeros_like(o_ref)

    @pl.when(pl.program_id(0) < nu_ref[0])
    def _():
        x = _from_tiles(x_ref, MOE_TILE).astype(BF16)
        acc = None
        for k in range(n_fchunk):
            sl = slice(k * MOE_FCHUNK, (k + 1) * MOE_FCHUNK)
            a = _dot(x, w1_ref[0, 0, :, sl])
            b = _dot(x, w3_ref[0, 0, :, sl])
            act = (_silu(a) * b).astype(BF16)
            part = _dot(act, w2_ref[0, 0, sl, :])
            acc = part if acc is None else acc + part
        _to_tiles(o_ref, acc, MOE_TILE)


def _experts(x_sorted, tile_expert, n_used, j, w1, w3, w2):
    d, dffe = w1.shape[-2:]
    n_tiles = x_sorted.shape[0] // (MOE_TILE * SUB)
    kern = functools.partial(_expert_kernel, dffe // MOE_FCHUNK)
    tile = lambda i, te, nu: (jnp.minimum(i, nu[0] - 1), 0)
    wspec = lambda shape: pl.BlockSpec((1, 1) + shape, lambda i, te, nu: (j, te[i], 0, 0))
    gs = pltpu.PrefetchScalarGridSpec(
        num_scalar_prefetch=2,
        grid=(n_tiles,),
        in_specs=[pl.BlockSpec((MOE_TILE * SUB, 128), tile), wspec((d, dffe)), wspec((d, dffe)), wspec((dffe, d))],
        out_specs=pl.BlockSpec((MOE_TILE * SUB, 128), lambda i, te, nu: (i, 0)),
    )
    return pl.pallas_call(
        kern,
        grid_spec=gs,
        out_shape=jax.ShapeDtypeStruct(x_sorted.shape, F32),
        compiler_params=_params(("arbitrary",)),
        name=f"experts_{j}",
    )(tile_expert, n_used, x_sorted, w1, w3, w2)


def _combine_kernel(row_of_step, d, final,
                    pos_ref, posn_ref, h_ref, info_ref, tbl_ref, y_ref, *rest):
    if final:
        fw_ref, o_ref, buf, sem = rest
    else:
        o_ref, buf, sem = rest
    step = pl.program_id(0)
    nsteps = pl.num_programs(0)
    slot = step % 2

    def gather(p_ref, s):
        def body(i, c):
            for u in range(DMA_UNROLL):
                r = i * DMA_UNROLL + u
                for k in range(2):
                    pltpu.make_async_copy(_tile_rows(y_ref, p_ref[0, 0, 2 * r + k]), _tile_rows(buf.at[s, k], r),
                                          sem.at[s]).start(priority=k)
            return c
        lax.fori_loop(0, BLK // DMA_UNROLL, body, 0)

    @pl.when(step == 0)
    def _():
        gather(pos_ref, 0)

    @pl.when(step + 1 < nsteps)
    def _():
        gather(posn_ref, 1 - slot)

    def wait(i, c):
        for _ in range(2 * DMA_UNROLL):
            pltpu.make_async_copy(y_ref.at[pl.ds(0, SUB), :], buf.at[slot, 0, pl.ds(0, SUB), :], sem.at[slot]).wait()
        return c

    lax.fori_loop(0, BLK // DMA_UNROLL, wait, 0)
    info = info_ref[...]
    g1 = info[:, 4:5]
    g2 = info[:, 5:6]
    gate = tbl_ref[0, pl.ds(row_of_step(step), 1), 5 * d:6 * d]
    out = h_ref[...] + gate * (g1 * _from_tiles(buf.at[slot, 0], BLK) + g2 * _from_tiles(buf.at[slot, 1], BLK))
    if final:
        o_ref[0] = _rms(out, fw_ref[...])
    else:
        o_ref[...] = out


def _combine(h, info, tbl, layer, y_sorted, pos, geom, final_w=None):
    nt, d = h.shape
    nblk, nctx_blk, nbatch = geom
    final = final_w is not None
    pos3 = pos.reshape(nt // BLK, 1, 2 * BLK)
    if final:
        nlat = nblk - nctx_blk
        nsteps = nbatch * nlat
        blk = lambda i: (i // nlat) * nblk + nctx_blk + i % nlat
        row_of_step = lambda step: step // nlat
        out_spec = pl.BlockSpec((1, BLK, d), lambda i: (i // nlat, i % nlat, 0))
        out_shape = jax.ShapeDtypeStruct((nbatch, nlat * BLK, d), F32)
    else:
        nsteps = nt // BLK
        blk = lambda i: i
        row_of_step = lambda step: _block_row(nblk, nctx_blk, nbatch, step)[0]
        out_spec = pl.BlockSpec((BLK, d), lambda i: (i, 0))
        out_shape = jax.ShapeDtypeStruct((nt, d), F32)
    nxt = lambda i: blk(jnp.minimum(i + 1, nsteps - 1))
    in_specs = [
        pl.BlockSpec((1, 1, 2 * BLK), lambda i: (blk(i), 0, 0), memory_space=pltpu.SMEM),
        pl.BlockSpec((1, 1, 2 * BLK), lambda i: (nxt(i), 0, 0), memory_space=pltpu.SMEM),
        pl.BlockSpec((BLK, d), lambda i: (blk(i), 0)),
        pl.BlockSpec((BLK, 8), lambda i: (blk(i), 0)),
        pl.BlockSpec((1, 8, N_MOD * d), lambda i: (layer, 0, 0)),
        pl.BlockSpec(memory_space=pl.ANY),
    ]
    args = [pos3, pos3, h, info, tbl, y_sorted]
    if final:
        in_specs.append(pl.BlockSpec((1, d), lambda i: (0, 0)))
        args.append(final_w.reshape(1, d))
    return pl.pallas_call(
        functools.partial(_combine_kernel, row_of_step, d, final),
        grid=(nsteps,),
        in_specs=in_specs,
        out_specs=out_spec,
        out_shape=out_shape,
        scratch_shapes=[pltpu.VMEM((2, 2, BLK * SUB, 128), F32), pltpu.SemaphoreType.DMA((2,))],
        compiler_params=_params(("arbitrary",)),
        name=f"combine_{layer}",
    )(*args)


def _moe_layer(h, tbl, layer, nw, rw, strict_lower, w1, w3, w2, geom, final_w=None):
    nt, d = h.shape
    j = layer // 2
    info, counts = _route(h, tbl, layer, nw, rw, strict_lower, geom)
    cnt = counts[0, :N_EXPERTS].astype(jnp.int32)
    tiles = (cnt + MOE_TILE - 1) // MOE_TILE
    tile_end = jnp.cumsum(tiles)
    offs = (tile_end - tiles) * MOE_TILE
    n_used = tile_end[-1]
    idx = info[:, 0:2].astype(jnp.int32)
    group_start = jnp.sum(jnp.where(idx[..., None] == jnp.arange(N_EXPERTS), offs, 0), axis=-1)
    pos = (group_start + info[:, 2:4].astype(jnp.int32)).reshape(-1)
    n_tiles = (2 * nt) // MOE_TILE + N_EXPERTS
    t = jnp.minimum(jnp.arange(n_tiles, dtype=jnp.int32), n_used - 1)
    tile_expert = jnp.sum(t[:, None] >= tile_end[None, :], axis=1).astype(jnp.int32)
    spare = n_used + jnp.arange(N_EXPERTS, dtype=jnp.int32)
    ztile = jnp.concatenate([tile_end - 1, spare]).astype(jnp.int32)
    zvalid = jnp.concatenate([tiles > 0, spare < n_tiles]).astype(jnp.int32)
    x_sorted = _dispatch(h, tbl, layer, nw, pos, ztile, zvalid, n_tiles * MOE_TILE, geom)
    y_sorted = _experts(x_sorted, tile_expert, n_used.reshape(1).astype(jnp.int32), j, w1, w3, w2)
    return _combine(h, info, tbl, layer, y_sorted, pos, geom, final_w)


def _final_kernel(h_ref, w_ref, o_ref):
    o_ref[0] = _rms(h_ref[...], w_ref[...])


def _final_norm(h, w, nbatch, nblk, nctx_blk):
    nt, d = h.shape
    nlat = nblk - nctx_blk
    return pl.pallas_call(
        _final_kernel,
        grid=(nbatch, nlat),
        in_specs=[
            pl.BlockSpec((BLK, d), lambda b, j: (b * nblk + nctx_blk + j, 0)),
            pl.BlockSpec((1, d), lambda b, j: (0, 0)),
        ],
        out_specs=pl.BlockSpec((1, BLK, d), lambda b, j: (b, j, 0)),
        out_shape=jax.ShapeDtypeStruct((nbatch, nlat * BLK, d), F32),
        compiler_params=_params(("arbitrary", "arbitrary")),
        name="final_norm",
    )(h, w.reshape(1, d))


def kernel(x, c, ctx, c_ctx, ada_w, ada_b, norm_mix_w, norm_ffn_w, pool_w, pool_scale, ssd_in_w, ssd_conv_w, ssd_conv_b, ssd_A_log, ssd_dt_bias, ssd_D, ssd_norm_w, ssd_out_w, ffn_w1, ffn_w3, ffn_w2, moe_router_w, moe_w1, moe_w3, moe_w2, final_norm_w):
    nbatch, seq, d = x.shape
    ctx_len = ctx.shape[1]
    depth = ada_w.shape[0]
    d_inner = ssd_norm_w.shape[-1]
    assert ctx_len % BLK == 0 and seq % BLK == 0 and nbatch < 8
    assert d_inner == N_HEADS * HEAD_DIM and d % len(POOL_WINDOWS) == 0
    nblk = (ctx_len + seq) // BLK
    nctx_blk = ctx_len // BLK
    geom = (nblk, nctx_blk, nbatch)
    nchunk = (ctx_len + seq) // CHUNK
    ncc = ctx_len // CHUNK

    h = (ctx, x)
    cvec = jnp.zeros((8, d), F32).at[:nbatch].set(c).at[nbatch].set(c_ctx)
    tbl = _mod_table(cvec, ada_w, ada_b)

    vec3 = lambda a: a.reshape(a.shape[0], 1, a.shape[-1])
    nmix = vec3(norm_mix_w)
    nffn = vec3(norm_ffn_w)
    acat = _pool_matrices()
    scan_consts = _scan_constants()
    strict_lower = jnp.asarray(np.tril(np.ones((BLK, BLK), np.float32), -1), BF16)

    conv_dim = ssd_conv_w.shape[-1]
    wz = ssd_in_w[:, :, :d_inner].astype(BF16)
    wx = ssd_in_w[:, :, d_inner:d_inner + conv_dim].astype(BF16)
    wdt = ssd_in_w[:, :, d_inner + conv_dim:]
    pad_dt = lambda w: jnp.pad(w, ((0, 0), (0, 0), (0, CHUNK - N_HEADS))).astype(BF16)
    wdf = pad_dt(wdt[:, :, :N_HEADS])
    wdb = pad_dt(wdt[:, :, N_HEADS:])
    pad_h = lambda a: jnp.pad(a, ((0, 0), (0, 0), (0, CHUNK - N_HEADS)))
    dtb = pad_h(ssd_dt_bias)
    alog = pad_h(ssd_A_log)
    dskip = vec3(jnp.repeat(ssd_D, HEAD_DIM, axis=-1))
    rw = jnp.pad(moe_router_w, ((0, 0), (0, 0), (0, CHUNK - N_EXPERTS)))
    pool_wb = pool_w.astype(BF16)
    ffn = [w.astype(BF16) for w in (ffn_w1, ffn_w3, ffn_w2)]
    moe = [w.astype(BF16) for w in (moe_w1, moe_w3, moe_w2)]
    wout = ssd_out_w.astype(BF16)

    for i in range(depth):
        if i % 2 == 0:
            h = _even_layer(h, tbl, i, nmix, nffn, acat, pool_wb, vec3(pool_scale), *ffn, geom)
        else:
            z, xs, bc, dtf, dtbw = _ssd_in(h, tbl, i, nmix, wz, wx, wdf, wdb, ssd_conv_w,
                                           vec3(ssd_conv_b), dtb, geom)
            yf, yb = _ssd_scan(xs, bc, dtf, dtbw, alog, dskip, i, scan_consts, nbatch, nchunk, ncc)
            h = _ssd_out(yf, yb, z, h, tbl, i, vec3(ssd_norm_w), wout, geom)
            fin = final_norm_w if (i == depth - 1) else None
            h = _moe_layer(h, tbl, i, nffn, rw, strict_lower, *moe, geom, fin)
    if depth % 2 == 1:
        h = _final_norm(h, final_norm_w, nbatch, nblk, nctx_blk)
    return h
```

```python
import functools

import numpy as np
import jax
import jax.numpy as jnp
from jax import lax
from jax.experimental import pallas as pl
from jax.experimental.pallas import tpu as pltpu

F32 = jnp.float32
BF16 = jnp.bfloat16
EPS = 1e-6

BLK = 256
TILE_BLKS = 2
CHUNK = 128
SCAN_CHUNKS = 2
GRID_W = 64
POOL_WINDOWS = (2, 4, 8, 16)
N_MOD = 6
HEAD_DIM = 64
N_HEADS = 32
N_GROUPS = 4
D_STATE = 128
D_CONV = 4
CONV_LEFT = 2
HALO = 8
CONV_SLABS = 2
N_EXPERTS = 8
MOE_TILE = 512
MOE_FCHUNK = 256
FFN_FCHUNK = 512
VMEM_LIMIT = 56 * 2**20
LOG2E = 1.4426950408889634
SUB = 8
DMA_UNROLL = 8


def _dot(a, b):
    return jnp.dot(a, b, preferred_element_type=F32)


def _split2(x):
    hi = x.astype(BF16)
    lo = (x - hi.astype(F32)).astype(BF16)
    return hi, lo


def _split3(x):
    p0 = x.astype(BF16)
    r = x - p0.astype(F32)
    p1 = r.astype(BF16)
    p2 = (r - p1.astype(F32)).astype(BF16)
    return p0, p1, p2


def _dot_hi(a, b):
    ah, al = _split2(a)
    bh, bl = _split2(b)
    return _dot(ah, bh) + _dot(al, bh) + _dot(ah, bl)


def _sigmoid(x):
    return 1.0 / (1.0 + jnp.exp(-x))


def _silu(x):
    return x * _sigmoid(x)


def _rms(x, w):
    ms = jnp.mean(x * x, axis=-1, keepdims=True)
    return x * lax.rsqrt(ms + EPS) * w


def _params(sem):
    return pltpu.CompilerParams(dimension_semantics=sem, vmem_limit_bytes=VMEM_LIMIT)


def _resident(shape):
    nd = len(shape)
    return pl.BlockSpec(shape, lambda *_: (0,) * nd, pipeline_mode=pl.Buffered(1))


def _block_row(nblk, nctx_blk, nbatch, blk=None):
    if blk is None:
        blk = pl.program_id(0)
    b = blk // nblk
    j = blk - b * nblk
    is_ctx = j < nctx_blk
    return jnp.where(is_ctx, nbatch, b), is_ctx, j


def _mod_kernel(c_ref, w_ref, b_ref, o_ref):
    o_ref[0] = _dot_hi(_silu(c_ref[...]), w_ref[0]) + b_ref[0]


def _mod_table(cvec, ada_w, ada_b):
    depth, d, n = ada_w.shape
    tn = n // 4
    return pl.pallas_call(
        _mod_kernel,
        grid=(depth, n // tn),
        in_specs=[
            pl.BlockSpec((8, d), lambda l, j: (0, 0)),
            pl.BlockSpec((1, d, tn), lambda l, j: (l, 0, j)),
            pl.BlockSpec((1, 1, tn), lambda l, j: (l, 0, j)),
        ],
        out_specs=pl.BlockSpec((1, 8, tn), lambda l, j: (l, 0, j)),
        out_shape=jax.ShapeDtypeStruct((depth, 8, n), F32),
        compiler_params=_params(("arbitrary", "arbitrary")),
        name="mod_table",
    )(cvec, ada_w, ada_b.reshape(depth, 1, n))


def _even_kernel(nblk, nctx_blk, nbatch, d, from_inputs, *refs):
    n_h = 2 * TILE_BLKS if from_inputs else 1
    h_refs = refs[:n_h]
    tbl_ref, nw1_ref, nw2_ref, acat_ref, pw_ref, ps_ref, w1_ref, w3_ref, w2_ref, o_ref = refs[n_h:]

    def mods(sb):
        row, is_ctx, _ = _block_row(nblk, nctx_blk, nbatch, pl.program_id(0) * TILE_BLKS + sb)
        return (lambda k: tbl_ref[0, pl.ds(row, 1), k * d:(k + 1) * d]), is_ctx

    def mixer(sb):
        mod, is_ctx = mods(sb)
        kind = is_ctx.astype(jnp.int32)
        if from_inputs:
            h = jnp.where(is_ctx, h_refs[2 * sb][0], h_refs[2 * sb + 1][0])
        else:
            h = h_refs[0][sb * BLK:(sb + 1) * BLK, :]
        u = _rms(h, nw1_ref[0]) * (1.0 + mod(1)) + mod(0)
        gw = d // len(POOL_WINDOWS)
        ys = []
        for g in range(len(POOL_WINDOWS)):
            ug = u[:, g * gw:(g + 1) * gw]
            uh, ul = _split2(ug)
            p = _dot(acat_ref[kind, g], jnp.concatenate([uh, uh, ul], axis=0)) - ug
            ys.append(_dot(p.astype(BF16), pw_ref[0, g]))
        y = jnp.concatenate(ys, axis=1) * ps_ref[0]
        h1 = h + mod(2) * y
        v = (_rms(h1, nw2_ref[0]) * (1.0 + mod(4)) + mod(3)).astype(BF16)
        return h1, v

    dff = w1_ref.shape[-1]
    chunks = [slice(c, min(c + FFN_FCHUNK, dff)) for c in range(0, dff, FFN_FCHUNK)]
    n_f = len(chunks)

    def up(v, k):
        return _dot(v, w1_ref[0, :, chunks[k]]), _dot(v, w3_ref[0, :, chunks[k]])

    def down(ab, k):
        act = (_silu(ab[0]) * ab[1]).astype(BF16)
        return _dot(act, w2_ref[0, chunks[k], :])

    mixed = [mixer(sb) for sb in range(TILE_BLKS)]
    v = jnp.concatenate([m[1] for m in mixed], axis=0)
    ab = up(v, 0)
    acc = None
    for k in range(n_f):
        ab_next = up(v, k + 1) if k + 1 < n_f else None
        part = down(ab, k)
        acc = part if acc is None else acc + part
        ab = ab_next
    for sb in range(TILE_BLKS):
        mod, _ = mods(sb)
        o_ref[sb * BLK:(sb + 1) * BLK, :] = mixed[sb][0] + mod(5) * acc[sb * BLK:(sb + 1) * BLK, :]


def _pool_matrices():
    mats = np.zeros((2, len(POOL_WINDOWS), BLK, BLK), np.float64)
    for kind, seg in enumerate((GRID_W, BLK)):
        for g, w in enumerate(POOL_WINDOWS):
            lo = w // 2
            hi = w - 1 - lo
            for t in range(BLK):
                base = (t // seg) * seg
                tt = t - base
                start = max(tt - lo, 0)
                end = min(tt + hi + 1, seg)
                mats[kind, g, t, base + start:base + end] = 1.0 / (end - start)
    m32 = jnp.asarray(mats, F32)
    hi = m32.astype(BF16)
    lo = (m32 - hi.astype(F32)).astype(BF16)
    return jnp.concatenate([hi, lo, hi], axis=-1)


def _even_layer(h, tbl, layer, nw1, nw2, acat, pw, ps, w1, w3, w2, geom):
    nblk, nctx_blk, nbatch = geom
    from_inputs = isinstance(h, tuple)
    d = h[0].shape[-1]
    nt = nbatch * nblk * BLK
    dff = w1.shape[-1]
    j = layer // 2
    gw = d // len(POOL_WINDOWS)
    kern = functools.partial(_even_kernel, nblk, nctx_blk, nbatch, d, from_inputs)
    vec = lambda idx: pl.BlockSpec((1, 1, d), lambda i: (idx, 0, 0))
    tm = TILE_BLKS * BLK
    if from_inputs:
        def src_spec(sb, latent):
            def index(i):
                g = i * TILE_BLKS + sb
                b = g // nblk
                jb = g - b * nblk
                blk = jnp.maximum(jb - nctx_blk, 0) if latent else jnp.minimum(jb, nctx_blk - 1)
                return (b, blk, 0)
            return pl.BlockSpec((1, BLK, d), index)
        h_specs = [src_spec(sb, latent) for sb in range(TILE_BLKS) for latent in (False, True)]
        h_args = [h[0], h[1]] * TILE_BLKS
    else:
        h_specs = [pl.BlockSpec((tm, d), lambda i: (i, 0))]
        h_args = [h]
    return pl.pallas_call(
        kern,
        grid=(nt // tm,),
        in_specs=h_specs + [
            pl.BlockSpec((1, 8, N_MOD * d), lambda i: (layer, 0, 0)),
            vec(layer), vec(layer),
            _resident(acat.shape),
            pl.BlockSpec((1, len(POOL_WINDOWS), gw, gw), lambda i: (j, 0, 0, 0)),
            vec(j),
            pl.BlockSpec((1, d, dff), lambda i: (j, 0, 0), pipeline_mode=pl.Buffered(1)),
            pl.BlockSpec((1, d, dff), lambda i: (j, 0, 0), pipeline_mode=pl.Buffered(1)),
            pl.BlockSpec((1, dff, d), lambda i: (j, 0, 0), pipeline_mode=pl.Buffered(1)),
        ],
        out_specs=pl.BlockSpec((tm, d), lambda i: (i, 0)),
        out_shape=jax.ShapeDtypeStruct((nt, d), F32),
        compiler_params=_params(("arbitrary",)),
        name=f"pool_ffn_{layer}",
    )(*h_args, tbl, nw1, nw2, acat, pw, ps, w1, w3, w2)


def _ssd_in_kernel(nblk, nctx_blk, nbatch, d, d_inner,
                   hp_ref, h_ref, hn_ref, tbl_ref, nw_ref, wz_ref, wx_ref, wdf_ref, wdb_ref,
                   cw_ref, cb_ref, dtb_ref,
                   z_ref, xs_ref, bc_ref, dtf_ref, dtb_out_ref, *xbc_scrs):
    for sb in range(TILE_BLKS):
        rows = slice(sb * BLK, (sb + 1) * BLK)
        prev_rows = hp_ref[...] if sb == 0 else h_ref[sb * BLK - HALO:sb * BLK, :]
        next_rows = hn_ref[...] if sb == TILE_BLKS - 1 else h_ref[(sb + 1) * BLK:(sb + 1) * BLK + HALO, :]
        _ssd_in_block(nblk, nctx_blk, nbatch, d, d_inner, pl.program_id(0) * TILE_BLKS + sb,
                      prev_rows, h_ref[rows, :], next_rows, tbl_ref, nw_ref, wz_ref, wx_ref, wdf_ref, wdb_ref,
                      cw_ref, cb_ref, dtb_ref, z_ref, xs_ref, bc_ref, dtf_ref, dtb_out_ref, xbc_scrs[sb], rows)


def _ssd_in_block(nblk, nctx_blk, nbatch, d, d_inner, blk, h_prev, h_blk, h_next,
                  tbl_ref, nw_ref, wz_ref, wx_ref, wdf_ref, wdb_ref, cw_ref, cb_ref, dtb_ref,
                  z_ref, xs_ref, bc_ref, dtf_ref, dtb_out_ref, xbc_scr, rows):
    row, _, j = _block_row(nblk, nctx_blk, nbatch, blk)
    first = jnp.logical_or(j == 0, j == nctx_blk)
    last = jnp.logical_or(j == nctx_blk - 1, j == nblk - 1)
    shift = tbl_ref[0, pl.ds(row, 1), 0:d]
    scale = tbl_ref[0, pl.ds(row, 1), d:2 * d]
    nw = nw_ref[0]

    def modn(x):
        return _rms(x, nw) * (1.0 + scale) + shift

    uf = modn(h_blk)
    u = uf.astype(BF16)
    up = modn(h_prev) * jnp.where(first, 0.0, 1.0)
    un = modn(h_next) * jnp.where(last, 0.0, 1.0)
    u_ext = jnp.concatenate([up, uf, un], axis=0).astype(BF16)
    nslab = xbc_scr.shape[0]
    xs_slabs = d_inner // 128
    per = CONV_SLABS

    def project(c0):
        val = _dot(u_ext, wx_ref[0, :, c0 * 128:(c0 + per) * 128])
        for c in range(c0, c0 + per):
            xbc_scr[c, :, :] = val[:, (c - c0) * 128:(c - c0 + 1) * 128]

    def conv(c0):
        for c in range(c0, c0 + per):
            lanes = slice(c * 128, (c + 1) * 128)
            acc = cb_ref[0, :, lanes]
            for k in range(D_CONV):
                off = HALO - CONV_LEFT + k
                acc = acc + xbc_scr[c, off:off + BLK, :] * cw_ref[0, k:k + 1, lanes]
            y = _silu(acc).astype(BF16)
            if c < xs_slabs:
                xs_ref[rows, lanes] = y
            else:
                bc_ref[rows, (c - xs_slabs) * 128:(c - xs_slabs + 1) * 128] = y

    def softplus(x):
        return jnp.maximum(x, 0.0) + jnp.log1p(jnp.exp(-jnp.abs(x)))

    zw = per * 128
    z_pieces = d_inner // zw
    project(0)
    for i, c0 in enumerate(range(0, nslab, per)):
        if c0 + per < nslab:
            project(c0 + per)
        if i < z_pieces:
            z_ref[rows, i * zw:(i + 1) * zw] = _dot(u, wz_ref[0, :, i * zw:(i + 1) * zw]).astype(BF16)
        if i == z_pieces:
            dtf_ref[rows, :] = softplus(_dot(u, wdf_ref[0]) + dtb_ref[0, 0:1, :])
            dtb_out_ref[rows, :] = softplus(_dot(u, wdb_ref[0]) + dtb_ref[0, 1:2, :])
        conv(c0)


def _ssd_in(h, tbl, layer, nw, wz, wx, wdf, wdb, cw, cb, dtb, geom):
    nt, d = h.shape
    nblk, nctx_blk, nbatch = geom
    j = layer // 2
    d_inner = wz.shape[-1]
    conv_dim = wx.shape[-1]
    tm = TILE_BLKS * BLK
    hb = tm // HALO
    nh = nt // HALO
    kern = functools.partial(_ssd_in_kernel, nblk, nctx_blk, nbatch, d, d_inner)
    res3 = lambda a: pl.BlockSpec((1,) + a.shape[1:], lambda i: (j, 0, 0), pipeline_mode=pl.Buffered(1))
    return pl.pallas_call(
        kern,
        grid=(nt // tm,),
        in_specs=[
            pl.BlockSpec((HALO, d), lambda i: (jnp.maximum(i * hb - 1, 0), 0)),
            pl.BlockSpec((tm, d), lambda i: (i, 0)),
            pl.BlockSpec((HALO, d), lambda i: (jnp.minimum((i + 1) * hb, nh - 1), 0)),
            pl.BlockSpec((1, 8, N_MOD * d), lambda i: (layer, 0, 0)),
            pl.BlockSpec((1, 1, d), lambda i: (layer, 0, 0)),
            res3(wz), res3(wx), res3(wdf), res3(wdb),
            pl.BlockSpec((1, D_CONV, conv_dim), lambda i: (j, 0, 0)),
            pl.BlockSpec((1, 1, conv_dim), lambda i: (j, 0, 0)),
            pl.BlockSpec((1, 2, CHUNK), lambda i: (j, 0, 0)),
        ],
        out_specs=[
            pl.BlockSpec((tm, d_inner), lambda i: (i, 0)),
            pl.BlockSpec((tm, d_inner), lambda i: (i, 0)),
            pl.BlockSpec((tm, conv_dim - d_inner), lambda i: (i, 0)),
            pl.BlockSpec((tm, CHUNK), lambda i: (i, 0)),
            pl.BlockSpec((tm, CHUNK), lambda i: (i, 0)),
        ],
        out_shape=[
            jax.ShapeDtypeStruct((nt, d_inner), BF16),
            jax.ShapeDtypeStruct((nt, d_inner), BF16),
            jax.ShapeDtypeStruct((nt, conv_dim - d_inner), BF16),
            jax.ShapeDtypeStruct((nt, CHUNK), F32),
            jax.ShapeDtypeStruct((nt, CHUNK), F32),
        ],
        scratch_shapes=[pltpu.VMEM((conv_dim // 128, BLK + 2 * HALO, 128), F32)] * TILE_BLKS,
        compiler_params=_params(("arbitrary",)),
        name=f"ssd_in_{layer}",
    )(h, h, h, tbl, nw, wz, wx, wdf, wdb, cw, cb, dtb)


def _scan_prep(dt, alog, tri, reverse):
    t = dt.shape[0]
    lane = lax.broadcasted_iota(jnp.int32, (1, CHUNK), 1)
    a_row = jnp.where(lane < N_HEADS, -jnp.exp(alog), 0.0)
    a = dt * (a_row * LOG2E)
    p0, p1, p2 = _split3(a)
    cum = _dot(tri, p0) + _dot(tri, p1) + _dot(tri, p2)
    tot = cum[0:1, :] if reverse else cum[t - 1:t, :]
    dte = jnp.exp2(tot - cum)
    ecum = jnp.exp2(cum)
    cdec = jnp.exp2(tot)
    row_t = (cum - jnp.log2(dt)).T

    q0, q1, q2 = _split3(cum)
    stacked = (q0.astype(F32) + pltpu.roll(q1.astype(F32), N_HEADS, 1)
               + pltpu.roll(q2.astype(F32), 2 * N_HEADS, 1)).astype(BF16)
    li = lax.broadcasted_iota(jnp.int32, (t, t), 0)
    si = lax.broadcasted_iota(jnp.int32, (t, t), 1)
    keep = (si >= li) if reverse else (si <= li)
    return dict(wdt=(dt * dte).astype(BF16), ecum=ecum.astype(BF16),
                cdec=_split3(jnp.broadcast_to(cdec, (8, CHUNK))), stacked=stacked, row_t=row_t, keep=keep)


def _scan_expand(g, preps, e_ref):
    heads = N_HEADS // N_GROUPS
    e = e_ref[:, g * heads * HEAD_DIM:(g + 1) * heads * HEAD_DIM]
    stack = lambda key: jnp.concatenate([p[key] for p in preps], axis=0)
    w_x = _dot(stack("wdt"), e).astype(BF16)
    ec_x = _dot(stack("ecum"), e)
    cd_x = sum(_dot(jnp.concatenate([p["cdec"][i] for p in preps], axis=0), e) for i in range(3))
    rows = lambda a, n: a[n * CHUNK:(n + 1) * CHUNK, :]
    return [dict(w_x=rows(w_x, n), ec_x=rows(ec_x, n), cd_x=cd_x[8 * n:8 * n + 1, :])
            for n in range(len(preps))]


def _scan_cb(g, bc_ref, rows):
    gn = N_GROUPS * D_STATE
    b_g = bc_ref[rows, g * D_STATE:(g + 1) * D_STATE]
    c_g = bc_ref[rows, gn + g * D_STATE:gn + (g + 1) * D_STATE]
    return lax.dot_general(c_g, b_g, (((1,), (1,)), ((), ())), preferred_element_type=F32)


def _scan_decay_pair(g, q, prep, cb, e3_ref):
    heads = N_HEADS // N_GROUPS
    h0 = g * heads + 2 * q
    colb = _dot(prep["stacked"], e3_ref[:, h0 * CHUNK:(h0 + 2) * CHUNK])
    ms = []
    for i in range(2):
        seg = colb[:, i * CHUNK:(i + 1) * CHUNK] - prep["row_t"][h0 + i:h0 + i + 1, :]
        lmat = jnp.exp2(jnp.where(prep["keep"], seg, -1e30))
        ms.append((cb * lmat).astype(BF16))
    return jnp.concatenate(ms, axis=1)


def _scan_pair_dot(g, q, m_pair, x_ref, rows):
    heads = N_HEADS // N_GROUPS
    first_head = lax.broadcasted_iota(jnp.int32, (CHUNK, 2 * HEAD_DIM), 1) < HEAD_DIM
    x_pair = x_ref[rows, (g * heads + 2 * q) * HEAD_DIM:(g * heads + 2 * q + 2) * HEAD_DIM]
    zero = jnp.zeros_like(x_pair)
    rhs = jnp.concatenate([jnp.where(first_head, x_pair, zero),
                           jnp.where(first_head, zero, x_pair)], axis=0)
    return _dot(m_pair, rhs)


def _scan_finish(g, ydiag, ex, x_ref, bc_ref, s_ref, y_ref, dskip, rows):
    heads = N_HEADS // N_GROUPS
    gn = N_GROUPS * D_STATE
    gp = heads * HEAD_DIM
    b_g = bc_ref[rows, g * D_STATE:(g + 1) * D_STATE]
    c_g = bc_ref[rows, gn + g * D_STATE:gn + (g + 1) * D_STATE]
    sl = slice(g * gp, (g + 1) * gp)
    xg = x_ref[rows, sl]
    s_old = s_ref[:, sl]
    y_g = jnp.concatenate(ydiag, axis=1) + _dot(c_g, s_old.astype(BF16)) * ex["ec_x"]
    if dskip is not None:
        y_g = y_g + xg.astype(F32) * dskip[:, sl]
    y_ref[rows, sl] = y_g.astype(y_ref.dtype)
    s_new = lax.dot_general(b_g, xg * ex["w_x"], (((0,), (0,)), ((), ())), preferred_element_type=F32)
    s_ref[:, sl] = s_old * ex["cd_x"] + s_new


def _ssd_scan_kernel(xf_ref, bcf_ref, dtf_ref, xb_ref, bcb_ref, dtb_ref, alog_ref, dsk_ref,
                     tril_ref, triu_ref, e_ref, e3_ref, yf_ref, yb_ref, sf_ref, sb_ref):
    @pl.when(pl.program_id(1) == 0)
    def _():
        sf_ref[...] = jnp.zeros_like(sf_ref)
        sb_ref[...] = jnp.zeros_like(sb_ref)

    rows_f = [slice(n * CHUNK, (n + 1) * CHUNK) for n in range(SCAN_CHUNKS)]
    rows_b = rows_f[::-1]
    pf = [_scan_prep(dtf_ref[r, :], alog_ref[0, 0:1, :], tril_ref[...], False) for r in rows_f]
    pb = [_scan_prep(dtb_ref[r, :], alog_ref[0, 1:2, :], triu_ref[...], True) for r in rows_b]
    cb_of = lambda g: ([_scan_cb(g, bcf_ref, r) for r in rows_f], [_scan_cb(g, bcb_ref, r) for r in rows_b])
    cbs = {0: cb_of(0)}
    dirs = ((pf, xf_ref, bcf_ref, sf_ref, yf_ref, dsk_ref[0], rows_f),
            (pb, xb_ref, bcb_ref, sb_ref, yb_ref, None, rows_b))
    pairs = N_HEADS // N_GROUPS // 2
    items = [(g, n, dr, q) for g in range(N_GROUPS) for n in range(SCAN_CHUNKS) for dr in (0, 1)
             for q in range(pairs)]

    def decay(item):
        g, n, dr, q = item
        return _scan_decay_pair(g, q, dirs[dr][0][n], cbs[g][dr][n], e3_ref)

    ex = [_scan_expand(0, pf, e_ref), _scan_expand(0, pb, e_ref)]
    ex_next = None
    m_pair = decay(items[0])
    ydiag = []
    for i, (g, n, dr, q) in enumerate(items):
        prep, x_ref, bc_ref, s_ref, y_ref, dskip, rows = dirs[dr]
        if (n, dr, q) == (0, 0, 0) and g + 1 < N_GROUPS:
            cbs[g + 1] = cb_of(g + 1)
            ex_next = [_scan_expand(g + 1, pf, e_ref), _scan_expand(g + 1, pb, e_ref)]
        m_next = decay(items[i + 1]) if i + 1 < len(items) else None
        ydiag.append(_scan_pair_dot(g, q, m_pair, x_ref, rows[n]))
        m_pair = m_next
        if q == pairs - 1:
            _scan_finish(g, ydiag, ex[dr][n], x_ref, bc_ref, s_ref, y_ref, dskip, rows[n])
            ydiag = []
            if (n, dr) == (SCAN_CHUNKS - 1, 1):
                ex = ex_next


def _scan_constants():
    li = np.arange(CHUNK)[:, None]
    ti = np.arange(CHUNK)[None, :]
    tril = (ti <= li).astype(np.float32)
    triu = (ti >= li).astype(np.float32)
    e = np.zeros((CHUNK, N_HEADS * HEAD_DIM), np.float32)
    e3 = np.zeros((CHUNK, N_HEADS * CHUNK), np.float32)
    for h in range(N_HEADS):
        e[h, h * HEAD_DIM:(h + 1) * HEAD_DIM] = 1.0
        for piece in range(3):
            e3[piece * N_HEADS + h, h * CHUNK:(h + 1) * CHUNK] = 1.0
    return tuple(jnp.asarray(m, BF16) for m in (tril, triu, e, e3))


def _ssd_scan(xs, bc, dtf, dtb, alog, dskip, layer, consts, nbatch, nchunk, ncc):
    nt, d_inner = xs.shape
    bcw = bc.shape[1]
    j = layer // 2
    tril, triu, e, e3 = consts

    assert nchunk % SCAN_CHUNKS == 0 and ncc % SCAN_CHUNKS == 0
    nstep = nchunk // SCAN_CHUNKS
    ncs = ncc // SCAN_CHUNKS
    rows = SCAN_CHUNKS * CHUNK

    def fwd(b, c):
        return (b * nstep + c, 0)

    def bwd(b, c):
        return (b * nstep + jnp.where(c < ncs, ncs - 1 - c, nstep - 1 - (c - ncs)), 0)

    return pl.pallas_call(
        _ssd_scan_kernel,
        grid=(nbatch, nstep),
        in_specs=[
            pl.BlockSpec((rows, d_inner), fwd), pl.BlockSpec((rows, bcw), fwd), pl.BlockSpec((rows, CHUNK), fwd),
            pl.BlockSpec((rows, d_inner), bwd), pl.BlockSpec((rows, bcw), bwd), pl.BlockSpec((rows, CHUNK), bwd),
            pl.BlockSpec((1, 2, CHUNK), lambda b, c: (j, 0, 0)),
            pl.BlockSpec((1, 1, d_inner), lambda b, c: (j, 0, 0)),
            _resident(tril.shape), _resident(triu.shape), _resident(e.shape), _resident(e3.shape),
        ],
        out_specs=[pl.BlockSpec((rows, d_inner), fwd), pl.BlockSpec((rows, d_inner), bwd)],
        out_shape=[jax.ShapeDtypeStruct((nt, d_inner), BF16)] * 2,
        scratch_shapes=[pltpu.VMEM((D_STATE, d_inner), F32)] * 2,
        compiler_params=_params(("arbitrary", "arbitrary")),
        name=f"ssd_scan_{layer}",
    )(xs, bc, dtf, xs, bc, dtb, alog, dskip, tril, triu, e, e3)


def _ssd_out_kernel(nblk, nctx_blk, nbatch, d,
                    yf_ref, yb_ref, z_ref, h_ref, tbl_ref, nw_ref, wo_ref, o_ref):
    y = (yf_ref[...].astype(F32) + yb_ref[...].astype(F32)) * _silu(z_ref[...].astype(F32))
    proj = _dot(_rms(y, nw_ref[0]).astype(BF16), wo_ref[0])
    for sb in range(TILE_BLKS):
        rows = slice(sb * BLK, (sb + 1) * BLK)
        row, _, _ = _block_row(nblk, nctx_blk, nbatch, pl.program_id(0) * TILE_BLKS + sb)
        gate = tbl_ref[0, pl.ds(row, 1), 2 * d:3 * d]
        o_ref[rows, :] = h_ref[rows, :] + gate * proj[rows, :]


def _ssd_out(yf, yb, z, h, tbl, layer, nw, wo, geom):
    nt, d = h.shape
    d_inner = z.shape[1]
    nblk, nctx_blk, nbatch = geom
    j = layer // 2
    kern = functools.partial(_ssd_out_kernel, nblk, nctx_blk, nbatch, d)
    tm = TILE_BLKS * BLK
    big = pl.BlockSpec((tm, d_inner), lambda i: (i, 0))
    return pl.pallas_call(
        kern,
        grid=(nt // tm,),
        in_specs=[
            big, big, big,
            pl.BlockSpec((tm, d), lambda i: (i, 0)),
            pl.BlockSpec((1, 8, N_MOD * d), lambda i: (layer, 0, 0)),
            pl.BlockSpec((1, 1, d_inner), lambda i: (j, 0, 0)),
            pl.BlockSpec((1, d_inner, d), lambda i: (j, 0, 0), pipeline_mode=pl.Buffered(1)),
        ],
        out_specs=pl.BlockSpec((tm, d), lambda i: (i, 0)),
        out_shape=jax.ShapeDtypeStruct((nt, d), F32),
        compiler_params=_params(("arbitrary",)),
        name=f"ssd_out_{layer}",
    )(yf, yb, z, h, tbl, nw, wo)


def _route_kernel(nblk, nctx_blk, nbatch, d,
                  h_ref, tbl_ref, nw_ref, rw_ref, sl_ref, info_ref, cnt_ref, carry_ref):
    @pl.when(pl.program_id(0) == 0)
    def _():
        carry_ref[...] = jnp.zeros_like(carry_ref)

    row, _, _ = _block_row(nblk, nctx_blk, nbatch)
    shift = tbl_ref[0, pl.ds(row, 1), 3 * d:4 * d]
    scale = tbl_ref[0, pl.ds(row, 1), 4 * d:5 * d]
    v = _rms(h_ref[...], nw_ref[0]) * (1.0 + scale) + shift
    lane = lax.broadcasted_iota(jnp.int32, (BLK, CHUNK), 1).astype(F32)
    logits = jnp.where(lane < N_EXPERTS, _dot_hi(v, rw_ref[0]), -jnp.inf)
    m1 = jnp.max(logits, axis=1, keepdims=True)
    i1 = jnp.min(jnp.where(logits == m1, lane, float(CHUNK)), axis=1, keepdims=True)
    rest = jnp.where(lane == i1, -jnp.inf, logits)
    m2 = jnp.max(rest, axis=1, keepdims=True)
    i2 = jnp.min(jnp.where(rest == m2, lane, float(CHUNK)), axis=1, keepdims=True)
    e2 = jnp.exp(m2 - m1)
    g1 = 1.0 / (1.0 + e2)
    g2 = e2 / (1.0 + e2)
    oh1 = (lane == i1)
    oh2 = (lane == i2)
    member = jnp.where(jnp.logical_or(oh1, oh2), 1.0, 0.0)
    before = carry_ref[...] + _dot(sl_ref[...], member.astype(BF16))
    r1 = jnp.sum(jnp.where(oh1, before, 0.0), axis=1, keepdims=True)
    r2 = jnp.sum(jnp.where(oh2, before, 0.0), axis=1, keepdims=True)
    total = carry_ref[...] + jnp.sum(member, axis=0, keepdims=True)
    carry_ref[...] = total
    cnt_ref[...] = jnp.broadcast_to(total, cnt_ref.shape)
    lane8 = lax.broadcasted_iota(jnp.int32, (BLK, 8), 1)
    info = jnp.where(lane8 == 0, i1,
           jnp.where(lane8 == 1, i2,
           jnp.where(lane8 == 2, r1,
           jnp.where(lane8 == 3, r2,
           jnp.where(lane8 == 4, g1,
           jnp.where(lane8 == 5, g2, 0.0))))))
    info_ref[...] = info


def _route(h, tbl, layer, nw, rw, strict_lower, geom):
    nt, d = h.shape
    nblk, nctx_blk, nbatch = geom
    j = layer // 2
    kern = functools.partial(_route_kernel, nblk, nctx_blk, nbatch, d)
    return pl.pallas_call(
        kern,
        grid=(nt // BLK,),
        in_specs=[
            pl.BlockSpec((BLK, d), lambda i: (i, 0)),
            pl.BlockSpec((1, 8, N_MOD * d), lambda i: (layer, 0, 0)),
            pl.BlockSpec((1, 1, d), lambda i: (layer, 0, 0)),
            pl.BlockSpec((1, d, CHUNK), lambda i: (j, 0, 0)),
            _resident(strict_lower.shape),
        ],
        out_specs=[
            pl.BlockSpec((BLK, 8), lambda i: (i, 0)),
            pl.BlockSpec((8, CHUNK), lambda i: (0, 0)),
        ],
        out_shape=[jax.ShapeDtypeStruct((nt, 8), F32), jax.ShapeDtypeStruct((8, CHUNK), F32)],
        scratch_shapes=[pltpu.VMEM((1, CHUNK), F32)],
        compiler_params=_params(("arbitrary",)),
        name=f"route_{layer}",
    )(h, tbl, nw, rw, strict_lower)


def _to_tiles(ref, x, rows):
    for k in range(SUB):
        ref[pl.ds(k, rows, stride=SUB), :] = x[:, k * 128:(k + 1) * 128]


def _from_tiles(ref, rows):
    return jnp.concatenate([ref[pl.ds(k, rows, stride=SUB), :] for k in range(SUB)], axis=1)


def _tile_rows(ref, p):
    return ref.at[pl.ds(pl.multiple_of(p * SUB, SUB), SUB), :]


def _dispatch_kernel(nblk, nctx_blk, nbatch, d,
                     ztile_ref, zvalid_ref, pos_ref, h_ref, tbl_ref, nw_ref, xs_ref, v_scr, z_scr, sem, zsem):
    @pl.when(pl.program_id(0) == 0)
    def _():
        z_scr[...] = jnp.zeros_like(z_scr)
        tile_rows = MOE_TILE * SUB
        for e in range(2 * N_EXPERTS):
            @pl.when(zvalid_ref[e] == 1)
            def _():
                first = pl.multiple_of(ztile_ref[e] * tile_rows, SUB)
                pltpu.make_async_copy(z_scr, xs_ref.at[pl.ds(first, tile_rows), :], zsem).start()
        for e in range(2 * N_EXPERTS):
            @pl.when(zvalid_ref[e] == 1)
            def _():
                pltpu.make_async_copy(z_scr, xs_ref.at[pl.ds(0, tile_rows), :], zsem).wait()

    step = pl.program_id(0)
    nsteps = pl.num_programs(0)
    slot = step % 2
    v_slot = v_scr.at[slot]

    def wait_block(s):
        def body(i, c):
            for _ in range(2 * DMA_UNROLL):
                pltpu.make_async_copy(v_scr.at[s, pl.ds(0, SUB), :], xs_ref.at[pl.ds(0, SUB), :], sem.at[s]).wait()
            return c
        lax.fori_loop(0, BLK // DMA_UNROLL, body, 0)

    @pl.when(step >= 2)
    def _():
        wait_block(slot)

    row, _, _ = _block_row(nblk, nctx_blk, nbatch)
    shift = tbl_ref[0, pl.ds(row, 1), 3 * d:4 * d]
    scale = tbl_ref[0, pl.ds(row, 1), 4 * d:5 * d]
    _to_tiles(v_slot, _rms(h_ref[...], nw_ref[0]) * (1.0 + scale) + shift, BLK)

    def start(i, c):
        for u in range(DMA_UNROLL):
            r = i * DMA_UNROLL + u
            for k in range(2):
                pltpu.make_async_copy(_tile_rows(v_slot, r), _tile_rows(xs_ref, pos_ref[0, 0, 2 * r + k]),
                                      sem.at[slot]).start(priority=k)
        return c

    lax.fori_loop(0, BLK // DMA_UNROLL, start, 0)

    @pl.when(step == nsteps - 1)
    def _():
        wait_block(slot)

        @pl.when(nsteps >= 2)
        def _():
            wait_block(1 - slot)


def _dispatch(h, tbl, layer, nw, pos, ztile, zvalid, nslots, geom):
    nt, d = h.shape
    nblk, nctx_blk, nbatch = geom
    kern = functools.partial(_dispatch_kernel, nblk, nctx_blk, nbatch, d)
    gs = pltpu.PrefetchScalarGridSpec(
        num_scalar_prefetch=2,
        grid=(nt // BLK,),
        in_specs=[
            pl.BlockSpec((1, 1, 2 * BLK), lambda i, zt, zv: (i, 0, 0), memory_space=pltpu.SMEM),
            pl.BlockSpec((BLK, d), lambda i, zt, zv: (i, 0)),
            pl.BlockSpec((1, 8, N_MOD * d), lambda i, zt, zv: (layer, 0, 0)),
            pl.BlockSpec((1, 1, d), lambda i, zt, zv: (layer, 0, 0)),
        ],
        out_specs=pl.BlockSpec(memory_space=pl.ANY),
        scratch_shapes=[pltpu.VMEM((2, BLK * SUB, 128), F32), pltpu.VMEM((MOE_TILE * SUB, 128), F32),
                        pltpu.SemaphoreType.DMA((2,)), pltpu.SemaphoreType.DMA(())],
    )
    return pl.pallas_call(
        kern,
        grid_spec=gs,
        out_shape=jax.ShapeDtypeStruct((nslots * SUB, 128), F32),
        compiler_params=_params(("arbitrary",)),
        name=f"dispatch_{layer}",
    )(ztile, zvalid, pos.reshape(nt // BLK, 1, 2 * BLK), h, tbl, nw)


def _expert_kernel(n_fchunk, te_ref, nu_ref, x_ref, w1_ref, w3_ref, w2_ref, o_ref):
    @pl.when(pl.program_id(0) >= nu_ref[0])
    def _():
        o_ref[...] = jnp.zeros_like(o_ref)

    @pl.when(pl.program_id(0) < nu_ref[0])
    def _():
        x = _from_tiles(x_ref, MOE_TILE).astype(BF16)
        acc = None
        for k in range(n_fchunk):
            sl = slice(k * MOE_FCHUNK, (k + 1) * MOE_FCHUNK)
            a = _dot(x, w1_ref[0, 0, :, sl])
            b = _dot(x, w3_ref[0, 0, :, sl])
            act = (_silu(a) * b).astype(BF16)
            part = _dot(act, w2_ref[0, 0, sl, :])
            acc = part if acc is None else acc + part
        _to_tiles(o_ref, acc, MOE_TILE)


def _experts(x_sorted, tile_expert, n_used, j, w1, w3, w2):
    d, dffe = w1.shape[-2:]
    n_tiles = x_sorted.shape[0] // (MOE_TILE * SUB)
    kern = functools.partial(_expert_kernel, dffe // MOE_FCHUNK)
    tile = lambda i, te, nu: (jnp.minimum(i, nu[0] - 1), 0)
    wspec = lambda shape: pl.BlockSpec((1, 1) + shape, lambda i, te, nu: (j, te[i], 0, 0))
    gs = pltpu.PrefetchScalarGridSpec(
        num_scalar_prefetch=2,
        grid=(n_tiles,),
        in_specs=[pl.BlockSpec((MOE_TILE * SUB, 128), tile), wspec((d, dffe)), wspec((d, dffe)), wspec((dffe, d))],
        out_specs=pl.BlockSpec((MOE_TILE * SUB, 128), lambda i, te, nu: (i, 0)),
    )
    return pl.pallas_call(
        kern,
        grid_spec=gs,
        out_shape=jax.ShapeDtypeStruct(x_sorted.shape, F32),
        compiler_params=_params(("arbitrary",)),
        name=f"experts_{j}",
    )(tile_expert, n_used, x_sorted, w1, w3, w2)


def _combine_kernel(row_of_step, d, final,
                    pos_ref, posn_ref, h_ref, info_ref, tbl_ref, y_ref, *rest):
    if final:
        fw_ref, o_ref, buf, sem = rest
    else:
        o_ref, buf, sem = rest
    step = pl.program_id(0)
    nsteps = pl.num_programs(0)
    slot = step % 2

    def gather(p_ref, s):
        def body(i, c):
            for u in range(DMA_UNROLL):
                r = i * DMA_UNROLL + u
                for k in range(2):
                    pltpu.make_async_copy(_tile_rows(y_ref, p_ref[0, 0, 2 * r + k]), _tile_rows(buf.at[s, k], r),
                                          sem.at[s]).start(priority=k)
            return c
        lax.fori_loop(0, BLK // DMA_UNROLL, body, 0)

    @pl.when(step == 0)
    def _():
        gather(pos_ref, 0)

    @pl.when(step + 1 < nsteps)
    def _():
        gather(posn_ref, 1 - slot)

    def wait(i, c):
        for _ in range(2 * DMA_UNROLL):
            pltpu.make_async_copy(y_ref.at[pl.ds(0, SUB), :], buf.at[slot, 0, pl.ds(0, SUB), :], sem.at[slot]).wait()
        return c

    lax.fori_loop(0, BLK // DMA_UNROLL, wait, 0)
    info = info_ref[...]
    g1 = info[:, 4:5]
    g2 = info[:, 5:6]
    gate = tbl_ref[0, pl.ds(row_of_step(step), 1), 5 * d:6 * d]
    out = h_ref[...] + gate * (g1 * _from_tiles(buf.at[slot, 0], BLK) + g2 * _from_tiles(buf.at[slot, 1], BLK))
    if final:
        o_ref[0] = _rms(out, fw_ref[...])
    else:
        o_ref[...] = out


def _combine(h, info, tbl, layer, y_sorted, pos, geom, final_w=None):
    nt, d = h.shape
    nblk, nctx_blk, nbatch = geom
    final = final_w is not None
    pos3 = pos.reshape(nt // BLK, 1, 2 * BLK)
    if final:
        nlat = nblk - nctx_blk
        nsteps = nbatch * nlat
        blk = lambda i: (i // nlat) * nblk + nctx_blk + i % nlat
        row_of_step = lambda step: step // nlat
        out_spec = pl.BlockSpec((1, BLK, d), lambda i: (i // nlat, i % nlat, 0))
        out_shape = jax.ShapeDtypeStruct((nbatch, nlat * BLK, d), F32)
    else:
        nsteps = nt // BLK
        blk = lambda i: i
        row_of_step = lambda step: _block_row(nblk, nctx_blk, nbatch, step)[0]
        out_spec = pl.BlockSpec((BLK, d), lambda i: (i, 0))
        out_shape = jax.ShapeDtypeStruct((nt, d), F32)
    nxt = lambda i: blk(jnp.minimum(i + 1, nsteps - 1))
    in_specs = [
        pl.BlockSpec((1, 1, 2 * BLK), lambda i: (blk(i), 0, 0), memory_space=pltpu.SMEM),
        pl.BlockSpec((1, 1, 2 * BLK), lambda i: (nxt(i), 0, 0), memory_space=pltpu.SMEM),
        pl.BlockSpec((BLK, d), lambda i: (blk(i), 0)),
        pl.BlockSpec((BLK, 8), lambda i: (blk(i), 0)),
        pl.BlockSpec((1, 8, N_MOD * d), lambda i: (layer, 0, 0)),
        pl.BlockSpec(memory_space=pl.ANY),
    ]
    args = [pos3, pos3, h, info, tbl, y_sorted]
    if final:
        in_specs.append(pl.BlockSpec((1, d), lambda i: (0, 0)))
        args.append(final_w.reshape(1, d))
    return pl.pallas_call(
        functools.partial(_combine_kernel, row_of_step, d, final),
        grid=(nsteps,),
        in_specs=in_specs,
        out_specs=out_spec,
        out_shape=out_shape,
        scratch_shapes=[pltpu.VMEM((2, 2, BLK * SUB, 128), F32), pltpu.SemaphoreType.DMA((2,))],
        compiler_params=_params(("arbitrary",)),
        name=f"combine_{layer}",
    )(*args)


def _moe_layer(h, tbl, layer, nw, rw, strict_lower, w1, w3, w2, geom, final_w=None):
    nt, d = h.shape
    j = layer // 2
    info, counts = _route(h, tbl, layer, nw, rw, strict_lower, geom)
    cnt = counts[0, :N_EXPERTS].astype(jnp.int32)
    tiles = (cnt + MOE_TILE - 1) // MOE_TILE
    tile_end = jnp.cumsum(tiles)
    offs = (tile_end - tiles) * MOE_TILE
    n_used = tile_end[-1]
    idx = info[:, 0:2].astype(jnp.int32)
    group_start = jnp.sum(jnp.where(idx[..., None] == jnp.arange(N_EXPERTS), offs, 0), axis=-1)
    pos = (group_start + info[:, 2:4].astype(jnp.int32)).reshape(-1)
    n_tiles = (2 * nt) // MOE_TILE + N_EXPERTS
    t = jnp.minimum(jnp.arange(n_tiles, dtype=jnp.int32), n_used - 1)
    tile_expert = jnp.sum(t[:, None] >= tile_end[None, :], axis=1).astype(jnp.int32)
    spare = n_used + jnp.arange(N_EXPERTS, dtype=jnp.int32)
    ztile = jnp.concatenate([tile_end - 1, spare]).astype(jnp.int32)
    zvalid = jnp.concatenate([tiles > 0, spare < n_tiles]).astype(jnp.int32)
    x_sorted = _dispatch(h, tbl, layer, nw, pos, ztile, zvalid, n_tiles * MOE_TILE, geom)
    y_sorted = _experts(x_sorted, tile_expert, n_used.reshape(1).astype(jnp.int32), j, w1, w3, w2)
    return _combine(h, info, tbl, layer, y_sorted, pos, geom, final_w)


def _final_kernel(h_ref, w_ref, o_ref):
    o_ref[0] = _rms(h_ref[...], w_ref[...])


def _final_norm(h, w, nbatch, nblk, nctx_blk):
    nt, d = h.shape
    nlat = nblk - nctx_blk
    return pl.pallas_call(
        _final_kernel,
        grid=(nbatch, nlat),
        in_specs=[
            pl.BlockSpec((BLK, d), lambda b, j: (b * nblk + nctx_blk + j, 0)),
            pl.BlockSpec((1, d), lambda b, j: (0, 0)),
        ],
        out_specs=pl.BlockSpec((1, BLK, d), lambda b, j: (b, j, 0)),
        out_shape=jax.ShapeDtypeStruct((nbatch, nlat * BLK, d), F32),
        compiler_params=_params(("arbitrary", "arbitrary")),
        name="final_norm",
    )(h, w.reshape(1, d))


def kernel(x, c, ctx, c_ctx, ada_w, ada_b, norm_mix_w, norm_ffn_w, pool_w, pool_scale, ssd_in_w, ssd_conv_w, ssd_conv_b, ssd_A_log, ssd_dt_bias, ssd_D, ssd_norm_w, ssd_out_w, ffn_w1, ffn_w3, ffn_w2, moe_router_w, moe_w1, moe_w3, moe_w2, final_norm_w):
    nbatch, seq, d = x.shape
    ctx_len = ctx.shape[1]
    depth = ada_w.shape[0]
    d_inner = ssd_norm_w.shape[-1]
    assert ctx_len % BLK == 0 and seq % BLK == 0 and nbatch < 8
    assert d_inner == N_HEADS * HEAD_DIM and d % len(POOL_WINDOWS) == 0
    nblk = (ctx_len + seq) // BLK
    nctx_blk = ctx_len // BLK
    geom = (nblk, nctx_blk, nbatch)
    nchunk = (ctx_len + seq) // CHUNK
    ncc = ctx_len // CHUNK

    h = (ctx, x)
    cvec = jnp.zeros((8, d), F32).at[:nbatch].set(c).at[nbatch].set(c_ctx)
    tbl = _mod_table(cvec, ada_w, ada_b)

    vec3 = lambda a: a.reshape(a.shape[0], 1, a.shape[-1])
    nmix = vec3(norm_mix_w)
    nffn = vec3(norm_ffn_w)
    acat = _pool_matrices()
    scan_consts = _scan_constants()
    strict_lower = jnp.asarray(np.tril(np.ones((BLK, BLK), np.float32), -1), BF16)

    conv_dim = ssd_conv_w.shape[-1]
    wz = ssd_in_w[:, :, :d_inner].astype(BF16)
    wx = ssd_in_w[:, :, d_inner:d_inner + conv_dim].astype(BF16)
    wdt = ssd_in_w[:, :, d_inner + conv_dim:]
    pad_dt = lambda w: jnp.pad(w, ((0, 0), (0, 0), (0, CHUNK - N_HEADS))).astype(BF16)
    wdf = pad_dt(wdt[:, :, :N_HEADS])
    wdb = pad_dt(wdt[:, :, N_HEADS:])
    pad_h = lambda a: jnp.pad(a, ((0, 0), (0, 0), (0, CHUNK - N_HEADS)))
    dtb = pad_h(ssd_dt_bias)
    alog = pad_h(ssd_A_log)
    dskip = vec3(jnp.repeat(ssd_D, HEAD_DIM, axis=-1))
    rw = jnp.pad(moe_router_w, ((0, 0), (0, 0), (0, CHUNK - N_EXPERTS)))
    pool_wb = pool_w.astype(BF16)
    ffn = [w.astype(BF16) for w in (ffn_w1, ffn_w3, ffn_w2)]
    moe = [w.astype(BF16) for w in (moe_w1, moe_w3, moe_w2)]
    wout = ssd_out_w.astype(BF16)

    for i in range(depth):
        if i % 2 == 0:
            h = _even_layer(h, tbl, i, nmix, nffn, acat, pool_wb, vec3(pool_scale), *ffn, geom)
        else:
            z, xs, bc, dtf, dtbw = _ssd_in(h, tbl, i, nmix, wz, wx, wdf, wdb, ssd_conv_w,
                                           vec3(ssd_conv_b), dtb, geom)
            yf, yb = _ssd_scan(xs, bc, dtf, dtbw, alog, dskip, i, scan_consts, nbatch, nchunk, ncc)
            h = _ssd_out(yf, yb, z, h, tbl, i, vec3(ssd_norm_w), wout, geom)
            fin = final_norm_w if (i == depth - 1) else None
            h = _moe_layer(h, tbl, i, nffn, rw, strict_lower, *moe, geom, fin)
    if depth % 2 == 1:
        h = _final_norm(h, final_norm_w, nbatch, nblk, nctx_blk)
    return h
```

```python
import functools

import numpy as np
import jax
import jax.numpy as jnp
from jax import lax
from jax.experimental import pallas as pl
from jax.experimental.pallas import tpu as pltpu

F32 = jnp.float32
BF16 = jnp.bfloat16
EPS = 1e-6

BLK = 256
TILE_BLKS = 2
CHUNK = 128
SCAN_CHUNKS = 2
GRID_W = 64
POOL_WINDOWS = (2, 4, 8, 16)
N_MOD = 6
HEAD_DIM = 64
N_HEADS = 32
N_GROUPS = 4
D_STATE = 128
D_CONV = 4
CONV_LEFT = 2
HALO = 8
CONV_SLABS = 2
N_EXPERTS = 8
MOE_TILE = 512
MOE_FCHUNK = 512
FFN_FCHUNK = 512
VMEM_LIMIT = 56 * 2**20
LOG2E = 1.4426950408889634
SUB = 8
DMA_UNROLL = 8


def _dot(a, b):
    return jnp.dot(a, b, preferred_element_type=F32)


def _split2(x):
    hi = x.astype(BF16)
    lo = (x - hi.astype(F32)).astype(BF16)
    return hi, lo


def _split3(x):
    p0 = x.astype(BF16)
    r = x - p0.astype(F32)
    p1 = r.astype(BF16)
    p2 = (r - p1.astype(F32)).astype(BF16)
    return p0, p1, p2


def _dot_hi(a, b):
    ah, al = _split2(a)
    bh, bl = _split2(b)
    return _dot(ah, bh) + _dot(al, bh) + _dot(ah, bl)


def _sigmoid(x):
    return 1.0 / (1.0 + jnp.exp(-x))


def _silu(x):
    return x * _sigmoid(x)


def _rms(x, w):
    ms = jnp.mean(x * x, axis=-1, keepdims=True)
    return x * lax.rsqrt(ms + EPS) * w


def _params(sem):
    return pltpu.CompilerParams(dimension_semantics=sem, vmem_limit_bytes=VMEM_LIMIT)


def _resident(shape):
    nd = len(shape)
    return pl.BlockSpec(shape, lambda *_: (0,) * nd, pipeline_mode=pl.Buffered(1))


def _block_row(nblk, nctx_blk, nbatch, blk=None):
    if blk is None:
        blk = pl.program_id(0)
    b = blk // nblk
    j = blk - b * nblk
    is_ctx = j < nctx_blk
    return jnp.where(is_ctx, nbatch, b), is_ctx, j


def _mod_kernel(c_ref, w_ref, b_ref, o_ref):
    o_ref[0] = _dot_hi(_silu(c_ref[...]), w_ref[0]) + b_ref[0]


def _mod_table(cvec, ada_w, ada_b):
    depth, d, n = ada_w.shape
    tn = n // 4
    return pl.pallas_call(
        _mod_kernel,
        grid=(depth, n // tn),
        in_specs=[
            pl.BlockSpec((8, d), lambda l, j: (0, 0)),
            pl.BlockSpec((1, d, tn), lambda l, j: (l, 0, j)),
            pl.BlockSpec((1, 1, tn), lambda l, j: (l, 0, j)),
        ],
        out_specs=pl.BlockSpec((1, 8, tn), lambda l, j: (l, 0, j)),
        out_shape=jax.ShapeDtypeStruct((depth, 8, n), F32),
        compiler_params=_params(("arbitrary", "arbitrary")),
        name="mod_table",
    )(cvec, ada_w, ada_b.reshape(depth, 1, n))


def _even_kernel(nblk, nctx_blk, nbatch, d, from_inputs, *refs):
    n_h = 2 * TILE_BLKS if from_inputs else 1
    h_refs = refs[:n_h]
    tbl_ref, nw1_ref, nw2_ref, acat_ref, pw_ref, ps_ref, w1_ref, w3_ref, w2_ref, o_ref = refs[n_h:]

    def mods(sb):
        row, is_ctx, _ = _block_row(nblk, nctx_blk, nbatch, pl.program_id(0) * TILE_BLKS + sb)
        return (lambda k: tbl_ref[0, pl.ds(row, 1), k * d:(k + 1) * d]), is_ctx

    def mixer(sb):
        mod, is_ctx = mods(sb)
        kind = is_ctx.astype(jnp.int32)
        if from_inputs:
            h = jnp.where(is_ctx, h_refs[2 * sb][0], h_refs[2 * sb + 1][0])
        else:
            h = h_refs[0][sb * BLK:(sb + 1) * BLK, :]
        u = _rms(h, nw1_ref[0]) * (1.0 + mod(1)) + mod(0)
        gw = d // len(POOL_WINDOWS)
        ys = []
        for g in range(len(POOL_WINDOWS)):
            ug = u[:, g * gw:(g + 1) * gw]
            uh, ul = _split2(ug)
            p = _dot(acat_ref[kind, g], jnp.concatenate([uh, uh, ul], axis=0)) - ug
            ys.append(_dot(p.astype(BF16), pw_ref[0, g]))
        y = jnp.concatenate(ys, axis=1) * ps_ref[0]
        h1 = h + mod(2) * y
        v = (_rms(h1, nw2_ref[0]) * (1.0 + mod(4)) + mod(3)).astype(BF16)
        return h1, v

    dff = w1_ref.shape[-1]
    chunks = [slice(c, min(c + FFN_FCHUNK, dff)) for c in range(0, dff, FFN_FCHUNK)]
    n_f = len(chunks)

    def up(v, k):
        return _dot(v, w1_ref[0, :, chunks[k]]), _dot(v, w3_ref[0, :, chunks[k]])

    def down(ab, k):
        act = (_silu(ab[0]) * ab[1]).astype(BF16)
        return _dot(act, w2_ref[0, chunks[k], :])

    mixed = [mixer(sb) for sb in range(TILE_BLKS)]
    v = jnp.concatenate([m[1] for m in mixed], axis=0)
    ab = up(v, 0)
    acc = None
    for k in range(n_f):
        ab_next = up(v, k + 1) if k + 1 < n_f else None
        part = down(ab, k)
        acc = part if acc is None else acc + part
        ab = ab_next
    for sb in range(TILE_BLKS):
        mod, _ = mods(sb)
        o_ref[sb * BLK:(sb + 1) * BLK, :] = mixed[sb][0] + mod(5) * acc[sb * BLK:(sb + 1) * BLK, :]


def _pool_matrices():
    mats = np.zeros((2, len(POOL_WINDOWS), BLK, BLK), np.float64)
    for kind, seg in enumerate((GRID_W, BLK)):
        for g, w in enumerate(POOL_WINDOWS):
            lo = w // 2
            hi = w - 1 - lo
            for t in range(BLK):
                base = (t // seg) * seg
                tt = t - base
                start = max(tt - lo, 0)
                end = min(tt + hi + 1, seg)
                mats[kind, g, t, base + start:base + end] = 1.0 / (end - start)
    m32 = jnp.asarray(mats, F32)
    hi = m32.astype(BF16)
    lo = (m32 - hi.astype(F32)).astype(BF16)
    return jnp.concatenate([hi, lo, hi], axis=-1)


def _even_layer(h, tbl, layer, nw1, nw2, acat, pw, ps, w1, w3, w2, geom):
    nblk, nctx_blk, nbatch = geom
    from_inputs = isinstance(h, tuple)
    d = h[0].shape[-1]
    nt = nbatch * nblk * BLK
    dff = w1.shape[-1]
    j = layer // 2
    gw = d // len(POOL_WINDOWS)
    kern = functools.partial(_even_kernel, nblk, nctx_blk, nbatch, d, from_inputs)
    vec = lambda idx: pl.BlockSpec((1, 1, d), lambda i: (idx, 0, 0))
    tm = TILE_BLKS * BLK
    if from_inputs:
        def src_spec(sb, latent):
            def index(i):
                g = i * TILE_BLKS + sb
                b = g // nblk
                jb = g - b * nblk
                blk = jnp.maximum(jb - nctx_blk, 0) if latent else jnp.minimum(jb, nctx_blk - 1)
                return (b, blk, 0)
            return pl.BlockSpec((1, BLK, d), index)
        h_specs = [src_spec(sb, latent) for sb in range(TILE_BLKS) for latent in (False, True)]
        h_args = [h[0], h[1]] * TILE_BLKS
    else:
        h_specs = [pl.BlockSpec((tm, d), lambda i: (i, 0))]
        h_args = [h]
    return pl.pallas_call(
        kern,
        grid=(nt // tm,),
        in_specs=h_specs + [
            pl.BlockSpec((1, 8, N_MOD * d), lambda i: (layer, 0, 0)),
            vec(layer), vec(layer),
            _resident(acat.shape),
            pl.BlockSpec((1, len(POOL_WINDOWS), gw, gw), lambda i: (j, 0, 0, 0)),
            vec(j),
            pl.BlockSpec((1, d, dff), lambda i: (j, 0, 0), pipeline_mode=pl.Buffered(1)),
            pl.BlockSpec((1, d, dff), lambda i: (j, 0, 0), pipeline_mode=pl.Buffered(1)),
            pl.BlockSpec((1, dff, d), lambda i: (j, 0, 0), pipeline_mode=pl.Buffered(1)),
        ],
        out_specs=pl.BlockSpec((tm, d), lambda i: (i, 0)),
        out_shape=jax.ShapeDtypeStruct((nt, d), F32),
        compiler_params=_params(("arbitrary",)),
        name=f"pool_ffn_{layer}",
    )(*h_args, tbl, nw1, nw2, acat, pw, ps, w1, w3, w2)


def _ssd_in_kernel(nblk, nctx_blk, nbatch, d, d_inner,
                   hp_ref, h_ref, hn_ref, tbl_ref, nw_ref, wz_ref, wx_ref, wdf_ref, wdb_ref,
                   cw_ref, cb_ref, dtb_ref,
                   z_ref, xs_ref, bc_ref, dtf_ref, dtb_out_ref, *xbc_scrs):
    for sb in range(TILE_BLKS):
        rows = slice(sb * BLK, (sb + 1) * BLK)
        prev_rows = hp_ref[...] if sb == 0 else h_ref[sb * BLK - HALO:sb * BLK, :]
        next_rows = hn_ref[...] if sb == TILE_BLKS - 1 else h_ref[(sb + 1) * BLK:(sb + 1) * BLK + HALO, :]
        _ssd_in_block(nblk, nctx_blk, nbatch, d, d_inner, pl.program_id(0) * TILE_BLKS + sb,
                      prev_rows, h_ref[rows, :], next_rows, tbl_ref, nw_ref, wz_ref, wx_ref, wdf_ref, wdb_ref,
                      cw_ref, cb_ref, dtb_ref, z_ref, xs_ref, bc_ref, dtf_ref, dtb_out_ref, xbc_scrs[sb], rows)


def _ssd_in_block(nblk, nctx_blk, nbatch, d, d_inner, blk, h_prev, h_blk, h_next,
                  tbl_ref, nw_ref, wz_ref, wx_ref, wdf_ref, wdb_ref, cw_ref, cb_ref, dtb_ref,
                  z_ref, xs_ref, bc_ref, dtf_ref, dtb_out_ref, xbc_scr, rows):
    row, _, j = _block_row(nblk, nctx_blk, nbatch, blk)
    first = jnp.logical_or(j == 0, j == nctx_blk)
    last = jnp.logical_or(j == nctx_blk - 1, j == nblk - 1)
    shift = tbl_ref[0, pl.ds(row, 1), 0:d]
    scale = tbl_ref[0, pl.ds(row, 1), d:2 * d]
    nw = nw_ref[0]

    def modn(x):
        return _rms(x, nw) * (1.0 + scale) + shift

    uf = modn(h_blk)
    u = uf.astype(BF16)
    up = modn(h_prev) * jnp.where(first, 0.0, 1.0)
    un = modn(h_next) * jnp.where(last, 0.0, 1.0)
    u_ext = jnp.concatenate([up, uf, un], axis=0).astype(BF16)
    nslab = xbc_scr.shape[0]
    xs_slabs = d_inner // 128
    per = CONV_SLABS

    def project(c0):
        val = _dot(u_ext, wx_ref[0, :, c0 * 128:(c0 + per) * 128])
        for c in range(c0, c0 + per):
            xbc_scr[c, :, :] = val[:, (c - c0) * 128:(c - c0 + 1) * 128]

    def conv(c0):
        for c in range(c0, c0 + per):
            lanes = slice(c * 128, (c + 1) * 128)
            acc = cb_ref[0, :, lanes]
            for k in range(D_CONV):
                off = HALO - CONV_LEFT + k
                acc = acc + xbc_scr[c, off:off + BLK, :] * cw_ref[0, k:k + 1, lanes]
            y = _silu(acc).astype(BF16)
            if c < xs_slabs:
                xs_ref[rows, lanes] = y
            else:
                bc_ref[rows, (c - xs_slabs) * 128:(c - xs_slabs + 1) * 128] = y

    def softplus(x):
        return jnp.maximum(x, 0.0) + jnp.log1p(jnp.exp(-jnp.abs(x)))

    zw = per * 128
    z_pieces = d_inner // zw
    project(0)
    for i, c0 in enumerate(range(0, nslab, per)):
        if c0 + per < nslab:
            project(c0 + per)
        if i < z_pieces:
            z_ref[rows, i * zw:(i + 1) * zw] = _dot(u, wz_ref[0, :, i * zw:(i + 1) * zw]).astype(BF16)
        if i == z_pieces:
            dtf_ref[rows, :] = softplus(_dot(u, wdf_ref[0]) + dtb_ref[0, 0:1, :])
            dtb_out_ref[rows, :] = softplus(_dot(u, wdb_ref[0]) + dtb_ref[0, 1:2, :])
        conv(c0)


def _ssd_in(h, tbl, layer, nw, wz, wx, wdf, wdb, cw, cb, dtb, geom):
    nt, d = h.shape
    nblk, nctx_blk, nbatch = geom
    j = layer // 2
    d_inner = wz.shape[-1]
    conv_dim = wx.shape[-1]
    tm = TILE_BLKS * BLK
    hb = tm // HALO
    nh = nt // HALO
    kern = functools.partial(_ssd_in_kernel, nblk, nctx_blk, nbatch, d, d_inner)
    res3 = lambda a: pl.BlockSpec((1,) + a.shape[1:], lambda i: (j, 0, 0), pipeline_mode=pl.Buffered(1))
    return pl.pallas_call(
        kern,
        grid=(nt // tm,),
        in_specs=[
            pl.BlockSpec((HALO, d), lambda i: (jnp.maximum(i * hb - 1, 0), 0)),
            pl.BlockSpec((tm, d), lambda i: (i, 0)),
            pl.BlockSpec((HALO, d), lambda i: (jnp.minimum((i + 1) * hb, nh - 1), 0)),
            pl.BlockSpec((1, 8, N_MOD * d), lambda i: (layer, 0, 0)),
            pl.BlockSpec((1, 1, d), lambda i: (layer, 0, 0)),
            res3(wz), res3(wx), res3(wdf), res3(wdb),
            pl.BlockSpec((1, D_CONV, conv_dim), lambda i: (j, 0, 0)),
            pl.BlockSpec((1, 1, conv_dim), lambda i: (j, 0, 0)),
            pl.BlockSpec((1, 2, CHUNK), lambda i: (j, 0, 0)),
        ],
        out_specs=[
            pl.BlockSpec((tm, d_inner), lambda i: (i, 0)),
            pl.BlockSpec((tm, d_inner), lambda i: (i, 0)),
            pl.BlockSpec((tm, conv_dim - d_inner), lambda i: (i, 0)),
            pl.BlockSpec((tm, CHUNK), lambda i: (i, 0)),
            pl.BlockSpec((tm, CHUNK), lambda i: (i, 0)),
        ],
        out_shape=[
            jax.ShapeDtypeStruct((nt, d_inner), BF16),
            jax.ShapeDtypeStruct((nt, d_inner), BF16),
            jax.ShapeDtypeStruct((nt, conv_dim - d_inner), BF16),
            jax.ShapeDtypeStruct((nt, CHUNK), F32),
            jax.ShapeDtypeStruct((nt, CHUNK), F32),
        ],
        scratch_shapes=[pltpu.VMEM((conv_dim // 128, BLK + 2 * HALO, 128), F32)] * TILE_BLKS,
        compiler_params=_params(("arbitrary",)),
        name=f"ssd_in_{layer}",
    )(h, h, h, tbl, nw, wz, wx, wdf, wdb, cw, cb, dtb)


def _scan_prep(dt, alog, tri, reverse):
    t = dt.shape[0]
    lane = lax.broadcasted_iota(jnp.int32, (1, CHUNK), 1)
    a_row = jnp.where(lane < N_HEADS, -jnp.exp(alog), 0.0)
    a = dt * (a_row * LOG2E)
    p0, p1, p2 = _split3(a)
    cum = _dot(tri, p0) + _dot(tri, p1) + _dot(tri, p2)
    tot = cum[0:1, :] if reverse else cum[t - 1:t, :]
    dte = jnp.exp2(tot - cum)
    ecum = jnp.exp2(cum)
    cdec = jnp.exp2(tot)
    row_t = (cum - jnp.log2(dt)).T

    q0, q1, q2 = _split3(cum)
    stacked = (q0.astype(F32) + pltpu.roll(q1.astype(F32), N_HEADS, 1)
               + pltpu.roll(q2.astype(F32), 2 * N_HEADS, 1)).astype(BF16)
    li = lax.broadcasted_iota(jnp.int32, (t, t), 0)
    si = lax.broadcasted_iota(jnp.int32, (t, t), 1)
    keep = (si >= li) if reverse else (si <= li)
    return dict(wdt=(dt * dte).astype(BF16), ecum=ecum.astype(BF16),
                cdec=_split3(jnp.broadcast_to(cdec, (8, CHUNK))), stacked=stacked, row_t=row_t, keep=keep)


def _scan_expand(g, preps, e_ref):
    heads = N_HEADS // N_GROUPS
    e = e_ref[:, g * heads * HEAD_DIM:(g + 1) * heads * HEAD_DIM]
    stack = lambda key: jnp.concatenate([p[key] for p in preps], axis=0)
    w_x = _dot(stack("wdt"), e).astype(BF16)
    ec_x = _dot(stack("ecum"), e)
    cd_x = sum(_dot(jnp.concatenate([p["cdec"][i] for p in preps], axis=0), e) for i in range(3))
    rows = lambda a, n: a[n * CHUNK:(n + 1) * CHUNK, :]
    return [dict(w_x=rows(w_x, n), ec_x=rows(ec_x, n), cd_x=cd_x[8 * n:8 * n + 1, :])
            for n in range(len(preps))]


def _scan_cb(g, bc_ref, rows):
    gn = N_GROUPS * D_STATE
    b_g = bc_ref[rows, g * D_STATE:(g + 1) * D_STATE]
    c_g = bc_ref[rows, gn + g * D_STATE:gn + (g + 1) * D_STATE]
    return lax.dot_general(c_g, b_g, (((1,), (1,)), ((), ())), preferred_element_type=F32)


def _scan_decay_pair(g, q, prep, cb, e3_ref):
    heads = N_HEADS // N_GROUPS
    h0 = g * heads + 2 * q
    colb = _dot(prep["stacked"], e3_ref[:, h0 * CHUNK:(h0 + 2) * CHUNK])
    ms = []
    for i in range(2):
        seg = colb[:, i * CHUNK:(i + 1) * CHUNK] - prep["row_t"][h0 + i:h0 + i + 1, :]
        lmat = jnp.exp2(jnp.where(prep["keep"], seg, -1e30))
        ms.append((cb * lmat).astype(BF16))
    return jnp.concatenate(ms, axis=1)


def _scan_pair_dot(g, q, m_pair, x_ref, rows):
    heads = N_HEADS // N_GROUPS
    first_head = lax.broadcasted_iota(jnp.int32, (CHUNK, 2 * HEAD_DIM), 1) < HEAD_DIM
    x_pair = x_ref[rows, (g * heads + 2 * q) * HEAD_DIM:(g * heads + 2 * q + 2) * HEAD_DIM]
    zero = jnp.zeros_like(x_pair)
    rhs = jnp.concatenate([jnp.where(first_head, x_pair, zero),
                           jnp.where(first_head, zero, x_pair)], axis=0)
    return _dot(m_pair, rhs)


def _scan_finish(g, ydiag, ex, x_ref, bc_ref, s_ref, y_ref, dskip, rows):
    heads = N_HEADS // N_GROUPS
    gn = N_GROUPS * D_STATE
    gp = heads * HEAD_DIM
    b_g = bc_ref[rows, g * D_STATE:(g + 1) * D_STATE]
    c_g = bc_ref[rows, gn + g * D_STATE:gn + (g + 1) * D_STATE]
    sl = slice(g * gp, (g + 1) * gp)
    xg = x_ref[rows, sl]
    s_old = s_ref[:, sl]
    y_g = jnp.concatenate(ydiag, axis=1) + _dot(c_g, s_old.astype(BF16)) * ex["ec_x"]
    if dskip is not None:
        y_g = y_g + xg.astype(F32) * dskip[:, sl]
    y_ref[rows, sl] = y_g.astype(y_ref.dtype)
    s_new = lax.dot_general(b_g, xg * ex["w_x"], (((0,), (0,)), ((), ())), preferred_element_type=F32)
    s_ref[:, sl] = s_old * ex["cd_x"] + s_new


def _ssd_scan_kernel(xf_ref, bcf_ref, dtf_ref, xb_ref, bcb_ref, dtb_ref, alog_ref, dsk_ref,
                     tril_ref, triu_ref, e_ref, e3_ref, yf_ref, yb_ref, sf_ref, sb_ref):
    @pl.when(pl.program_id(1) == 0)
    def _():
        sf_ref[...] = jnp.zeros_like(sf_ref)
        sb_ref[...] = jnp.zeros_like(sb_ref)

    rows_f = [slice(n * CHUNK, (n + 1) * CHUNK) for n in range(SCAN_CHUNKS)]
    rows_b = rows_f[::-1]
    pf = [_scan_prep(dtf_ref[r, :], alog_ref[0, 0:1, :], tril_ref[...], False) for r in rows_f]
    pb = [_scan_prep(dtb_ref[r, :], alog_ref[0, 1:2, :], triu_ref[...], True) for r in rows_b]
    cb_of = lambda g: ([_scan_cb(g, bcf_ref, r) for r in rows_f], [_scan_cb(g, bcb_ref, r) for r in rows_b])
    cbs = {0: cb_of(0)}
    dirs = ((pf, xf_ref, bcf_ref, sf_ref, yf_ref, dsk_ref[0], rows_f),
            (pb, xb_ref, bcb_ref, sb_ref, yb_ref, None, rows_b))
    pairs = N_HEADS // N_GROUPS // 2
    items = [(g, n, dr, q) for g in range(N_GROUPS) for n in range(SCAN_CHUNKS) for dr in (0, 1)
             for q in range(pairs)]

    def decay(item):
        g, n, dr, q = item
        return _scan_decay_pair(g, q, dirs[dr][0][n], cbs[g][dr][n], e3_ref)

    ex = [_scan_expand(0, pf, e_ref), _scan_expand(0, pb, e_ref)]
    ex_next = None
    m_pair = decay(items[0])
    ydiag = []
    for i, (g, n, dr, q) in enumerate(items):
        prep, x_ref, bc_ref, s_ref, y_ref, dskip, rows = dirs[dr]
        if (n, dr, q) == (0, 0, 0) and g + 1 < N_GROUPS:
            cbs[g + 1] = cb_of(g + 1)
            ex_next = [_scan_expand(g + 1, pf, e_ref), _scan_expand(g + 1, pb, e_ref)]
        m_next = decay(items[i + 1]) if i + 1 < len(items) else None
        ydiag.append(_scan_pair_dot(g, q, m_pair, x_ref, rows[n]))
        m_pair = m_next
        if q == pairs - 1:
            _scan_finish(g, ydiag, ex[dr][n], x_ref, bc_ref, s_ref, y_ref, dskip, rows[n])
            ydiag = []
            if (n, dr) == (SCAN_CHUNKS - 1, 1):
                ex = ex_next


def _scan_constants():
    li = np.arange(CHUNK)[:, None]
    ti = np.arange(CHUNK)[None, :]
    tril = (ti <= li).astype(np.float32)
    triu = (ti >= li).astype(np.float32)
    e = np.zeros((CHUNK, N_HEADS * HEAD_DIM), np.float32)
    e3 = np.zeros((CHUNK, N_HEADS * CHUNK), np.float32)
    for h in range(N_HEADS):
        e[h, h * HEAD_DIM:(h + 1) * HEAD_DIM] = 1.0
        for piece in range(3):
            e3[piece * N_HEADS + h, h * CHUNK:(h + 1) * CHUNK] = 1.0
    return tuple(jnp.asarray(m, BF16) for m in (tril, triu, e, e3))


def _ssd_scan(xs, bc, dtf, dtb, alog, dskip, layer, consts, nbatch, nchunk, ncc):
    nt, d_inner = xs.shape
    bcw = bc.shape[1]
    j = layer // 2
    tril, triu, e, e3 = consts

    assert nchunk % SCAN_CHUNKS == 0 and ncc % SCAN_CHUNKS == 0
    nstep = nchunk // SCAN_CHUNKS
    ncs = ncc // SCAN_CHUNKS
    rows = SCAN_CHUNKS * CHUNK

    def fwd(b, c):
        return (b * nstep + c, 0)

    def bwd(b, c):
        return (b * nstep + jnp.where(c < ncs, ncs - 1 - c, nstep - 1 - (c - ncs)), 0)

    return pl.pallas_call(
        _ssd_scan_kernel,
        grid=(nbatch, nstep),
        in_specs=[
            pl.BlockSpec((rows, d_inner), fwd), pl.BlockSpec((rows, bcw), fwd), pl.BlockSpec((rows, CHUNK), fwd),
            pl.BlockSpec((rows, d_inner), bwd), pl.BlockSpec((rows, bcw), bwd), pl.BlockSpec((rows, CHUNK), bwd),
            pl.BlockSpec((1, 2, CHUNK), lambda b, c: (j, 0, 0)),
            pl.BlockSpec((1, 1, d_inner), lambda b, c: (j, 0, 0)),
            _resident(tril.shape), _resident(triu.shape), _resident(e.shape), _resident(e3.shape),
        ],
        out_specs=[pl.BlockSpec((rows, d_inner), fwd), pl.BlockSpec((rows, d_inner), bwd)],
        out_shape=[jax.ShapeDtypeStruct((nt, d_inner), BF16)] * 2,
        scratch_shapes=[pltpu.VMEM((D_STATE, d_inner), F32)] * 2,
        compiler_params=_params(("arbitrary", "arbitrary")),
        name=f"ssd_scan_{layer}",
    )(xs, bc, dtf, xs, bc, dtb, alog, dskip, tril, triu, e, e3)


def _ssd_out_kernel(nblk, nctx_blk, nbatch, d,
                    yf_ref, yb_ref, z_ref, h_ref, tbl_ref, nw_ref, wo_ref, nw2_ref, rw_ref, sl_ref,
                    o_ref, info_ref, cnt_ref, carry_ref):
    @pl.when(pl.program_id(0) == 0)
    def _():
        carry_ref[...] = jnp.zeros_like(carry_ref)

    y = (yf_ref[...].astype(F32) + yb_ref[...].astype(F32)) * _silu(z_ref[...].astype(F32))
    proj = _dot(_rms(y, nw_ref[0]).astype(BF16), wo_ref[0])
    for sb in range(TILE_BLKS):
        rows = slice(sb * BLK, (sb + 1) * BLK)
        row, _, _ = _block_row(nblk, nctx_blk, nbatch, pl.program_id(0) * TILE_BLKS + sb)
        gate = tbl_ref[0, pl.ds(row, 1), 2 * d:3 * d]
        out = h_ref[rows, :] + gate * proj[rows, :]
        o_ref[rows, :] = out
        info_ref[rows, :] = _route_block(out, row, d, tbl_ref, nw2_ref, rw_ref, sl_ref, carry_ref)
    cnt_ref[...] = jnp.broadcast_to(carry_ref[...], cnt_ref.shape)


def _ssd_out(yf, yb, z, h, tbl, layer, nw, wo, nw2, rw, strict_lower, geom):
    nt, d = h.shape
    d_inner = z.shape[1]
    nblk, nctx_blk, nbatch = geom
    j = layer // 2
    kern = functools.partial(_ssd_out_kernel, nblk, nctx_blk, nbatch, d)
    tm = TILE_BLKS * BLK
    big = pl.BlockSpec((tm, d_inner), lambda i: (i, 0))
    return pl.pallas_call(
        kern,
        grid=(nt // tm,),
        in_specs=[
            big, big, big,
            pl.BlockSpec((tm, d), lambda i: (i, 0)),
            pl.BlockSpec((1, 8, N_MOD * d), lambda i: (layer, 0, 0)),
            pl.BlockSpec((1, 1, d_inner), lambda i: (j, 0, 0)),
            pl.BlockSpec((1, d_inner, d), lambda i: (j, 0, 0), pipeline_mode=pl.Buffered(1)),
            pl.BlockSpec((1, 1, d), lambda i: (layer, 0, 0)),
            pl.BlockSpec((1, d, CHUNK), lambda i: (j, 0, 0)),
            _resident(strict_lower.shape),
        ],
        out_specs=[
            pl.BlockSpec((tm, d), lambda i: (i, 0)),
            pl.BlockSpec((tm, 8), lambda i: (i, 0)),
            pl.BlockSpec((8, CHUNK), lambda i: (0, 0)),
        ],
        out_shape=[jax.ShapeDtypeStruct((nt, d), F32), jax.ShapeDtypeStruct((nt, 8), F32),
                   jax.ShapeDtypeStruct((8, CHUNK), F32)],
        scratch_shapes=[pltpu.VMEM((1, CHUNK), F32)],
        compiler_params=_params(("arbitrary",)),
        name=f"ssd_out_{layer}",
    )(yf, yb, z, h, tbl, nw, wo, nw2, rw, strict_lower)


def _route_block(h, row, d, tbl_ref, nw_ref, rw_ref, sl_ref, carry_ref):
    shift = tbl_ref[0, pl.ds(row, 1), 3 * d:4 * d]
    scale = tbl_ref[0, pl.ds(row, 1), 4 * d:5 * d]
    v = _rms(h, nw_ref[0]) * (1.0 + scale) + shift
    lane = lax.broadcasted_iota(jnp.int32, (BLK, CHUNK), 1).astype(F32)
    logits = jnp.where(lane < N_EXPERTS, _dot_hi(v, rw_ref[0]), -jnp.inf)
    m1 = jnp.max(logits, axis=1, keepdims=True)
    i1 = jnp.min(jnp.where(logits == m1, lane, float(CHUNK)), axis=1, keepdims=True)
    rest = jnp.where(lane == i1, -jnp.inf, logits)
    m2 = jnp.max(rest, axis=1, keepdims=True)
    i2 = jnp.min(jnp.where(rest == m2, lane, float(CHUNK)), axis=1, keepdims=True)
    e2 = jnp.exp(m2 - m1)
    g1 = 1.0 / (1.0 + e2)
    g2 = e2 / (1.0 + e2)
    oh1 = (lane == i1)
    oh2 = (lane == i2)
    member = jnp.where(jnp.logical_or(oh1, oh2), 1.0, 0.0)
    before = carry_ref[...] + _dot(sl_ref[...], member.astype(BF16))
    r1 = jnp.sum(jnp.where(oh1, before, 0.0), axis=1, keepdims=True)
    r2 = jnp.sum(jnp.where(oh2, before, 0.0), axis=1, keepdims=True)
    carry_ref[...] = carry_ref[...] + jnp.sum(member, axis=0, keepdims=True)
    lane8 = lax.broadcasted_iota(jnp.int32, (BLK, 8), 1)
    return jnp.where(lane8 == 0, i1,
           jnp.where(lane8 == 1, i2,
           jnp.where(lane8 == 2, r1,
           jnp.where(lane8 == 3, r2,
           jnp.where(lane8 == 4, g1,
           jnp.where(lane8 == 5, g2, 0.0))))))


def _to_tiles(ref, x, rows):
    for k in range(SUB):
        ref[pl.ds(k, rows, stride=SUB), :] = x[:, k * 128:(k + 1) * 128]


def _from_tiles(ref, rows):
    return jnp.concatenate([ref[pl.ds(k, rows, stride=SUB), :] for k in range(SUB)], axis=1)


def _tile_rows(ref, p):
    return ref.at[pl.ds(pl.multiple_of(p * SUB, SUB), SUB), :]


def _dispatch_kernel(nblk, nctx_blk, nbatch, d,
                     ztile_ref, zvalid_ref, pos_ref, h_ref, tbl_ref, nw_ref, xs_ref, v_scr, z_scr, sem, zsem):
    @pl.when(pl.program_id(0) == 0)
    def _():
        z_scr[...] = jnp.zeros_like(z_scr)
        tile_rows = MOE_TILE * SUB
        for e in range(2 * N_EXPERTS):
            @pl.when(zvalid_ref[e] == 1)
            def _():
                first = pl.multiple_of(ztile_ref[e] * tile_rows, SUB)
                pltpu.make_async_copy(z_scr, xs_ref.at[pl.ds(first, tile_rows), :], zsem).start()
        for e in range(2 * N_EXPERTS):
            @pl.when(zvalid_ref[e] == 1)
            def _():
                pltpu.make_async_copy(z_scr, xs_ref.at[pl.ds(0, tile_rows), :], zsem).wait()

    step = pl.program_id(0)
    nsteps = pl.num_programs(0)
    slot = step % 2
    v_slot = v_scr.at[slot]

    def wait_block(s):
        def body(i, c):
            for _ in range(2 * DMA_UNROLL):
                pltpu.make_async_copy(v_scr.at[s, pl.ds(0, SUB), :], xs_ref.at[pl.ds(0, SUB), :], sem.at[s]).wait()
            return c
        lax.fori_loop(0, BLK // DMA_UNROLL, body, 0)

    @pl.when(step >= 2)
    def _():
        wait_block(slot)

    row, _, _ = _block_row(nblk, nctx_blk, nbatch)
    shift = tbl_ref[0, pl.ds(row, 1), 3 * d:4 * d]
    scale = tbl_ref[0, pl.ds(row, 1), 4 * d:5 * d]
    _to_tiles(v_slot, _rms(h_ref[...], nw_ref[0]) * (1.0 + scale) + shift, BLK)

    def start(i, c):
        for u in range(DMA_UNROLL):
            r = i * DMA_UNROLL + u
            for k in range(2):
                pltpu.make_async_copy(_tile_rows(v_slot, r), _tile_rows(xs_ref, pos_ref[0, 0, 2 * r + k]),
                                      sem.at[slot]).start(priority=k)
        return c

    lax.fori_loop(0, BLK // DMA_UNROLL, start, 0)

    @pl.when(step == nsteps - 1)
    def _():
        wait_block(slot)

        @pl.when(nsteps >= 2)
        def _():
            wait_block(1 - slot)


def _dispatch(h, tbl, layer, nw, pos, ztile, zvalid, nslots, geom):
    nt, d = h.shape
    nblk, nctx_blk, nbatch = geom
    kern = functools.partial(_dispatch_kernel, nblk, nctx_blk, nbatch, d)
    gs = pltpu.PrefetchScalarGridSpec(
        num_scalar_prefetch=2,
        grid=(nt // BLK,),
        in_specs=[
            pl.BlockSpec((1, 1, 2 * BLK), lambda i, zt, zv: (i, 0, 0), memory_space=pltpu.SMEM),
            pl.BlockSpec((BLK, d), lambda i, zt, zv: (i, 0)),
            pl.BlockSpec((1, 8, N_MOD * d), lambda i, zt, zv: (layer, 0, 0)),
            pl.BlockSpec((1, 1, d), lambda i, zt, zv: (layer, 0, 0)),
        ],
        out_specs=pl.BlockSpec(memory_space=pl.ANY),
        scratch_shapes=[pltpu.VMEM((2, BLK * SUB, 128), F32), pltpu.VMEM((MOE_TILE * SUB, 128), F32),
                        pltpu.SemaphoreType.DMA((2,)), pltpu.SemaphoreType.DMA(())],
    )
    return pl.pallas_call(
        kern,
        grid_spec=gs,
        out_shape=jax.ShapeDtypeStruct((nslots * SUB, 128), F32),
        compiler_params=_params(("arbitrary",)),
        name=f"dispatch_{layer}",
    )(ztile, zvalid, pos.reshape(nt // BLK, 1, 2 * BLK), h, tbl, nw)


def _expert_kernel(n_fchunk, te_ref, nu_ref, x_ref, w1_ref, w3_ref, w2_ref, o_ref):
    @pl.when(pl.program_id(0) >= nu_ref[0])
    def _():
        o_ref[...] = jnp.zeros_like(o_ref)

    @pl.when(pl.program_id(0) < nu_ref[0])
    def _():
        x = _from_tiles(x_ref, MOE_TILE).astype(BF16)
        acc = None
        for k in range(n_fchunk):
            sl = slice(k * MOE_FCHUNK, (k + 1) * MOE_FCHUNK)
            a = _dot(x, w1_ref[0, 0, :, sl])
            b = _dot(x, w3_ref[0, 0, :, sl])
            act = (_silu(a) * b).astype(BF16)
            part = _dot(act, w2_ref[0, 0, sl, :])
            acc = part if acc is None else acc + part
        _to_tiles(o_ref, acc, MOE_TILE)


def _experts(x_sorted, tile_expert, n_used, j, w1, w3, w2):
    d, dffe = w1.shape[-2:]
    n_tiles = x_sorted.shape[0] // (MOE_TILE * SUB)
    kern = functools.partial(_expert_kernel, dffe // MOE_FCHUNK)
    tile = lambda i, te, nu: (jnp.minimum(i, nu[0] - 1), 0)
    wspec = lambda shape: pl.BlockSpec((1, 1) + shape, lambda i, te, nu: (j, te[i], 0, 0))
    gs = pltpu.PrefetchScalarGridSpec(
        num_scalar_prefetch=2,
        grid=(n_tiles,),
        in_specs=[pl.BlockSpec((MOE_TILE * SUB, 128), tile), wspec((d, dffe)), wspec((d, dffe)), wspec((dffe, d))],
        out_specs=pl.BlockSpec((MOE_TILE * SUB, 128), lambda i, te, nu: (i, 0)),
    )
    return pl.pallas_call(
        kern,
        grid_spec=gs,
        out_shape=jax.ShapeDtypeStruct(x_sorted.shape, F32),
        compiler_params=_params(("arbitrary",)),
        name=f"experts_{j}",
    )(tile_expert, n_used, x_sorted, w1, w3, w2)


def _combine_kernel(row_of_step, d, final,
                    pos_ref, posn_ref, h_ref, info_ref, tbl_ref, y_ref, *rest):
    if final:
        fw_ref, o_ref, buf, sem = rest
    else:
        o_ref, buf, sem = rest
    step = pl.program_id(0)
    nsteps = pl.num_programs(0)
    slot = step % 2

    def gather(p_ref, s):
        def body(i, c):
            for u in range(DMA_UNROLL):
                r = i * DMA_UNROLL + u
                for k in range(2):
                    pltpu.make_async_copy(_tile_rows(y_ref, p_ref[0, 0, 2 * r + k]), _tile_rows(buf.at[s, k], r),
                                          sem.at[s]).start(priority=k)
            return c
        lax.fori_loop(0, BLK // DMA_UNROLL, body, 0)

    @pl.when(step == 0)
    def _():
        gather(pos_ref, 0)

    @pl.when(step + 1 < nsteps)
    def _():
        gather(posn_ref, 1 - slot)

    def wait(i, c):
        for _ in range(2 * DMA_UNROLL):
            pltpu.make_async_copy(y_ref.at[pl.ds(0, SUB), :], buf.at[slot, 0, pl.ds(0, SUB), :], sem.at[slot]).wait()
        return c

    lax.fori_loop(0, BLK // DMA_UNROLL, wait, 0)
    info = info_ref[...]
    g1 = info[:, 4:5]
    g2 = info[:, 5:6]
    gate = tbl_ref[0, pl.ds(row_of_step(step), 1), 5 * d:6 * d]
    out = h_ref[...] + gate * (g1 * _from_tiles(buf.at[slot, 0], BLK) + g2 * _from_tiles(buf.at[slot, 1], BLK))
    if final:
        o_ref[0] = _rms(out, fw_ref[...])
    else:
        o_ref[...] = out


def _combine(h, info, tbl, layer, y_sorted, pos, geom, final_w=None):
    nt, d = h.shape
    nblk, nctx_blk, nbatch = geom
    final = final_w is not None
    pos3 = pos.reshape(nt // BLK, 1, 2 * BLK)
    if final:
        nlat = nblk - nctx_blk
        nsteps = nbatch * nlat
        blk = lambda i: (i // nlat) * nblk + nctx_blk + i % nlat
        row_of_step = lambda step: step // nlat
        out_spec = pl.BlockSpec((1, BLK, d), lambda i: (i // nlat, i % nlat, 0))
        out_shape = jax.ShapeDtypeStruct((nbatch, nlat * BLK, d), F32)
    else:
        nsteps = nt // BLK
        blk = lambda i: i
        row_of_step = lambda step: _block_row(nblk, nctx_blk, nbatch, step)[0]
        out_spec = pl.BlockSpec((BLK, d), lambda i: (i, 0))
        out_shape = jax.ShapeDtypeStruct((nt, d), F32)
    nxt = lambda i: blk(jnp.minimum(i + 1, nsteps - 1))
    in_specs = [
        pl.BlockSpec((1, 1, 2 * BLK), lambda i: (blk(i), 0, 0), memory_space=pltpu.SMEM),
        pl.BlockSpec((1, 1, 2 * BLK), lambda i: (nxt(i), 0, 0), memory_space=pltpu.SMEM),
        pl.BlockSpec((BLK, d), lambda i: (blk(i), 0)),
        pl.BlockSpec((BLK, 8), lambda i: (blk(i), 0)),
        pl.BlockSpec((1, 8, N_MOD * d), lambda i: (layer, 0, 0)),
        pl.BlockSpec(memory_space=pl.ANY),
    ]
    args = [pos3, pos3, h, info, tbl, y_sorted]
    if final:
        in_specs.append(pl.BlockSpec((1, d), lambda i: (0, 0)))
        args.append(final_w.reshape(1, d))
    return pl.pallas_call(
        functools.partial(_combine_kernel, row_of_step, d, final),
        grid=(nsteps,),
        in_specs=in_specs,
        out_specs=out_spec,
        out_shape=out_shape,
        scratch_shapes=[pltpu.VMEM((2, 2, BLK * SUB, 128), F32), pltpu.SemaphoreType.DMA((2,))],
        compiler_params=_params(("arbitrary",)),
        name=f"combine_{layer}",
    )(*args)


def _moe_layer(h, info, counts, tbl, layer, nw, w1, w3, w2, geom, final_w=None):
    nt, d = h.shape
    j = layer // 2
    cnt = counts[0, :N_EXPERTS].astype(jnp.int32)
    tiles = (cnt + MOE_TILE - 1) // MOE_TILE
    tile_end = jnp.cumsum(tiles)
    offs = (tile_end - tiles) * MOE_TILE
    n_used = tile_end[-1]
    idx = info[:, 0:2].astype(jnp.int32)
    group_start = jnp.sum(jnp.where(idx[..., None] == jnp.arange(N_EXPERTS), offs, 0), axis=-1)
    pos = (group_start + info[:, 2:4].astype(jnp.int32)).reshape(-1)
    n_tiles = (2 * nt) // MOE_TILE + N_EXPERTS
    t = jnp.minimum(jnp.arange(n_tiles, dtype=jnp.int32), n_used - 1)
    tile_expert = jnp.sum(t[:, None] >= tile_end[None, :], axis=1).astype(jnp.int32)
    spare = n_used + jnp.arange(N_EXPERTS, dtype=jnp.int32)
    ztile = jnp.concatenate([tile_end - 1, spare]).astype(jnp.int32)
    zvalid = jnp.concatenate([tiles > 0, spare < n_tiles]).astype(jnp.int32)
    x_sorted = _dispatch(h, tbl, layer, nw, pos, ztile, zvalid, n_tiles * MOE_TILE, geom)
    y_sorted = _experts(x_sorted, tile_expert, n_used.reshape(1).astype(jnp.int32), j, w1, w3, w2)
    return _combine(h, info, tbl, layer, y_sorted, pos, geom, final_w)


def _final_kernel(h_ref, w_ref, o_ref):
    o_ref[0] = _rms(h_ref[...], w_ref[...])


def _final_norm(h, w, nbatch, nblk, nctx_blk):
    nt, d = h.shape
    nlat = nblk - nctx_blk
    return pl.pallas_call(
        _final_kernel,
        grid=(nbatch, nlat),
        in_specs=[
            pl.BlockSpec((BLK, d), lambda b, j: (b * nblk + nctx_blk + j, 0)),
            pl.BlockSpec((1, d), lambda b, j: (0, 0)),
        ],
        out_specs=pl.BlockSpec((1, BLK, d), lambda b, j: (b, j, 0)),
        out_shape=jax.ShapeDtypeStruct((nbatch, nlat * BLK, d), F32),
        compiler_params=_params(("arbitrary", "arbitrary")),
        name="final_norm",
    )(h, w.reshape(1, d))


def kernel(x, c, ctx, c_ctx, ada_w, ada_b, norm_mix_w, norm_ffn_w, pool_w, pool_scale, ssd_in_w, ssd_conv_w, ssd_conv_b, ssd_A_log, ssd_dt_bias, ssd_D, ssd_norm_w, ssd_out_w, ffn_w1, ffn_w3, ffn_w2, moe_router_w, moe_w1, moe_w3, moe_w2, final_norm_w):
    nbatch, seq, d = x.shape
    ctx_len = ctx.shape[1]
    depth = ada_w.shape[0]
    d_inner = ssd_norm_w.shape[-1]
    assert ctx_len % BLK == 0 and seq % BLK == 0 and nbatch < 8
    assert d_inner == N_HEADS * HEAD_DIM and d % len(POOL_WINDOWS) == 0
    nblk = (ctx_len + seq) // BLK
    nctx_blk = ctx_len // BLK
    geom = (nblk, nctx_blk, nbatch)
    nchunk = (ctx_len + seq) // CHUNK
    ncc = ctx_len // CHUNK

    h = (ctx, x)
    cvec = jnp.zeros((8, d), F32).at[:nbatch].set(c).at[nbatch].set(c_ctx)
    tbl = _mod_table(cvec, ada_w, ada_b)

    vec3 = lambda a: a.reshape(a.shape[0], 1, a.shape[-1])
    nmix = vec3(norm_mix_w)
    nffn = vec3(norm_ffn_w)
    acat = _pool_matrices()
    scan_consts = _scan_constants()
    strict_lower = jnp.asarray(np.tril(np.ones((BLK, BLK), np.float32), -1), BF16)

    conv_dim = ssd_conv_w.shape[-1]
    wz = ssd_in_w[:, :, :d_inner].astype(BF16)
    wx = ssd_in_w[:, :, d_inner:d_inner + conv_dim].astype(BF16)
    wdt = ssd_in_w[:, :, d_inner + conv_dim:]
    pad_dt = lambda w: jnp.pad(w, ((0, 0), (0, 0), (0, CHUNK - N_HEADS))).astype(BF16)
    wdf = pad_dt(wdt[:, :, :N_HEADS])
    wdb = pad_dt(wdt[:, :, N_HEADS:])
    pad_h = lambda a: jnp.pad(a, ((0, 0), (0, 0), (0, CHUNK - N_HEADS)))
    dtb = pad_h(ssd_dt_bias)
    alog = pad_h(ssd_A_log)
    dskip = vec3(jnp.repeat(ssd_D, HEAD_DIM, axis=-1))
    rw = jnp.pad(moe_router_w, ((0, 0), (0, 0), (0, CHUNK - N_EXPERTS)))
    pool_wb = pool_w.astype(BF16)
    ffn = [w.astype(BF16) for w in (ffn_w1, ffn_w3, ffn_w2)]
    moe = [w.astype(BF16) for w in (moe_w1, moe_w3, moe_w2)]
    wout = ssd_out_w.astype(BF16)

    for i in range(depth):
        if i % 2 == 0:
            h = _even_layer(h, tbl, i, nmix, nffn, acat, pool_wb, vec3(pool_scale), *ffn, geom)
        else:
            z, xs, bc, dtf, dtbw = _ssd_in(h, tbl, i, nmix, wz, wx, wdf, wdb, ssd_conv_w,
                                           vec3(ssd_conv_b), dtb, geom)
            yf, yb = _ssd_scan(xs, bc, dtf, dtbw, alog, dskip, i, scan_consts, nbatch, nchunk, ncc)
            h, info, counts = _ssd_out(yf, yb, z, h, tbl, i, vec3(ssd_norm_w), wout, nffn, rw, strict_lower, geom)
            fin = final_norm_w if (i == depth - 1) else None
            h = _moe_layer(h, info, counts, tbl, i, nffn, *moe, geom, fin)
    if depth % 2 == 1:
        h = _final_norm(h, final_norm_w, nbatch, nblk, nctx_blk)
    return h
```
